```python
import jax, jax.numpy as jnp
from jax import lax
import numpy as np

D_MODEL = 2048
BATCH = 32
SEQ = 256
DEPTH = 4
DEC_BATCH = 4
DEC_SEQ = 4096
PAST_LEN = 256

GRID_W = 64
N_MIXERS = 2
EXPAND = 2
D_INNER = EXPAND * D_MODEL
N_LRU = (DEPTH + 1) // 2
N_RWKV = DEPTH // 2
LRU_BLOCKS = 16
LRU_BLOCK = D_INNER // LRU_BLOCKS
CONV_W = 4
CONV_LEFT = 2
LRU_C = 8.0
RWKV_HEAD = 64
RWKV_HEADS = D_INNER // RWKV_HEAD
LORA_DECAY = 128
LORA_A = 128
LORA_V = 96
NORM_EPS = 1e-6
GN_EPS = 64e-5

kernel_name = 'hybrid_rglru_rwkv7_diffusion_step'


def rmsnorm(x, g):
    xf = x.astype(jnp.float32)
    y = xf * lax.rsqrt(jnp.mean(xf * xf, axis=-1, keepdims=True) + NORM_EPS)
    return (y * g.astype(jnp.float32)).astype(x.dtype)


def adaln(cond, w, b):
    m = (jax.nn.silu(cond) @ w + b)[:, None, :]
    return jnp.split(m, 3, axis=-1)


def shift_sequence(x):
    half = x.shape[-1] // 2
    prev = jnp.pad(x[:, :-1, :half], ((0, 0), (1, 0), (0, 0)))
    nxt = jnp.pad(x[:, 1:, half:], ((0, 0), (0, 1), (0, 0)))
    return jnp.concatenate([prev, nxt], axis=-1)


def shift_grid(x):
    bsz, t, d = x.shape
    rows = t // GRID_W
    q = d // 4
    g = x.reshape(bsz, rows, GRID_W, d)
    left = jnp.pad(g[:, :, :-1, :q], ((0, 0), (0, 0), (1, 0), (0, 0)))
    right = jnp.pad(g[:, :, 1:, q:2 * q], ((0, 0), (0, 0), (0, 1), (0, 0)))
    up = jnp.pad(g[:, :-1, :, 2 * q:3 * q], ((0, 0), (1, 0), (0, 0), (0, 0)))
    down = jnp.pad(g[:, 1:, :, 3 * q:], ((0, 0), (0, 1), (0, 0), (0, 0)))
    return jnp.concatenate([left, right, up, down], axis=-1).reshape(bsz, t, d)


def conv_centred(x, w, b):
    t = x.shape[1]
    xp = jnp.pad(x, ((0, 0), (CONV_LEFT, CONV_W - 1 - CONV_LEFT), (0, 0)))
    y = b
    for tap in range(CONV_W):
        y = y + xp[:, tap:tap + t] * w[tap]
    return y


def lru_scan(a, u, h0, reverse):
    def step(h, inp):
        at, ut = inp
        h = at * h + ut
        return h, h
    h_final, hs = lax.scan(step, h0, (jnp.swapaxes(a, 0, 1), jnp.swapaxes(u, 0, 1)), reverse=reverse)
    return jnp.swapaxes(hs, 0, 1), h_final


def lru_mixer(h, h0, P, j):
    bsz, t, _ = h.shape
    x, z = jnp.split(h @ P['lru_w_in'][j], 2, axis=-1)
    x = conv_centred(x, P['lru_conv_w'][j], P['lru_conv_b'][j])
    xb = x.reshape(bsz, t, LRU_BLOCKS, LRU_BLOCK)
    xf = x.astype(jnp.float32)
    ys, finals = [], []
    for d in range(2):
        gw, gb = P['lru_gate_w'][j, d], P['lru_gate_b'][j, d]
        r = jax.nn.sigmoid(jnp.einsum('btnc,nce->btne', xb, gw[0]).reshape(bsz, t, D_INNER) + gb[0])
        i = jax.nn.sigmoid(jnp.einsum('btnc,nce->btne', xb, gw[1]).reshape(bsz, t, D_INNER) + gb[1])
        log_a = -LRU_C * r.astype(jnp.float32) * jax.nn.softplus(-P['lru_lambda'][j, d].astype(jnp.float32))
        u = jnp.sqrt(-jnp.expm1(2.0 * log_a)) * (i.astype(jnp.float32) * xf)
        y_d, h_d = lru_scan(jnp.exp(log_a), u, h0[d].astype(jnp.float32), reverse=(d == 1))
        ys.append(y_d)
        finals.append(h_d)
    y = (ys[0] + ys[1]).astype(h.dtype)
    out = (y * jax.nn.silu(z)) @ P['lru_w_out'][j]
    return out, jnp.stack(finals, 0)


def to_heads(t):
    return t.reshape(t.shape[:-1] + (RWKV_HEADS, RWKV_HEAD)).astype(jnp.float32)


def rwkv_scan(r, w, k, v, a, b, s0, reverse):
    def step(s, inp):
        rt, wt, kt, vt, at, bt = inp
        sa = jnp.einsum('bhvk,bhk->bhv', s, at)
        s = s * wt[:, :, None, :] + sa[..., None] * bt[:, :, None, :] + vt[..., None] * kt[:, :, None, :]
        return s, jnp.einsum('bhvk,bhk->bhv', s, rt)
    xs = tuple(jnp.swapaxes(q, 0, 1) for q in (r, w, k, v, a, b))
    s_final, ys = lax.scan(step, s0, xs, reverse=reverse)
    return jnp.swapaxes(ys, 0, 1), s_final


def rwkv_mixer(h, s0, shift_fn, v_first, P, j):
    bsz, t, _ = h.shape
    mu = P['rwkv_mu'][j]
    xx = shift_fn(h) - h
    xr, xw, xk, xv, xa = [h + xx * mu[m] for m in range(5)]
    r = xr @ P['rwkv_w_r'][j]
    k = xk @ P['rwkv_w_k'][j]
    v = xv @ P['rwkv_w_v'][j]
    z = h @ P['rwkv_w_g'][j]
    if j > 0:
        v_mix = jax.nn.sigmoid(P['rwkv_v0'][j - 1] + (xv @ P['rwkv_v1'][j - 1]) @ P['rwkv_v2'][j - 1])
        v = v + (v_first - v) * v_mix
    rh, kh, vh = to_heads(r), to_heads(k), to_heads(v)
    kk = kh * to_heads(P['rwkv_k_k'][j])
    kk = kk / jnp.maximum(jnp.sqrt(jnp.sum(kk * kk, axis=-1, keepdims=True)), 1e-12)
    k_a = to_heads(P['rwkv_k_a'][j])
    r_k = to_heads(P['rwkv_r_k'][j])
    ys, bonus, finals = [], [], []
    for d in range(2):
        w_pre = to_heads(P['rwkv_w0'][j, d] + jnp.tanh(xw @ P['rwkv_w1'][j, d]) @ P['rwkv_w2'][j, d])
        w_log = -jax.nn.softplus(-w_pre) - 0.5
        decay = jnp.exp(-jnp.exp(w_log))
        a = jax.nn.sigmoid(to_heads(P['rwkv_a0'][j, d] + (xa @ P['rwkv_a1'][j, d]) @ P['rwkv_a2'][j, d]))
        kd = kh * (1.0 + (a - 1.0) * k_a)
        y_d, s_d = rwkv_scan(rh, decay, kd, vh, -kk, kk * a, s0[d].astype(jnp.float32), reverse=(d == 1))
        ys.append(y_d)
        bonus.append(jnp.sum(rh * kd * r_k, axis=-1, keepdims=True) * vh)
        finals.append(s_d)
    y = ys[0] + ys[1]
    mean = jnp.mean(y, axis=-1, keepdims=True)
    var = jnp.mean(jnp.square(y - mean), axis=-1, keepdims=True)
    y = (y - mean) * lax.rsqrt(var + GN_EPS) * to_heads(P['rwkv_ln_w'][j]) + to_heads(P['rwkv_ln_b'][j])
    y = (y + bonus[0] + bonus[1]).reshape(bsz, t, D_INNER).astype(h.dtype)
    out = (y * jax.nn.silu(z)) @ P['rwkv_w_o'][j]
    return out, jnp.stack(finals, 0), v


def trunk(x, cond, init_lru, init_rwkv, shift_fn, P):
    fin_lru, fin_rwkv = [], []
    v_first = None
    for i in range(DEPTH):
        shift, scale, gate = adaln(cond, P['ada_w'][i], P['ada_b'][i])
        h = rmsnorm(x, P['norm_pre'][i]) * (1 + scale) + shift
        j = i // N_MIXERS
        if i % N_MIXERS == 0:
            m, fin = lru_mixer(h, init_lru[j], P, j)
            fin_lru.append(fin)
        else:
            m, fin, v = rwkv_mixer(h, init_rwkv[j], shift_fn, v_first, P, j)
            if j == 0:
                v_first = v
            fin_rwkv.append(fin)
        x = x + gate * rmsnorm(m, P['norm_post'][i])
    return x, jnp.stack(fin_lru, 0), jnp.stack(fin_rwkv, 0)


def setup_inputs(seed: int = 0) -> dict:
    key = jax.random.key(seed)
    keys = jax.random.split(key, 48)
    ks = iter([keys[n] for n in range(48)])
    f32 = jnp.float32
    E, H, N, D = D_INNER, RWKV_HEADS, RWKV_HEAD, D_MODEL

    def nrm(shape, s):
        return jax.random.normal(next(ks), shape, f32) * s

    def uni(shape, lo, hi):
        return jax.random.uniform(next(ks), shape, f32, lo, hi)

    inp = {}
    inp['x_prompt'] = nrm((BATCH, SEQ, D), 1.0)
    inp['x_sample'] = nrm((DEC_BATCH, DEC_SEQ, D), 1.0)
    inp['state_lru'] = nrm((DEC_BATCH, N_LRU, 2, E), 0.5)
    inp['state_rwkv'] = nrm((DEC_BATCH, N_RWKV, 2, H, N, N), 0.3)
    inp['c'] = nrm((DEC_BATCH, D), 1.0)
    inp['c_ctx'] = nrm((D,), 1.0)
    inp['ada_w'] = nrm((DEPTH, D, 3 * D), D ** -0.5)
    inp['ada_b'] = nrm((DEPTH, 3 * D), 0.02)
    inp['norm_pre'] = 1.0 + nrm((DEPTH, D), 0.05)
    inp['norm_post'] = 1.0 + nrm((DEPTH, D), 0.05)
    inp['lru_w_in'] = nrm((N_LRU, D, 2 * E), D ** -0.5)
    inp['lru_conv_w'] = nrm((N_LRU, CONV_W, E), CONV_W ** -0.5)
    inp['lru_conv_b'] = nrm((N_LRU, E), 0.02)
    inp['lru_gate_w'] = nrm((N_LRU, 2, 2, LRU_BLOCKS, LRU_BLOCK, LRU_BLOCK), LRU_BLOCK ** -0.5)
    inp['lru_gate_b'] = nrm((N_LRU, 2, 2, E), 0.02)
    a_c = uni((N_LRU, 2, E), 0.9, 0.999)
    a_base = a_c ** (1.0 / LRU_C)
    inp['lru_lambda'] = jnp.log(a_base) - jnp.log1p(-a_base)
    inp['lru_w_out'] = nrm((N_LRU, E, D), E ** -0.5)
    inp['rwkv_mu'] = uni((N_RWKV, 5, D), 0.0, 1.0)
    inp['rwkv_w_r'] = nrm((N_RWKV, D, E), D ** -0.5)
    inp['rwkv_w_k'] = nrm((N_RWKV, D, E), D ** -0.5)
    inp['rwkv_w_v'] = nrm((N_RWKV, D, E), D ** -0.5)
    inp['rwkv_w_g'] = nrm((N_RWKV, D, E), D ** -0.5)
    inp['rwkv_w_o'] = nrm((N_RWKV, E, D), E ** -0.5)
    inp['rwkv_w0'] = uni((N_RWKV, 2, E), -5.0, 1.0)
    inp['rwkv_w1'] = nrm((N_RWKV, 2, D, LORA_DECAY), D ** -0.5)
    inp['rwkv_w2'] = nrm((N_RWKV, 2, LORA_DECAY, E), 0.5 * LORA_DECAY ** -0.5)
    inp['rwkv_a0'] = nrm((N_RWKV, 2, E), 0.1)
    inp['rwkv_a1'] = nrm((N_RWKV, 2, D, LORA_A), D ** -0.5)
    inp['rwkv_a2'] = nrm((N_RWKV, 2, LORA_A, E), 0.5 * LORA_A ** -0.5)
    inp['rwkv_k_k'] = 0.85 + nrm((N_RWKV, E), 0.05)
    inp['rwkv_k_a'] = 1.0 + nrm((N_RWKV, E), 0.05)
    inp['rwkv_r_k'] = nrm((N_RWKV, E), 0.1)
    inp['rwkv_ln_w'] = 1.0 + nrm((N_RWKV, E), 0.05)
    inp['rwkv_ln_b'] = nrm((N_RWKV, E), 0.02)
    inp['rwkv_v0'] = nrm((N_RWKV - 1, E), 0.1)
    inp['rwkv_v1'] = nrm((N_RWKV - 1, D, LORA_V), D ** -0.5)
    inp['rwkv_v2'] = nrm((N_RWKV - 1, LORA_V, E), 0.5 * LORA_V ** -0.5)
    return inp


def reference(x_prompt, x_sample, state_lru, state_rwkv, c, c_ctx,
              ada_w, ada_b, norm_pre, norm_post,
              lru_w_in, lru_conv_w, lru_conv_b, lru_gate_w, lru_gate_b, lru_lambda, lru_w_out,
              rwkv_mu, rwkv_w_r, rwkv_w_k, rwkv_w_v, rwkv_w_g, rwkv_w_o,
              rwkv_w0, rwkv_w1, rwkv_w2, rwkv_a0, rwkv_a1, rwkv_a2,
              rwkv_k_k, rwkv_k_a, rwkv_r_k, rwkv_ln_w, rwkv_ln_b,
              rwkv_v0, rwkv_v1, rwkv_v2):
    P = dict(ada_w=ada_w, ada_b=ada_b, norm_pre=norm_pre, norm_post=norm_post,
             lru_w_in=lru_w_in, lru_conv_w=lru_conv_w, lru_conv_b=lru_conv_b,
             lru_gate_w=lru_gate_w, lru_gate_b=lru_gate_b, lru_lambda=lru_lambda, lru_w_out=lru_w_out,
             rwkv_mu=rwkv_mu, rwkv_w_r=rwkv_w_r, rwkv_w_k=rwkv_w_k, rwkv_w_v=rwkv_w_v,
             rwkv_w_g=rwkv_w_g, rwkv_w_o=rwkv_w_o, rwkv_w0=rwkv_w0, rwkv_w1=rwkv_w1, rwkv_w2=rwkv_w2,
             rwkv_a0=rwkv_a0, rwkv_a1=rwkv_a1, rwkv_a2=rwkv_a2, rwkv_k_k=rwkv_k_k, rwkv_k_a=rwkv_k_a,
             rwkv_r_k=rwkv_r_k, rwkv_ln_w=rwkv_ln_w, rwkv_ln_b=rwkv_ln_b,
             rwkv_v0=rwkv_v0, rwkv_v1=rwkv_v1, rwkv_v2=rwkv_v2)
    bp = x_prompt.shape[0]
    zero_lru = jnp.zeros((N_LRU, 2, bp, D_INNER), jnp.float32)
    zero_rwkv = jnp.zeros((N_RWKV, 2, bp, RWKV_HEADS, RWKV_HEAD, RWKV_HEAD), jnp.float32)
    y_prompt, fin_lru, fin_rwkv = trunk(x_prompt, c_ctx[None, :], zero_lru, zero_rwkv, shift_sequence, P)
    new_state_lru = jnp.moveaxis(fin_lru, 2, 0)
    new_state_rwkv = jnp.moveaxis(fin_rwkv, 2, 0)
    init_lru = jnp.moveaxis(state_lru, 0, 2)
    init_rwkv = jnp.moveaxis(state_rwkv, 0, 2)
    y_sample, _, _ = trunk(x_sample, c, init_lru, init_rwkv, shift_grid, P)
    return (y_prompt, y_sample, new_state_lru, new_state_rwkv)
```

```python
import functools

import jax
import jax.numpy as jnp
from jax import lax
from jax.experimental import pallas as pl
from jax.experimental.pallas import tpu as pltpu

F32 = jnp.float32
BF = jnp.bfloat16

NORM_EPS = 1e-6
GN_EPS = 64e-5
LRU_C = 8.0
GRID_W = 64
CONV_W = 4
CONV_LEFT = 2

HD = 64
GH = 4
GW = GH * HD
CH = 64
INV_BASE = 16
LB = 256
SUBLANES = 8
VMEM_LIMIT = 56 * 1024 * 1024


def _cparams(*sem):
    return pltpu.CompilerParams(dimension_semantics=sem, vmem_limit_bytes=VMEM_LIMIT)


def _dot(a, b):
    return jnp.dot(a, b, preferred_element_type=F32)


def _dot_nt(a, b):
    return lax.dot_general(a, b, (((1,), (1,)), ((), ())), preferred_element_type=F32)


def _dot_tn(a, b):
    return lax.dot_general(a, b, (((0,), (0,)), ((), ())), preferred_element_type=F32)


def _split3(x):
    h1 = x.astype(BF)
    r1 = x - h1.astype(F32)
    h2 = r1.astype(BF)
    h3 = (r1 - h2.astype(F32)).astype(BF)
    return h1, h2, h3


def _modnorm(x, g, mod):
    y = x * lax.rsqrt(jnp.mean(x * x, axis=-1, keepdims=True) + NORM_EPS) * g
    return y * (1.0 + mod[1:2]) + mod[0:1]


def _mm_body(*refs, act, a_pre, n_extra):
    a_ref, w_ref, b_ref = refs[:3]
    extras = refs[3:3 + n_extra]
    o_ref = refs[3 + n_extra]
    a = a_ref[...]
    if a_pre is not None:
        a = a_pre(a.astype(F32))
    acc = _dot(a.astype(BF), w_ref[...]) + b_ref[...]
    if act is not None:
        acc = act(acc, *[e[...] for e in extras])
    o_ref[...] = acc.astype(o_ref.dtype)


def _mm(a, w3, li, bias, *, tm, tn, m_tiles, a_imap=None, o_shape=None, o_imap=None,
        extras=(), act=None, a_pre=None, out_dtype=F32, name="mm"):
    _, K, N = w3.shape
    tn = min(tn, N)
    assert N % tn == 0 and K % 128 == 0
    n_tiles = N // tn
    if a_imap is None:
        a_imap = lambda n, m: (m, 0)
    if o_imap is None:
        o_imap = lambda n, m: (m, n)
    if o_shape is None:
        o_shape = (m_tiles * tm, N)
    if bias is None:
        bias = jnp.zeros((1, N), F32)
    in_specs = [
        pl.BlockSpec((tm, K), a_imap),
        pl.BlockSpec((None, K, tn), lambda n, m: (li, 0, n)),
        pl.BlockSpec((1, tn), lambda n, m: (0, n)),
    ] + [pl.BlockSpec((tm, tn), lambda n, m: (m, n)) for _ in extras]
    return pl.pallas_call(
        functools.partial(_mm_body, act=act, a_pre=a_pre, n_extra=len(extras)),
        out_shape=jax.ShapeDtypeStruct(o_shape, out_dtype),
        grid=(n_tiles, m_tiles),
        in_specs=in_specs,
        out_specs=pl.BlockSpec((tm, tn), o_imap),
        compiler_params=_cparams("parallel", "parallel"),
        name=name,
    )(a, w3, bias, *extras)


def _prenorm_body(x_ref, g_ref, mod_ref, o_ref):
    o_ref[...] = _modnorm(x_ref[...], g_ref[...], mod_ref[0]).astype(o_ref.dtype)


def _prenorm(x, g3, layer, mod, *, tm, rows_per_cond):
    rows, d = x.shape
    return pl.pallas_call(
        _prenorm_body,
        out_shape=jax.ShapeDtypeStruct((rows, d), BF),
        grid=(rows // tm,),
        in_specs=[
            pl.BlockSpec((tm, d), lambda i: (i, 0)),
            pl.BlockSpec((None, 1, d), lambda i: (layer, 0, 0)),
            pl.BlockSpec((1, 3, d), lambda i: ((i * tm) // rows_per_cond, 0, 0)),
        ],
        out_specs=pl.BlockSpec((tm, d), lambda i: (i, 0)),
        compiler_params=_cparams("parallel"),
        name="prenorm",
    )(x, g3, mod)


def _postnorm_body(x_ref, m_ref, g_ref, mod_ref, o_ref):
    m = m_ref[...]
    y = m * lax.rsqrt(jnp.mean(m * m, axis=-1, keepdims=True) + NORM_EPS) * g_ref[...]
    o_ref[...] = x_ref[...] + mod_ref[0][2:3] * y


def _postnorm(x, m, g3, layer, mod, *, tm, rows_per_cond):
    rows, d = x.shape
    return pl.pallas_call(
        _postnorm_body,
        out_shape=jax.ShapeDtypeStruct((rows, d), F32),
        grid=(rows // tm,),
        in_specs=[
            pl.BlockSpec((tm, d), lambda i: (i, 0)),
            pl.BlockSpec((tm, d), lambda i: (i, 0)),
            pl.BlockSpec((None, 1, d), lambda i: (layer, 0, 0)),
            pl.BlockSpec((1, 3, d), lambda i: ((i * tm) // rows_per_cond, 0, 0)),
        ],
        out_specs=pl.BlockSpec((tm, d), lambda i: (i, 0)),
        compiler_params=_cparams("parallel"),
        name="postnorm",
    )(x, m, g3, mod)


def _write_mixes(h, shifted, mu_ref, out_refs):
    xx = shifted - h
    out_refs[0][...] = h.astype(BF)
    for m in range(5):
        out_refs[m + 1][...] = (h + xx * mu_ref[m:m + 1]).astype(BF)


def _shiftmix_seq_body(x_ref, g_ref, mod_ref, mu_ref, *out_refs):
    h = _modnorm(x_ref[...], g_ref[...], mod_ref[0])
    t_len, d = h.shape
    half = d // 2
    t = lax.broadcasted_iota(jnp.int32, (t_len, 1), 0)
    prev = jnp.where(t == 0, 0.0, pltpu.roll(h[:, :half], 1, 0))
    nxt = jnp.where(t == t_len - 1, 0.0, pltpu.roll(h[:, half:], t_len - 1, 0))
    _write_mixes(h, jnp.concatenate([prev, nxt], axis=1), mu_ref, out_refs)


def _shiftmix_grid_body(x_ref, up_ref, dn_ref, g_ref, mod_ref, mu_ref, *out_refs, tiles_per_batch):
    ti = pl.program_id(0) % tiles_per_batch
    g = g_ref[...]
    mod = mod_ref[0]
    h = _modnorm(x_ref[...], g, mod)
    tr, d = h.shape
    q = d // 4
    h_up = jnp.where(ti > 0, _modnorm(up_ref[...], g, mod)[:, 2 * q:3 * q], 0.0)
    h_dn = jnp.where(ti < tiles_per_batch - 1, _modnorm(dn_ref[...], g, mod)[:, 3 * q:], 0.0)
    col = lax.broadcasted_iota(jnp.int32, (tr, 1), 0) % GRID_W
    left = jnp.where(col == 0, 0.0, pltpu.roll(h[:, :q], 1, 0))
    right = jnp.where(col == GRID_W - 1, 0.0, pltpu.roll(h[:, q:2 * q], tr - 1, 0))
    up = jnp.concatenate([h_up, h[:tr - GRID_W, 2 * q:3 * q]], axis=0)
    down = jnp.concatenate([h[GRID_W:, 3 * q:], h_dn], axis=0)
    _write_mixes(h, jnp.concatenate([left, right, up, down], axis=1), mu_ref, out_refs)


def _shiftmix(x, g3, layer, mod, mu, *, seq_len, cond_rows, grid_mode, tm):
    rows, d = x.shape
    outs = [jax.ShapeDtypeStruct((rows, d), BF)] * 6
    common = [
        pl.BlockSpec((None, 1, d), lambda i: (layer, 0, 0)),
        pl.BlockSpec((1, 3, d), lambda i: ((i * tm) // cond_rows, 0, 0)),
        pl.BlockSpec((5, d), lambda i: (0, 0)),
    ]
    out_specs = [pl.BlockSpec((tm, d), lambda i: (i, 0))] * 6
    if not grid_mode:
        assert tm == seq_len
        return pl.pallas_call(
            _shiftmix_seq_body, out_shape=outs, grid=(rows // tm,),
            in_specs=[pl.BlockSpec((tm, d), lambda i: (i, 0))] + common,
            out_specs=out_specs, compiler_params=_cparams("parallel"), name="shiftmix_seq",
        )(x, g3, mod, mu)
    per = tm // GRID_W
    n_halo = rows // GRID_W
    return pl.pallas_call(
        functools.partial(_shiftmix_grid_body, tiles_per_batch=seq_len // tm),
        out_shape=outs, grid=(rows // tm,),
        in_specs=[
            pl.BlockSpec((tm, d), lambda i: (i, 0)),
            pl.BlockSpec((GRID_W, d), lambda i: (jnp.maximum(i * per - 1, 0), 0)),
            pl.BlockSpec((GRID_W, d), lambda i: (jnp.minimum((i + 1) * per, n_halo - 1), 0)),
        ] + common,
        out_specs=out_specs, compiler_params=_cparams("parallel"), name="shiftmix_grid",
    )(x, x, x, g3, mod, mu)


def _lru_body(*refs, rb, reverse, combine, n_tiles):
    if combine:
        (x_ref, xp_ref, xn_ref, cw_ref, cb_ref, gw_ref, gb_ref, lam_ref, h0_ref, yp_ref, z_ref,
         y_ref, hf_ref, xs, a_s, u_s, hc) = refs
    else:
        (x_ref, xp_ref, xn_ref, cw_ref, cb_ref, gw_ref, gb_ref, lam_ref, h0_ref,
         y_ref, hf_ref, xs, a_s, u_s, hc) = refs
    tr = x_ref.shape[0]
    hb = xp_ref.shape[0]
    step = pl.program_id(1)
    t_idx = (n_tiles - 1 - step) if reverse else step

    @pl.when(step == 0)
    def _():
        hc[...] = h0_ref[...]

    xs[hb:hb + tr, :] = x_ref[...]
    xs[0:hb, :] = jnp.where(t_idx > 0, xp_ref[...], 0.0)
    xs[hb + tr:hb + tr + hb, :] = jnp.where(t_idx < n_tiles - 1, xn_ref[...], 0.0)
    xc = cb_ref[...]
    for tap in range(CONV_W):
        off = hb + (tap - CONV_LEFT) * rb
        xc = xc + xs[off:off + tr, :] * cw_ref[tap:tap + 1]
    g = _dot(xc.astype(BF), gw_ref[...]) + gb_ref[...]
    r = jax.nn.sigmoid(g[:, :LB])
    i = jax.nn.sigmoid(g[:, LB:])
    log_a = r * (-LRU_C * jax.nn.softplus(-lam_ref[...]))
    a_s[...] = jnp.exp(log_a)
    th = jnp.tanh(log_a)
    u_s[...] = jnp.sqrt(-2.0 * th / (1.0 - th)) * (i * xc)

    if rb % SUBLANES == 0:
        n_steps = tr // rb

        def body(s, h):
            row = pl.multiple_of(((n_steps - 1 - s) if reverse else s) * rb, rb)
            h = a_s[pl.ds(row, rb), :] * h + u_s[pl.ds(row, rb), :]
            u_s[pl.ds(row, rb), :] = h
            return h

        h = lax.fori_loop(0, n_steps, body, hc[...], unroll=4)
    else:
        assert 2 * rb == SUBLANES
        n_steps = tr // SUBLANES
        first = lax.broadcasted_iota(jnp.int32, (SUBLANES, 1), 0) < rb

        def body(s, h):
            row = pl.multiple_of(((n_steps - 1 - s) if reverse else s) * SUBLANES, SUBLANES)
            a = a_s[pl.ds(row, SUBLANES), :]
            u = u_s[pl.ds(row, SUBLANES), :]
            h1 = a * h + u
            h2 = a * pltpu.roll(h1, rb, 0) + u
            u_s[pl.ds(row, SUBLANES), :] = jnp.where(first != reverse, h1, h2)
            return pltpu.roll(h2, rb, 0)

        h = lax.fori_loop(0, n_steps, body, hc[...], unroll=4)
    hc[...] = h
    hf_ref[...] = h
    if combine:
        z = z_ref[...]
        y_ref[...] = ((yp_ref[...] + u_s[...]) * (z * jax.nn.sigmoid(z))).astype(y_ref.dtype)
    else:
        y_ref[...] = u_s[...]


def _lru_core(xz, lru, j, d, h0, rb, *, tr, y_prev=None):
    rows, e2 = xz.shape
    e = e2 // 2
    nblk = e // LB
    n_tiles = rows // tr
    hb = max(2 * rb, SUBLANES)
    per = tr // hb
    n_halo = rows // hb
    hr = h0.shape[0]
    reverse = d == 1
    combine = y_prev is not None

    def tidx(s):
        return (n_tiles - 1 - s) if reverse else s

    in_specs = [
        pl.BlockSpec((tr, LB), lambda c, s: (tidx(s), c)),
        pl.BlockSpec((hb, LB), lambda c, s: (jnp.maximum(tidx(s) * per - 1, 0), c)),
        pl.BlockSpec((hb, LB), lambda c, s: (jnp.minimum((tidx(s) + 1) * per, n_halo - 1), c)),
        pl.BlockSpec((None, CONV_W, LB), lambda c, s: (j, 0, c)),
        pl.BlockSpec((None, 1, LB), lambda c, s: (j, 0, c)),
        pl.BlockSpec((None, LB, 2 * LB), lambda c, s: ((j * 2 + d) * nblk + c, 0, 0)),
        pl.BlockSpec((None, 1, 2 * LB), lambda c, s: ((j * 2 + d) * nblk + c, 0, 0)),
        pl.BlockSpec((None, 1, LB), lambda c, s: (j * 2 + d, 0, c)),
        pl.BlockSpec((hr, LB), lambda c, s: (0, c)),
    ]
    args = [xz, xz, xz, lru["conv_w"], lru["conv_b"], lru["gate_w"], lru["gate_b"], lru["lam"], h0]
    if combine:
        in_specs += [
            pl.BlockSpec((tr, LB), lambda c, s: (tidx(s), c)),
            pl.BlockSpec((tr, LB), lambda c, s: (tidx(s), nblk + c)),
        ]
        args += [y_prev, xz]
    return pl.pallas_call(
        functools.partial(_lru_body, rb=rb, reverse=reverse, combine=combine, n_tiles=n_tiles),
        out_shape=[jax.ShapeDtypeStruct((rows, e), BF if combine else F32),
                   jax.ShapeDtypeStruct((hr, e), F32)],
        grid=(nblk, n_tiles),
        in_specs=in_specs,
        out_specs=[pl.BlockSpec((tr, LB), lambda c, s: (tidx(s), c)),
                   pl.BlockSpec((hr, LB), lambda c, s: (0, c))],
        scratch_shapes=[pltpu.VMEM((tr + 2 * hb, LB), F32), pltpu.VMEM((tr, LB), F32),
                        pltpu.VMEM((tr, LB), F32), pltpu.VMEM((hr, LB), F32)],
        compiler_params=_cparams("parallel", "arbitrary"),
        name="lru_core",
    )(*args)


def _rwkv_scan_body(*refs, reverse, zero_init, want_state, n_chunks):
    it = iter(refs)
    (r_ref, k_ref, v_ref, lw_ref, as_ref, kp_ref, bm_ref, tri_ref, trib_ref,
     lvl_ref) = (next(it) for _ in range(10))
    sel_ref = next(it) if want_state else None
    z0_ref = None if zero_init else next(it)
    y_ref = next(it)
    sf_ref = next(it) if want_state else None
    z_s = next(it)
    c = pl.program_id(2)

    @pl.when(c == 0)
    def _():
        z_s[...] = jnp.zeros_like(z_s) if zero_init else z0_ref[...]

    r = r_ref[...]
    k = k_ref[...]
    v = v_ref[...]
    lw = lw_ref[...]
    asg = as_ref[...]
    bm = bm_ref[...]
    incl = trib_ref[0]
    strict = trib_ref[1]
    eye = incl - strict

    kk = k * kp_ref[0:1]
    kk2 = kk * kk
    bm_b = bm.astype(BF)
    ss = sum(_dot(p, bm_b) for p in _split3(kk2))
    kk = kk / jnp.maximum(jnp.sqrt(ss), 1e-12)
    kd = k * (1.0 + (asg - 1.0) * kp_ref[1:2])
    bvec = kk * asg

    tri = tri_ref[...]
    cum = sum(_dot(tri, p) for p in _split3(lw))
    tot = cum[0:1] if reverse else cum[CH - 1:CH]
    e_pos = jnp.exp(cum)
    e_neg = jnp.exp(-cum)
    e_rem = jnp.exp(tot) * e_neg
    a_t = -kk * (e_pos * jnp.exp(-lw))
    b_t = bvec * e_neg
    k_t = kd * e_neg
    r_t = r * e_pos
    b_h = bvec * e_rem
    k_h = kd * e_rem

    def stack(x):
        return jnp.concatenate([x] * GH, axis=0)

    def bd(x):
        return (stack(x) * bm).astype(BF)

    at_bd = bd(a_t)
    rt_bd = bd(r_t)
    v_bd = bd(v)
    bt_s = stack(b_t).astype(BF)
    kt_s = stack(k_t).astype(BF)
    a_ab = jnp.where(strict > 0, _dot_nt(at_bd, bt_s), 0.0)
    a_ak = jnp.where(strict > 0, _dot_nt(at_bd, kt_s), 0.0)
    a_rb = jnp.where(incl > 0, _dot_nt(rt_bd, bt_s), 0.0)
    a_rk = jnp.where(incl > 0, _dot_nt(rt_bd, kt_s), 0.0)

    n_d = a_ab * lvl_ref[0]
    x = eye + n_d
    p = n_d
    n_sq = 1
    while 2 * n_sq < INV_BASE:
        pb = p.astype(BF)
        p = _dot(pb, pb)
        x = x + _dot(x.astype(BF), p.astype(BF))
        n_sq *= 2
    for lvl in range(1, lvl_ref.shape[0]):
        xb = x.astype(BF)
        x = x + _dot(xb, _dot((a_ab * lvl_ref[lvl]).astype(BF), xb).astype(BF))
    t_b = x.astype(BF)

    w = _dot(t_b, at_bd)
    u0 = _dot(t_b, _dot(a_ak.astype(BF), v_bd).astype(BF))
    y0 = _dot(a_rk.astype(BF), v_bd)
    z = z_s[...]
    z_b = z.astype(BF)
    u = _dot(w.astype(BF), z_b) + u0
    y = _dot(rt_bd, z_b) + _dot(a_rb.astype(BF), u.astype(BF)) + y0

    def unstack(m):
        return m[0:CH] + m[CH:2 * CH] + m[2 * CH:3 * CH] + m[3 * CH:4 * CH]

    y_ref[...] = unstack(y)
    lhs = jnp.concatenate([b_h, k_h], axis=0).astype(BF)
    rhs = jnp.concatenate([unstack(u), v], axis=0).astype(BF)
    upd = _dot_tn(lhs, rhs) * bm
    g_col = jnp.sum(eye * jnp.exp(tot), axis=1, keepdims=True)
    z_new = z * g_col + upd
    z_s[...] = z_new

    if want_state:
        @pl.when(c == n_chunks - 1)
        def _():
            sel = sel_ref[...]
            sf_ref[...] = sum(_dot_tn(p, sel) for p in _split3(z_new))


def _rwkv_scan(r, k, v, lw, asg, kp, consts, d, *, batch, seq_len, z0=None, want_state=False):
    rows, e = r.shape
    ng = e // GW
    nc = seq_len // CH
    reverse = d == 1
    zero_init = z0 is None

    def cidx(c):
        return (nc - 1 - c) if reverse else c

    tok = pl.BlockSpec((CH, GW), lambda b, g, c: (b * nc + cidx(c), g))
    in_specs = [tok, tok, tok, tok, tok,
                pl.BlockSpec((2, GW), lambda b, g, c: (0, g)),
                pl.BlockSpec((GW, GW), lambda b, g, c: (0, 0)),
                pl.BlockSpec((None, CH, CH), lambda b, g, c: (d, 0, 0)),
                pl.BlockSpec((None, 2, GW, GW), lambda b, g, c: (d, 0, 0, 0)),
                pl.BlockSpec(consts["lvl"].shape, lambda b, g, c: (0, 0, 0))]
    args = [r, k, v, lw, asg, kp, consts["bm"], consts["tri"], consts["trib"], consts["lvl"]]
    if want_state:
        in_specs.append(pl.BlockSpec((GW, HD), lambda b, g, c: (0, 0)))
        args.append(consts["sel"])
    if not zero_init:
        in_specs.append(pl.BlockSpec((None, None, GW, GW), lambda b, g, c: (b, g, 0, 0)))
        args.append(z0)
    out_shape = [jax.ShapeDtypeStruct((rows, e), F32)]
    out_specs = [tok]
    if want_state:
        out_shape.append(jax.ShapeDtypeStruct((batch, e, HD), F32))
        out_specs.append(pl.BlockSpec((None, GW, HD), lambda b, g, c: (b, g, 0)))
    res = pl.pallas_call(
        functools.partial(_rwkv_scan_body, reverse=reverse, zero_init=zero_init,
                          want_state=want_state, n_chunks=nc),
        out_shape=out_shape,
        grid=(batch, ng, nc),
        in_specs=in_specs,
        out_specs=out_specs,
        scratch_shapes=[pltpu.VMEM((GW, GW), F32)],
        compiler_params=_cparams("parallel", "parallel", "arbitrary"),
        name="rwkv_scan",
    )(*args)
    return res if want_state else (res[0], None)


def _rwkv_out_body(y0_ref, y1_ref, r_ref, k_ref, v_ref, a0_ref, a1_ref, z_ref, hp_ref, bm_ref, o_ref):
    bm_b = bm_ref[...].astype(BF)

    def headsum(x):
        return sum(_dot(p, bm_b) for p in _split3(x))

    y = y0_ref[...] + y1_ref[...]
    mean = headsum(y) * (1.0 / HD)
    yc = y - mean
    var = headsum(yc * yc) * (1.0 / HD)
    out = yc * lax.rsqrt(var + GN_EPS) * hp_ref[2:3] + hp_ref[3:4]
    r = r_ref[...]
    k = k_ref[...]
    v = v_ref[...]
    rk = r * k * hp_ref[1:2]
    k_a = hp_ref[0:1]
    for a_ref in (a0_ref, a1_ref):
        out = out + headsum(rk * (1.0 + (a_ref[...] - 1.0) * k_a)) * v
    z = z_ref[...]
    o_ref[...] = (out * (z * jax.nn.sigmoid(z))).astype(o_ref.dtype)


def _rwkv_out(y0, y1, r, k, v, a0, a1, z, hp, bm, *, tm):
    rows, e = r.shape
    tok = pl.BlockSpec((tm, GW), lambda i, g: (i, g))
    return pl.pallas_call(
        _rwkv_out_body,
        out_shape=jax.ShapeDtypeStruct((rows, e), BF),
        grid=(rows // tm, e // GW),
        in_specs=[tok] * 8 + [pl.BlockSpec((4, GW), lambda i, g: (0, g)),
                              pl.BlockSpec((GW, GW), lambda i, g: (0, 0))],
        out_specs=tok,
        compiler_params=_cparams("parallel", "parallel"),
        name="rwkv_out",
    )(y0, y1, r, k, v, a0, a1, z, hp, bm)


def _silu(x):
    return x * jax.nn.sigmoid(x)


def _act_tanh(x):
    return jnp.tanh(x)


def _act_sigmoid(x):
    return jax.nn.sigmoid(x)


def _act_logdecay(x):
    return -jnp.exp(-0.5) * jax.nn.sigmoid(x)


def _act_vmix(acc, v_first, mix):
    return acc + (v_first - acc) * mix


def _trunk(x, mods, init_lru, init_rwkv, W, *, batch, seq_len, grid_mode, want_state):
    rows, d = x.shape
    e = W["lru_w_out"].shape[1]
    depth = W["norm_pre"].shape[0]
    tm_row = min(256, seq_len)
    tm = min(1024, seq_len)
    nb = seq_len // tm
    m_tiles = rows // tm
    rb = batch
    lru_tr = max(2048 // rb, 1) * rb if rb % SUBLANES == 0 else 2048
    lru_tr = min(lru_tr, rows)
    cond_rows = rows // mods[0].shape[0]
    tn_in = min(1024, 2 * e)
    fin_lru, fin_rwkv = [], []
    v_first = None

    for i in range(depth):
        mod = mods[i]
        j = i // 2
        if i % 2 == 0:
            h = _prenorm(x, W["norm_pre"], i, mod, tm=tm_row, rows_per_cond=cond_rows)
            xz = _mm(h, W["lru_w_in"], j, None, tm=tm, tn=tn_in, m_tiles=m_tiles,
                     o_shape=(seq_len, batch * 2 * e),
                     o_imap=lambda n, m: (m % nb, (m // nb) * (2 * e // tn_in) + n), name="lru_in")
            xz = xz.reshape(seq_len * batch, 2 * e)
            y0, hf0 = _lru_core(xz, W["lru"], j, 0, init_lru[j][0], rb, tr=lru_tr)
            y, hf1 = _lru_core(xz, W["lru"], j, 1, init_lru[j][1], rb, tr=lru_tr, y_prev=y0)
            fin_lru.append((hf0, hf1))
            y = y.reshape(seq_len, batch * e)
            m = _mm(y, W["lru_w_out"], j, None, tm=tm, tn=1024, m_tiles=m_tiles,
                    a_imap=lambda n, m: (m % nb, m // nb), name="lru_out")
        else:
            h, xr, xw, xk, xv, xa = _shiftmix(
                x, W["norm_pre"], i, mod, W["rwkv_mu"][j], seq_len=seq_len, cond_rows=cond_rows, grid_mode=grid_mode,
                tm=seq_len if not grid_mode else min(512, seq_len))
            mk = dict(tm=tm, m_tiles=m_tiles)
            r = _mm(xr, W["rwkv_w_r"], j, None, tn=1024, name="rwkv_r", **mk)
            k = _mm(xk, W["rwkv_w_k"], j, None, tn=1024, name="rwkv_k", **mk)
            z = _mm(h, W["rwkv_w_g"], j, None, tn=1024, name="rwkv_g", **mk)
            if j == 0:
                v = _mm(xv, W["rwkv_w_v"], j, None, tn=1024, name="rwkv_v", **mk)
                v_first = v
            else:
                tv = _mm(xv, W["rwkv_v1"], j - 1, None, tn=128, out_dtype=BF, name="rwkv_v1", **mk)
                mix = _mm(tv, W["rwkv_v2"], j - 1, W["rwkv_v0"][j - 1], tn=1024, act=_act_sigmoid,
                          name="rwkv_v2", **mk)
                v = _mm(xv, W["rwkv_w_v"], j, None, tn=1024, extras=(v_first, mix), act=_act_vmix,
                        name="rwkv_v", **mk)
            tw = _mm(xw, W["rwkv_w1"], j, None, tn=256, act=_act_tanh, out_dtype=BF, name="rwkv_w1", **mk)
            ta = _mm(xa, W["rwkv_a1"], j, None, tn=256, out_dtype=BF, name="rwkv_a1", **mk)
            ys, fins, asgs = [], [], []
            for dd in range(2):
                sel_k = (lambda dd_: (lambda n, m: (m, dd_)))(dd)
                lw = _mm(tw, W["rwkv_w2"], j * 2 + dd, W["rwkv_w0"][j * 2 + dd], tn=1024, a_imap=sel_k,
                         act=_act_logdecay, name="rwkv_w2", **mk)
                asg = _mm(ta, W["rwkv_a2"], j * 2 + dd, W["rwkv_a0"][j * 2 + dd], tn=1024, a_imap=sel_k,
                          act=_act_sigmoid, name="rwkv_a2", **mk)
                z0 = None if init_rwkv is None else init_rwkv[j][dd]
                y_d, s_d = _rwkv_scan(r, k, v, lw, asg, W["rwkv_kp"][j], W["consts"], dd, batch=batch,
                                      seq_len=seq_len, z0=z0, want_state=want_state)
                ys.append(y_d)
                fins.append(s_d)
                asgs.append(asg)
            fin_rwkv.append(fins)
            yo = _rwkv_out(ys[0], ys[1], r, k, v, asgs[0], asgs[1], z, W["rwkv_hp"][j], W["consts"]["bm"],
                           tm=tm_row)
            m = _mm(yo, W["rwkv_w_o"], j, None, tn=1024, name="rwkv_o", **mk)
        x = _postnorm(x, m, W["norm_post"], i, mod, tm=tm_row, rows_per_cond=cond_rows)
    return x, fin_lru, fin_rwkv


def _scan_consts():
    idx = jnp.arange(GW)
    same = (idx[:, None] // HD) == (idx[None, :] // HD)
    t_r, t_c = idx[:, None] % CH, idx[None, :] % CH
    bm = same.astype(F32)
    tribs, tris = [], []
    ic = jnp.arange(CH)
    for d in range(2):
        ge = (t_r >= t_c) if d == 0 else (t_r <= t_c)
        gt = (t_r > t_c) if d == 0 else (t_r < t_c)
        tribs.append(jnp.stack([(same & ge).astype(F32), (same & gt).astype(F32)]))
        tris.append(((ic[:, None] >= ic[None, :]) if d == 0 else (ic[:, None] <= ic[None, :])).astype(BF))
    sel = (idx[:, None] % HD == jnp.arange(HD)[None, :]).astype(BF)
    lvl = [idx[:, None] // INV_BASE == idx[None, :] // INV_BASE]
    size = INV_BASE
    while size < CH:
        lvl.append((idx[:, None] // (2 * size) == idx[None, :] // (2 * size))
                   & (idx[:, None] // size != idx[None, :] // size))
        size *= 2
    return dict(bm=bm, trib=jnp.stack(tribs), tri=jnp.stack(tris), sel=sel,
                lvl=jnp.stack(lvl).astype(F32))


def kernel(x_prompt, x_sample, state_lru, state_rwkv, c, c_ctx, ada_w, ada_b, norm_pre, norm_post, lru_w_in, lru_conv_w, lru_conv_b, lru_gate_w, lru_gate_b, lru_lambda, lru_w_out, rwkv_mu, rwkv_w_r, rwkv_w_k, rwkv_w_v, rwkv_w_g, rwkv_w_o, rwkv_w0, rwkv_w1, rwkv_w2, rwkv_a0, rwkv_a1, rwkv_a2, rwkv_k_k, rwkv_k_a, rwkv_r_k, rwkv_ln_w, rwkv_ln_b, rwkv_v0, rwkv_v1, rwkv_v2):
    bp, tp, d = x_prompt.shape
    bs, ts, _ = x_sample.shape
    depth = ada_w.shape[0]
    n_lru, _, e2 = lru_w_in.shape
    e = e2 // 2
    n_rwkv = rwkv_w_r.shape[0]
    nblk = e // LB
    heads = e // HD
    ng = e // GW

    def merge_lead(w, k):
        return w.reshape((-1,) + w.shape[k:])

    gate_w = jnp.concatenate([lru_gate_w[:, :, 0], lru_gate_w[:, :, 1]], axis=-1)
    gate_b = lru_gate_b.reshape(n_lru, 2, 2, nblk, LB).transpose(0, 1, 3, 2, 4)
    lora_v = rwkv_v1.shape[-1]
    pad_v = (-lora_v) % 128
    W = dict(
        norm_pre=norm_pre.reshape(depth, 1, d), norm_post=norm_post.reshape(depth, 1, d),
        lru_w_in=lru_w_in.astype(BF), lru_w_out=lru_w_out.astype(BF),
        lru=dict(conv_w=lru_conv_w, conv_b=lru_conv_b.reshape(n_lru, 1, e),
                 gate_w=merge_lead(gate_w, 3).astype(BF),
                 gate_b=gate_b.reshape(n_lru * 2 * nblk, 1, 2 * LB),
                 lam=lru_lambda.reshape(n_lru * 2, 1, e)),
        rwkv_mu=rwkv_mu,
        rwkv_w_r=rwkv_w_r.astype(BF), rwkv_w_k=rwkv_w_k.astype(BF), rwkv_w_v=rwkv_w_v.astype(BF),
        rwkv_w_g=rwkv_w_g.astype(BF), rwkv_w_o=rwkv_w_o.astype(BF),
        rwkv_w1=jnp.concatenate([rwkv_w1[:, 0], rwkv_w1[:, 1]], axis=-1).astype(BF),
        rwkv_a1=jnp.concatenate([rwkv_a1[:, 0], rwkv_a1[:, 1]], axis=-1).astype(BF),
        rwkv_w2=merge_lead(rwkv_w2, 2).astype(BF), rwkv_a2=merge_lead(rwkv_a2, 2).astype(BF),
        rwkv_w0=rwkv_w0.reshape(n_rwkv * 2, 1, e), rwkv_a0=rwkv_a0.reshape(n_rwkv * 2, 1, e),
        rwkv_v0=rwkv_v0.reshape(-1, 1, e),
        rwkv_v1=jnp.pad(rwkv_v1, ((0, 0), (0, 0), (0, pad_v))).astype(BF),
        rwkv_v2=jnp.pad(rwkv_v2, ((0, 0), (0, pad_v), (0, 0))).astype(BF),
        rwkv_kp=jnp.stack([rwkv_k_k, rwkv_k_a], axis=1),
        rwkv_hp=jnp.stack([rwkv_k_a, rwkv_r_k, rwkv_ln_w, rwkv_ln_b], axis=1),
        consts=_scan_consts(),
    )

    n_cond = 1 + bs
    cond = jnp.zeros((SUBLANES * pl.cdiv(n_cond, SUBLANES), d), F32)
    cond = cond.at[0].set(c_ctx).at[1:n_cond].set(c)
    ada_wb = ada_w.astype(BF)
    mods_p, mods_s = [], []
    for i in range(depth):
        mod = _mm(cond, ada_wb, i, ada_b[i][None], tm=cond.shape[0], tn=256, m_tiles=1,
                  a_pre=_silu, name="adaln")
        mod = mod.reshape(cond.shape[0], 3, d)
        mods_p.append(mod[0:1])
        mods_s.append(mod[1:n_cond])

    zero_lru = [(jnp.zeros((bp, e), F32), jnp.zeros((bp, e), F32))] * n_lru
    y_p, fin_lru, fin_rwkv = _trunk(x_prompt.reshape(bp * tp, d), mods_p, zero_lru, None, W,
                                    batch=bp, seq_len=tp, grid_mode=False, want_state=True)
    new_state_lru = jnp.stack([jnp.stack(f, axis=1) for f in fin_lru], axis=1)
    new_state_rwkv = jnp.stack(
        [jnp.stack([s.reshape(bp, heads, HD, HD) for s in f], axis=1) for f in fin_rwkv], axis=1)

    rep = max(SUBLANES // bs, 1)
    init_lru = [tuple(jnp.tile(state_lru[:, jj, dd], (rep, 1)) for dd in range(2)) for jj in range(n_lru)]
    eye_g = jnp.eye(GH, dtype=F32)
    st = state_rwkv.reshape(bs, n_rwkv, 2, ng, GH, HD, HD)
    z0 = jnp.einsum("bldghvk,hi->bldghkiv", st, eye_g).reshape(bs, n_rwkv, 2, ng, GW, GW)
    init_rwkv = [[z0[:, jj, dd] for dd in range(2)] for jj in range(n_rwkv)]
    y_s, _, _ = _trunk(x_sample.reshape(bs * ts, d), mods_s, init_lru, init_rwkv, W,
                       batch=bs, seq_len=ts, grid_mode=True, want_state=False)
    return (y_p.reshape(bp, tp, d), y_s.reshape(bs, ts, d), new_state_lru, new_state_rwkv)
```

```python
import functools

import jax
import jax.numpy as jnp
from jax import lax
from jax.experimental import pallas as pl
from jax.experimental.pallas import tpu as pltpu

F32 = jnp.float32
BF = jnp.bfloat16

NORM_EPS = 1e-6
GN_EPS = 64e-5
LRU_C = 8.0
GRID_W = 64
CONV_W = 4
CONV_LEFT = 2

HD = 64
GH = 4
GW = GH * HD
CH = 64
INV_BASE = 16
SCAN_GROUPS = 4
LB = 256
SUBLANES = 8
VMEM_LIMIT = 56 * 1024 * 1024


def _cparams(*sem):
    return pltpu.CompilerParams(dimension_semantics=sem, vmem_limit_bytes=VMEM_LIMIT)


def _dot(a, b):
    return jnp.dot(a, b, preferred_element_type=F32)


def _dot_nt(a, b):
    return lax.dot_general(a, b, (((1,), (1,)), ((), ())), preferred_element_type=F32)


def _dot_tn(a, b):
    return lax.dot_general(a, b, (((0,), (0,)), ((), ())), preferred_element_type=F32)


def _split3(x):
    h1 = x.astype(BF)
    r1 = x - h1.astype(F32)
    h2 = r1.astype(BF)
    h3 = (r1 - h2.astype(F32)).astype(BF)
    return h1, h2, h3


def _modnorm(x, g, mod):
    y = x * lax.rsqrt(jnp.mean(x * x, axis=-1, keepdims=True) + NORM_EPS) * g
    return y * (1.0 + mod[1:2]) + mod[0:1]


def _mm_body(*refs, act, a_pre, n_extra):
    a_ref, w_ref, b_ref = refs[:3]
    extras = refs[3:3 + n_extra]
    o_ref = refs[3 + n_extra]
    a = a_ref[...]
    if a_pre is not None:
        a = a_pre(a.astype(F32))
    acc = _dot(a.astype(BF), w_ref[...]) + b_ref[...]
    if act is not None:
        acc = act(acc, *[e[...] for e in extras])
    o_ref[...] = acc.astype(o_ref.dtype)


def _mm(a, w3, li, bias, *, tm, tn, m_tiles, a_imap=None, o_shape=None, o_imap=None,
        extras=(), act=None, a_pre=None, out_dtype=F32, name="mm"):
    _, K, N = w3.shape
    tn = min(tn, N)
    assert N % tn == 0 and K % 128 == 0
    n_tiles = N // tn
    if a_imap is None:
        a_imap = lambda n, m: (m, 0)
    if o_imap is None:
        o_imap = lambda n, m: (m, n)
    if o_shape is None:
        o_shape = (m_tiles * tm, N)
    if bias is None:
        bias = jnp.zeros((1, N), F32)
    in_specs = [
        pl.BlockSpec((tm, K), a_imap),
        pl.BlockSpec((None, K, tn), lambda n, m: (li, 0, n)),
        pl.BlockSpec((1, tn), lambda n, m: (0, n)),
    ] + [pl.BlockSpec((tm, tn), lambda n, m: (m, n)) for _ in extras]
    return pl.pallas_call(
        functools.partial(_mm_body, act=act, a_pre=a_pre, n_extra=len(extras)),
        out_shape=jax.ShapeDtypeStruct(o_shape, out_dtype),
        grid=(n_tiles, m_tiles),
        in_specs=in_specs,
        out_specs=pl.BlockSpec((tm, tn), o_imap),
        compiler_params=_cparams("parallel", "parallel"),
        name=name,
    )(a, w3, bias, *extras)


def _prenorm_body(x_ref, g_ref, mod_ref, o_ref):
    o_ref[...] = _modnorm(x_ref[...], g_ref[...], mod_ref[0]).astype(o_ref.dtype)


def _prenorm(x, g3, layer, mod, *, tm, rows_per_cond):
    rows, d = x.shape
    return pl.pallas_call(
        _prenorm_body,
        out_shape=jax.ShapeDtypeStruct((rows, d), BF),
        grid=(rows // tm,),
        in_specs=[
            pl.BlockSpec((tm, d), lambda i: (i, 0)),
            pl.BlockSpec((None, 1, d), lambda i: (layer, 0, 0)),
            pl.BlockSpec((1, 3, d), lambda i: ((i * tm) // rows_per_cond, 0, 0)),
        ],
        out_specs=pl.BlockSpec((tm, d), lambda i: (i, 0)),
        compiler_params=_cparams("parallel"),
        name="prenorm",
    )(x, g3, mod)


def _postnorm_body(x_ref, m_ref, g_ref, mod_ref, o_ref):
    m = m_ref[...]
    y = m * lax.rsqrt(jnp.mean(m * m, axis=-1, keepdims=True) + NORM_EPS) * g_ref[...]
    o_ref[...] = x_ref[...] + mod_ref[0][2:3] * y


def _postnorm(x, m, g3, layer, mod, *, tm, rows_per_cond):
    rows, d = x.shape
    return pl.pallas_call(
        _postnorm_body,
        out_shape=jax.ShapeDtypeStruct((rows, d), F32),
        grid=(rows // tm,),
        in_specs=[
            pl.BlockSpec((tm, d), lambda i: (i, 0)),
            pl.BlockSpec((tm, d), lambda i: (i, 0)),
            pl.BlockSpec((None, 1, d), lambda i: (layer, 0, 0)),
            pl.BlockSpec((1, 3, d), lambda i: ((i * tm) // rows_per_cond, 0, 0)),
        ],
        out_specs=pl.BlockSpec((tm, d), lambda i: (i, 0)),
        compiler_params=_cparams("parallel"),
        name="postnorm",
    )(x, m, g3, mod)


def _write_mixes(h, shifted, mu_ref, out_refs):
    xx = shifted - h
    out_refs[0][...] = h.astype(BF)
    for m in range(5):
        out_refs[m + 1][...] = (h + xx * mu_ref[m:m + 1]).astype(BF)


def _shiftmix_seq_body(x_ref, g_ref, mod_ref, mu_ref, *out_refs):
    h = _modnorm(x_ref[...], g_ref[...], mod_ref[0])
    t_len, d = h.shape
    half = d // 2
    t = lax.broadcasted_iota(jnp.int32, (t_len, 1), 0)
    prev = jnp.where(t == 0, 0.0, pltpu.roll(h[:, :half], 1, 0))
    nxt = jnp.where(t == t_len - 1, 0.0, pltpu.roll(h[:, half:], t_len - 1, 0))
    _write_mixes(h, jnp.concatenate([prev, nxt], axis=1), mu_ref, out_refs)


def _shiftmix_grid_body(x_ref, up_ref, dn_ref, g_ref, mod_ref, mu_ref, *out_refs, tiles_per_batch):
    ti = pl.program_id(0) % tiles_per_batch
    g = g_ref[...]
    mod = mod_ref[0]
    h = _modnorm(x_ref[...], g, mod)
    tr, d = h.shape
    q = d // 4
    h_up = jnp.where(ti > 0, _modnorm(up_ref[...], g, mod)[:, 2 * q:3 * q], 0.0)
    h_dn = jnp.where(ti < tiles_per_batch - 1, _modnorm(dn_ref[...], g, mod)[:, 3 * q:], 0.0)
    col = lax.broadcasted_iota(jnp.int32, (tr, 1), 0) % GRID_W
    left = jnp.where(col == 0, 0.0, pltpu.roll(h[:, :q], 1, 0))
    right = jnp.where(col == GRID_W - 1, 0.0, pltpu.roll(h[:, q:2 * q], tr - 1, 0))
    up = jnp.concatenate([h_up, h[:tr - GRID_W, 2 * q:3 * q]], axis=0)
    down = jnp.concatenate([h[GRID_W:, 3 * q:], h_dn], axis=0)
    _write_mixes(h, jnp.concatenate([left, right, up, down], axis=1), mu_ref, out_refs)


def _shiftmix(x, g3, layer, mod, mu, *, seq_len, cond_rows, grid_mode, tm):
    rows, d = x.shape
    outs = [jax.ShapeDtypeStruct((rows, d), BF)] * 6
    common = [
        pl.BlockSpec((None, 1, d), lambda i: (layer, 0, 0)),
        pl.BlockSpec((1, 3, d), lambda i: ((i * tm) // cond_rows, 0, 0)),
        pl.BlockSpec((5, d), lambda i: (0, 0)),
    ]
    out_specs = [pl.BlockSpec((tm, d), lambda i: (i, 0))] * 6
    if not grid_mode:
        assert tm == seq_len
        return pl.pallas_call(
            _shiftmix_seq_body, out_shape=outs, grid=(rows // tm,),
            in_specs=[pl.BlockSpec((tm, d), lambda i: (i, 0))] + common,
            out_specs=out_specs, compiler_params=_cparams("parallel"), name="shiftmix_seq",
        )(x, g3, mod, mu)
    per = tm // GRID_W
    n_halo = rows // GRID_W
    return pl.pallas_call(
        functools.partial(_shiftmix_grid_body, tiles_per_batch=seq_len // tm),
        out_shape=outs, grid=(rows // tm,),
        in_specs=[
            pl.BlockSpec((tm, d), lambda i: (i, 0)),
            pl.BlockSpec((GRID_W, d), lambda i: (jnp.maximum(i * per - 1, 0), 0)),
            pl.BlockSpec((GRID_W, d), lambda i: (jnp.minimum((i + 1) * per, n_halo - 1), 0)),
        ] + common,
        out_specs=out_specs, compiler_params=_cparams("parallel"), name="shiftmix_grid",
    )(x, x, x, g3, mod, mu)


def _lru_body(*refs, rb, reverse, combine, n_tiles):
    if combine:
        (x_ref, xp_ref, xn_ref, cw_ref, cb_ref, gw_ref, gb_ref, lam_ref, h0_ref, yp_ref, z_ref,
         y_ref, hf_ref, xs, a_s, u_s, hc) = refs
    else:
        (x_ref, xp_ref, xn_ref, cw_ref, cb_ref, gw_ref, gb_ref, lam_ref, h0_ref,
         y_ref, hf_ref, xs, a_s, u_s, hc) = refs
    tr = x_ref.shape[0]
    hb = xp_ref.shape[0]
    step = pl.program_id(1)
    t_idx = (n_tiles - 1 - step) if reverse else step

    @pl.when(step == 0)
    def _():
        hc[...] = h0_ref[...]

    xs[hb:hb + tr, :] = x_ref[...]
    xs[0:hb, :] = jnp.where(t_idx > 0, xp_ref[...], 0.0)
    xs[hb + tr:hb + tr + hb, :] = jnp.where(t_idx < n_tiles - 1, xn_ref[...], 0.0)
    xc = cb_ref[...]
    for tap in range(CONV_W):
        off = hb + (tap - CONV_LEFT) * rb
        xc = xc + xs[off:off + tr, :] * cw_ref[tap:tap + 1]
    g = _dot(xc.astype(BF), gw_ref[...]) + gb_ref[...]
    r = jax.nn.sigmoid(g[:, :LB])
    i = jax.nn.sigmoid(g[:, LB:])
    log_a = r * (-LRU_C * jax.nn.softplus(-lam_ref[...]))
    a_s[...] = jnp.exp(log_a)
    th = jnp.tanh(log_a)
    u_s[...] = jnp.sqrt(-2.0 * th / (1.0 - th)) * (i * xc)

    if rb % SUBLANES == 0:
        n_steps = tr // rb

        def body(s, h):
            row = pl.multiple_of(((n_steps - 1 - s) if reverse else s) * rb, rb)
            h = a_s[pl.ds(row, rb), :] * h + u_s[pl.ds(row, rb), :]
            u_s[pl.ds(row, rb), :] = h
            return h

        h = lax.fori_loop(0, n_steps, body, hc[...], unroll=4)
    else:
        assert 2 * rb == SUBLANES
        n_steps = tr // SUBLANES
        first = lax.broadcasted_iota(jnp.int32, (SUBLANES, 1), 0) < rb

        def body(s, h):
            row = pl.multiple_of(((n_steps - 1 - s) if reverse else s) * SUBLANES, SUBLANES)
            a = a_s[pl.ds(row, SUBLANES), :]
            u = u_s[pl.ds(row, SUBLANES), :]
            h1 = a * h + u
            h2 = a * pltpu.roll(h1, rb, 0) + u
            u_s[pl.ds(row, SUBLANES), :] = jnp.where(first != reverse, h1, h2)
            return pltpu.roll(h2, rb, 0)

        h = lax.fori_loop(0, n_steps, body, hc[...], unroll=4)
    hc[...] = h
    hf_ref[...] = h
    if combine:
        z = z_ref[...]
        y_ref[...] = ((yp_ref[...] + u_s[...]) * (z * jax.nn.sigmoid(z))).astype(y_ref.dtype)
    else:
        y_ref[...] = u_s[...]


def _lru_core(xz, lru, j, d, h0, rb, *, tr, y_prev=None):
    rows, e2 = xz.shape
    e = e2 // 2
    nblk = e // LB
    n_tiles = rows // tr
    hb = max(2 * rb, SUBLANES)
    per = tr // hb
    n_halo = rows // hb
    hr = h0.shape[0]
    reverse = d == 1
    combine = y_prev is not None

    def tidx(s):
        return (n_tiles - 1 - s) if reverse else s

    in_specs = [
        pl.BlockSpec((tr, LB), lambda c, s: (tidx(s), c)),
        pl.BlockSpec((hb, LB), lambda c, s: (jnp.maximum(tidx(s) * per - 1, 0), c)),
        pl.BlockSpec((hb, LB), lambda c, s: (jnp.minimum((tidx(s) + 1) * per, n_halo - 1), c)),
        pl.BlockSpec((None, CONV_W, LB), lambda c, s: (j, 0, c)),
        pl.BlockSpec((None, 1, LB), lambda c, s: (j, 0, c)),
        pl.BlockSpec((None, LB, 2 * LB), lambda c, s: ((j * 2 + d) * nblk + c, 0, 0)),
        pl.BlockSpec((None, 1, 2 * LB), lambda c, s: ((j * 2 + d) * nblk + c, 0, 0)),
        pl.BlockSpec((None, 1, LB), lambda c, s: (j * 2 + d, 0, c)),
        pl.BlockSpec((hr, LB), lambda c, s: (0, c)),
    ]
    args = [xz, xz, xz, lru["conv_w"], lru["conv_b"], lru["gate_w"], lru["gate_b"], lru["lam"], h0]
    if combine:
        in_specs += [
            pl.BlockSpec((tr, LB), lambda c, s: (tidx(s), c)),
            pl.BlockSpec((tr, LB), lambda c, s: (tidx(s), nblk + c)),
        ]
        args += [y_prev, xz]
    return pl.pallas_call(
        functools.partial(_lru_body, rb=rb, reverse=reverse, combine=combine, n_tiles=n_tiles),
        out_shape=[jax.ShapeDtypeStruct((rows, e), BF if combine else F32),
                   jax.ShapeDtypeStruct((hr, e), F32)],
        grid=(nblk, n_tiles),
        in_specs=in_specs,
        out_specs=[pl.BlockSpec((tr, LB), lambda c, s: (tidx(s), c)),
                   pl.BlockSpec((hr, LB), lambda c, s: (0, c))],
        scratch_shapes=[pltpu.VMEM((tr + 2 * hb, LB), F32), pltpu.VMEM((tr, LB), F32),
                        pltpu.VMEM((tr, LB), F32), pltpu.VMEM((hr, LB), F32)],
        compiler_params=_cparams("parallel", "arbitrary"),
        name="lru_core",
    )(*args)


def _rwkv_scan_body(*refs, reverse, zero_init, want_state, n_chunks):
    it = iter(refs)
    (r_ref, k_ref, v_ref, lw_ref, as_ref, kp_ref, bm_ref, tri_ref, trib_ref,
     lvl_ref) = (next(it) for _ in range(10))
    sel_ref = next(it) if want_state else None
    z0_ref = None if zero_init else next(it)
    y_ref = next(it)
    sf_ref = next(it) if want_state else None
    z_s = next(it)
    c = pl.program_id(2)

    @pl.when(c == 0)
    def _():
        z_s[...] = jnp.zeros_like(z_s) if zero_init else z0_ref[...]

    groups = range(z_s.shape[0])
    lanes = [slice(g * GW, (g + 1) * GW) for g in groups]
    ys, z_new = _rwkv_chunk(
        [r_ref[:, l] for l in lanes], [k_ref[:, l] for l in lanes], [v_ref[:, l] for l in lanes],
        [lw_ref[:, l] for l in lanes], [as_ref[:, l] for l in lanes], [kp_ref[:, l] for l in lanes],
        bm_ref[...], tri_ref[...], trib_ref, lvl_ref, [z_s[g] for g in groups], reverse=reverse)
    for g in groups:
        y_ref[:, lanes[g]] = ys[g]
        z_s[g] = z_new[g]
    if want_state:
        @pl.when(c == n_chunks - 1)
        def _():
            sel = sel_ref[...]
            for g in groups:
                sf_ref[lanes[g], :] = sum(_dot_tn(p, sel) for p in _split3(z_new[g]))


def _vmap(fn, *lists):
    return [fn(*args) for args in zip(*lists)]


def _rwkv_chunk(r, k, v, lw, asg, kp, bm, tri, trib_ref, lvl_ref, z, *, reverse):
    incl = trib_ref[0]
    strict = trib_ref[1]
    eye = incl - strict
    bm_b = bm.astype(BF)

    def headsum(x):
        return sum(_dot(p, bm_b) for p in _split3(x))

    def stack(x):
        return jnp.concatenate([x] * GH, axis=0)

    def bd(x):
        return (stack(x) * bm).astype(BF)

    def unstack(m):
        return m[0:CH] + m[CH:2 * CH] + m[2 * CH:3 * CH] + m[3 * CH:4 * CH]

    def mm(a, b):
        return _vmap(lambda x, y: _dot(x.astype(BF), y.astype(BF)), a, b)

    kk = _vmap(lambda k_, p: k_ * p[0:1], k, kp)
    ss = _vmap(lambda x: headsum(x * x), kk)
    kk = _vmap(lambda x, s: x / jnp.maximum(jnp.sqrt(s), 1e-12), kk, ss)
    kd = _vmap(lambda k_, a, p: k_ * (1.0 + (a - 1.0) * p[1:2]), k, asg, kp)
    bvec = _vmap(lambda x, a: x * a, kk, asg)

    cum = _vmap(lambda x: sum(_dot(tri, p) for p in _split3(x)), lw)
    tot = _vmap(lambda x: x[0:1] if reverse else x[CH - 1:CH], cum)
    e_pos = _vmap(jnp.exp, cum)
    e_neg = _vmap(lambda x: jnp.exp(-x), cum)
    e_rem = _vmap(lambda t, en: jnp.exp(t) * en, tot, e_neg)
    at_bd = _vmap(lambda x, ep, l: bd(-x * (ep * jnp.exp(-l))), kk, e_pos, lw)
    rt_bd = _vmap(lambda x, ep: bd(x * ep), r, e_pos)
    v_bd = _vmap(bd, v)
    bt_s = _vmap(lambda x, en: stack(x * en).astype(BF), bvec, e_neg)
    kt_s = _vmap(lambda x, en: stack(x * en).astype(BF), kd, e_neg)
    a_ab = _vmap(lambda x, y: jnp.where(strict > 0, _dot_nt(x, y), 0.0), at_bd, bt_s)
    a_ak = _vmap(lambda x, y: jnp.where(strict > 0, _dot_nt(x, y), 0.0), at_bd, kt_s)
    a_rb = _vmap(lambda x, y: jnp.where(incl > 0, _dot_nt(x, y), 0.0), rt_bd, bt_s)
    a_rk = _vmap(lambda x, y: jnp.where(incl > 0, _dot_nt(x, y), 0.0), rt_bd, kt_s)

    p = _vmap(lambda n: n * lvl_ref[0], a_ab)
    x = _vmap(lambda n: eye + n, p)
    n_sq = 1
    while 2 * n_sq < INV_BASE:
        p = mm(p, p)
        x = _vmap(jnp.add, x, mm(x, p))
        n_sq *= 2
    for lvl in range(1, lvl_ref.shape[0]):
        off = _vmap(lambda n: n * lvl_ref[lvl], a_ab)
        x = _vmap(jnp.add, x, mm(x, mm(off, x)))

    w = mm(x, at_bd)
    u0 = mm(x, mm(a_ak, v_bd))
    y0 = mm(a_rk, v_bd)
    u = _vmap(jnp.add, mm(w, z), u0)
    y = _vmap(lambda a, b_, c_: a + b_ + c_, mm(rt_bd, z), mm(a_rb, u), y0)
    upd = _vmap(
        lambda b_, k_, er, u_, v_: _dot_tn(jnp.concatenate([b_ * er, k_ * er], axis=0).astype(BF),
                                           jnp.concatenate([unstack(u_), v_], axis=0).astype(BF)) * bm,
        bvec, kd, e_rem, u, v)
    z_new = _vmap(lambda z_, t, up: z_ * jnp.sum(eye * jnp.exp(t), axis=1, keepdims=True) + up, z, tot, upd)
    return _vmap(unstack, y), z_new


def _rwkv_scan(r, k, v, lw, asg, kp, consts, d, *, batch, seq_len, z0=None, want_state=False):
    rows, e = r.shape
    ngs = min(SCAN_GROUPS, e // GW)
    sw = ngs * GW
    ng = e // sw
    nc = seq_len // CH
    reverse = d == 1
    zero_init = z0 is None

    def cidx(c):
        return (nc - 1 - c) if reverse else c

    tok = pl.BlockSpec((CH, sw), lambda b, g, c: (b * nc + cidx(c), g))
    in_specs = [tok, tok, tok, tok, tok,
                pl.BlockSpec((2, sw), lambda b, g, c: (0, g)),
                pl.BlockSpec((GW, GW), lambda b, g, c: (0, 0)),
                pl.BlockSpec((None, CH, CH), lambda b, g, c: (d, 0, 0)),
                pl.BlockSpec((None, 2, GW, GW), lambda b, g, c: (d, 0, 0, 0)),
                pl.BlockSpec(consts["lvl"].shape, lambda b, g, c: (0, 0, 0))]
    args = [r, k, v, lw, asg, kp, consts["bm"], consts["tri"], consts["trib"], consts["lvl"]]
    if want_state:
        in_specs.append(pl.BlockSpec((GW, HD), lambda b, g, c: (0, 0)))
        args.append(consts["sel"])
    if not zero_init:
        in_specs.append(pl.BlockSpec((None, ngs, GW, GW), lambda b, g, c: (b, g, 0, 0)))
        args.append(z0)
    out_shape = [jax.ShapeDtypeStruct((rows, e), F32)]
    out_specs = [tok]
    if want_state:
        out_shape.append(jax.ShapeDtypeStruct((batch, e, HD), F32))
        out_specs.append(pl.BlockSpec((None, sw, HD), lambda b, g, c: (b, g, 0)))
    res = pl.pallas_call(
        functools.partial(_rwkv_scan_body, reverse=reverse, zero_init=zero_init,
                          want_state=want_state, n_chunks=nc),
        out_shape=out_shape,
        grid=(batch, ng, nc),
        in_specs=in_specs,
        out_specs=out_specs,
        scratch_shapes=[pltpu.VMEM((ngs, GW, GW), F32)],
        compiler_params=_cparams("parallel", "parallel", "arbitrary"),
        name="rwkv_scan",
    )(*args)
    return res if want_state else (res[0], None)


def _rwkv_out_body(y0_ref, y1_ref, r_ref, k_ref, v_ref, a0_ref, a1_ref, z_ref, hp_ref, bm_ref, o_ref):
    bm_b = bm_ref[...].astype(BF)

    def headsum(x):
        return sum(_dot(p, bm_b) for p in _split3(x))

    y = y0_ref[...] + y1_ref[...]
    mean = headsum(y) * (1.0 / HD)
    yc = y - mean
    var = headsum(yc * yc) * (1.0 / HD)
    out = yc * lax.rsqrt(var + GN_EPS) * hp_ref[2:3] + hp_ref[3:4]
    r = r_ref[...]
    k = k_ref[...]
    v = v_ref[...]
    rk = r * k * hp_ref[1:2]
    k_a = hp_ref[0:1]
    for a_ref in (a0_ref, a1_ref):
        out = out + headsum(rk * (1.0 + (a_ref[...] - 1.0) * k_a)) * v
    z = z_ref[...]
    o_ref[...] = (out * (z * jax.nn.sigmoid(z))).astype(o_ref.dtype)


def _rwkv_out(y0, y1, r, k, v, a0, a1, z, hp, bm, *, tm):
    rows, e = r.shape
    tok = pl.BlockSpec((tm, GW), lambda i, g: (i, g))
    return pl.pallas_call(
        _rwkv_out_body,
        out_shape=jax.ShapeDtypeStruct((rows, e), BF),
        grid=(rows // tm, e // GW),
        in_specs=[tok] * 8 + [pl.BlockSpec((4, GW), lambda i, g: (0, g)),
                              pl.BlockSpec((GW, GW), lambda i, g: (0, 0))],
        out_specs=tok,
        compiler_params=_cparams("parallel", "parallel"),
        name="rwkv_out",
    )(y0, y1, r, k, v, a0, a1, z, hp, bm)


def _silu(x):
    return x * jax.nn.sigmoid(x)


def _act_tanh(x):
    return jnp.tanh(x)


def _act_sigmoid(x):
    return jax.nn.sigmoid(x)


def _act_logdecay(x):
    return -jnp.exp(-0.5) * jax.nn.sigmoid(x)


def _act_vmix(acc, v_first, mix):
    return acc + (v_first - acc) * mix


def _trunk(x, mods, init_lru, init_rwkv, W, *, batch, seq_len, grid_mode, want_state):
    rows, d = x.shape
    e = W["lru_w_out"].shape[1]
    depth = W["norm_pre"].shape[0]
    tm_row = min(256, seq_len)
    tm = min(1024, seq_len)
    nb = seq_len // tm
    m_tiles = rows // tm
    rb = batch
    lru_tr = max(2048 // rb, 1) * rb if rb % SUBLANES == 0 else 2048
    lru_tr = min(lru_tr, rows)
    cond_rows = rows // mods[0].shape[0]
    tn_in = min(1024, 2 * e)
    fin_lru, fin_rwkv = [], []
    v_first = None

    for i in range(depth):
        mod = mods[i]
        j = i // 2
        if i % 2 == 0:
            h = _prenorm(x, W["norm_pre"], i, mod, tm=tm_row, rows_per_cond=cond_rows)
            xz = _mm(h, W["lru_w_in"], j, None, tm=tm, tn=tn_in, m_tiles=m_tiles,
                     o_shape=(seq_len, batch * 2 * e),
                     o_imap=lambda n, m: (m % nb, (m // nb) * (2 * e // tn_in) + n), name="lru_in")
            xz = xz.reshape(seq_len * batch, 2 * e)
            y0, hf0 = _lru_core(xz, W["lru"], j, 0, init_lru[j][0], rb, tr=lru_tr)
            y, hf1 = _lru_core(xz, W["lru"], j, 1, init_lru[j][1], rb, tr=lru_tr, y_prev=y0)
            fin_lru.append((hf0, hf1))
            y = y.reshape(seq_len, batch * e)
            m = _mm(y, W["lru_w_out"], j, None, tm=tm, tn=1024, m_tiles=m_tiles,
                    a_imap=lambda n, m: (m % nb, m // nb), name="lru_out")
        else:
            h, xr, xw, xk, xv, xa = _shiftmix(
                x, W["norm_pre"], i, mod, W["rwkv_mu"][j], seq_len=seq_len, cond_rows=cond_rows, grid_mode=grid_mode,
                tm=seq_len if not grid_mode else min(512, seq_len))
            mk = dict(tm=tm, m_tiles=m_tiles)
            r = _mm(xr, W["rwkv_w_r"], j, None, tn=1024, name="rwkv_r", **mk)
            k = _mm(xk, W["rwkv_w_k"], j, None, tn=1024, name="rwkv_k", **mk)
            z = _mm(h, W["rwkv_w_g"], j, None, tn=1024, name="rwkv_g", **mk)
            if j == 0:
                v = _mm(xv, W["rwkv_w_v"], j, None, tn=1024, name="rwkv_v", **mk)
                v_first = v
            else:
                tv = _mm(xv, W["rwkv_v1"], j - 1, None, tn=128, out_dtype=BF, name="rwkv_v1", **mk)
                mix = _mm(tv, W["rwkv_v2"], j - 1, W["rwkv_v0"][j - 1], tn=1024, act=_act_sigmoid,
                          name="rwkv_v2", **mk)
                v = _mm(xv, W["rwkv_w_v"], j, None, tn=1024, extras=(v_first, mix), act=_act_vmix,
                        name="rwkv_v", **mk)
            tw = _mm(xw, W["rwkv_w1"], j, None, tn=256, act=_act_tanh, out_dtype=BF, name="rwkv_w1", **mk)
            ta = _mm(xa, W["rwkv_a1"], j, None, tn=256, out_dtype=BF, name="rwkv_a1", **mk)
            ys, fins, asgs = [], [], []
            for dd in range(2):
                sel_k = (lambda dd_: (lambda n, m: (m, dd_)))(dd)
                lw = _mm(tw, W["rwkv_w2"], j * 2 + dd, W["rwkv_w0"][j * 2 + dd], tn=1024, a_imap=sel_k,
                         act=_act_logdecay, name="rwkv_w2", **mk)
                asg = _mm(ta, W["rwkv_a2"], j * 2 + dd, W["rwkv_a0"][j * 2 + dd], tn=1024, a_imap=sel_k,
                          act=_act_sigmoid, name="rwkv_a2", **mk)
                z0 = None if init_rwkv is None else init_rwkv[j][dd]
                y_d, s_d = _rwkv_scan(r, k, v, lw, asg, W["rwkv_kp"][j], W["consts"], dd, batch=batch,
                                      seq_len=seq_len, z0=z0, want_state=want_state)
                ys.append(y_d)
                fins.append(s_d)
                asgs.append(asg)
            fin_rwkv.append(fins)
            yo = _rwkv_out(ys[0], ys[1], r, k, v, asgs[0], asgs[1], z, W["rwkv_hp"][j], W["consts"]["bm"],
                           tm=tm_row)
            m = _mm(yo, W["rwkv_w_o"], j, None, tn=1024, name="rwkv_o", **mk)
        x = _postnorm(x, m, W["norm_post"], i, mod, tm=tm_row, rows_per_cond=cond_rows)
    return x, fin_lru, fin_rwkv


def _scan_consts():
    idx = jnp.arange(GW)
    same = (idx[:, None] // HD) == (idx[None, :] // HD)
    t_r, t_c = idx[:, None] % CH, idx[None, :] % CH
    bm = same.astype(F32)
    tribs, tris = [], []
    ic = jnp.arange(CH)
    for d in range(2):
        ge = (t_r >= t_c) if d == 0 else (t_r <= t_c)
        gt = (t_r > t_c) if d == 0 else (t_r < t_c)
        tribs.append(jnp.stack([(same & ge).astype(F32), (same & gt).astype(F32)]))
        tris.append(((ic[:, None] >= ic[None, :]) if d == 0 else (ic[:, None] <= ic[None, :])).astype(BF))
    sel = (idx[:, None] % HD == jnp.arange(HD)[None, :]).astype(BF)
    lvl = [idx[:, None] // INV_BASE == idx[None, :] // INV_BASE]
    size = INV_BASE
    while size < CH:
        lvl.append((idx[:, None] // (2 * size) == idx[None, :] // (2 * size))
                   & (idx[:, None] // size != idx[None, :] // size))
        size *= 2
    return dict(bm=bm, trib=jnp.stack(tribs), tri=jnp.stack(tris), sel=sel,
                lvl=jnp.stack(lvl).astype(F32))


def kernel(x_prompt, x_sample, state_lru, state_rwkv, c, c_ctx, ada_w, ada_b, norm_pre, norm_post, lru_w_in, lru_conv_w, lru_conv_b, lru_gate_w, lru_gate_b, lru_lambda, lru_w_out, rwkv_mu, rwkv_w_r, rwkv_w_k, rwkv_w_v, rwkv_w_g, rwkv_w_o, rwkv_w0, rwkv_w1, rwkv_w2, rwkv_a0, rwkv_a1, rwkv_a2, rwkv_k_k, rwkv_k_a, rwkv_r_k, rwkv_ln_w, rwkv_ln_b, rwkv_v0, rwkv_v1, rwkv_v2):
    bp, tp, d = x_prompt.shape
    bs, ts, _ = x_sample.shape
    depth = ada_w.shape[0]
    n_lru, _, e2 = lru_w_in.shape
    e = e2 // 2
    n_rwkv = rwkv_w_r.shape[0]
    nblk = e // LB
    heads = e // HD
    ng = e // GW

    def merge_lead(w, k):
        return w.reshape((-1,) + w.shape[k:])

    gate_w = jnp.concatenate([lru_gate_w[:, :, 0], lru_gate_w[:, :, 1]], axis=-1)
    gate_b = lru_gate_b.reshape(n_lru, 2, 2, nblk, LB).transpose(0, 1, 3, 2, 4)
    lora_v = rwkv_v1.shape[-1]
    pad_v = (-lora_v) % 128
    W = dict(
        norm_pre=norm_pre.reshape(depth, 1, d), norm_post=norm_post.reshape(depth, 1, d),
        lru_w_in=lru_w_in.astype(BF), lru_w_out=lru_w_out.astype(BF),
        lru=dict(conv_w=lru_conv_w, conv_b=lru_conv_b.reshape(n_lru, 1, e),
                 gate_w=merge_lead(gate_w, 3).astype(BF),
                 gate_b=gate_b.reshape(n_lru * 2 * nblk, 1, 2 * LB),
                 lam=lru_lambda.reshape(n_lru * 2, 1, e)),
        rwkv_mu=rwkv_mu,
        rwkv_w_r=rwkv_w_r.astype(BF), rwkv_w_k=rwkv_w_k.astype(BF), rwkv_w_v=rwkv_w_v.astype(BF),
        rwkv_w_g=rwkv_w_g.astype(BF), rwkv_w_o=rwkv_w_o.astype(BF),
        rwkv_w1=jnp.concatenate([rwkv_w1[:, 0], rwkv_w1[:, 1]], axis=-1).astype(BF),
        rwkv_a1=jnp.concatenate([rwkv_a1[:, 0], rwkv_a1[:, 1]], axis=-1).astype(BF),
        rwkv_w2=merge_lead(rwkv_w2, 2).astype(BF), rwkv_a2=merge_lead(rwkv_a2, 2).astype(BF),
        rwkv_w0=rwkv_w0.reshape(n_rwkv * 2, 1, e), rwkv_a0=rwkv_a0.reshape(n_rwkv * 2, 1, e),
        rwkv_v0=rwkv_v0.reshape(-1, 1, e),
        rwkv_v1=jnp.pad(rwkv_v1, ((0, 0), (0, 0), (0, pad_v))).astype(BF),
        rwkv_v2=jnp.pad(rwkv_v2, ((0, 0), (0, pad_v), (0, 0))).astype(BF),
        rwkv_kp=jnp.stack([rwkv_k_k, rwkv_k_a], axis=1),
        rwkv_hp=jnp.stack([rwkv_k_a, rwkv_r_k, rwkv_ln_w, rwkv_ln_b], axis=1),
        consts=_scan_consts(),
    )

    n_cond = 1 + bs
    cond = jnp.zeros((SUBLANES * pl.cdiv(n_cond, SUBLANES), d), F32)
    cond = cond.at[0].set(c_ctx).at[1:n_cond].set(c)
    ada_wb = ada_w.astype(BF)
    mods_p, mods_s = [], []
    for i in range(depth):
        mod = _mm(cond, ada_wb, i, ada_b[i][None], tm=cond.shape[0], tn=256, m_tiles=1,
                  a_pre=_silu, name="adaln")
        mod = mod.reshape(cond.shape[0], 3, d)
        mods_p.append(mod[0:1])
        mods_s.append(mod[1:n_cond])

    zero_lru = [(jnp.zeros((bp, e), F32), jnp.zeros((bp, e), F32))] * n_lru
    y_p, fin_lru, fin_rwkv = _trunk(x_prompt.reshape(bp * tp, d), mods_p, zero_lru, None, W,
                                    batch=bp, seq_len=tp, grid_mode=False, want_state=True)
    new_state_lru = jnp.stack([jnp.stack(f, axis=1) for f in fin_lru], axis=1)
    new_state_rwkv = jnp.stack(
        [jnp.stack([s.reshape(bp, heads, HD, HD) for s in f], axis=1) for f in fin_rwkv], axis=1)

    rep = max(SUBLANES // bs, 1)
    init_lru = [tuple(jnp.tile(state_lru[:, jj, dd], (rep, 1)) for dd in range(2)) for jj in range(n_lru)]
    eye_g = jnp.eye(GH, dtype=F32)
    st = state_rwkv.reshape(bs, n_rwkv, 2, ng, GH, HD, HD)
    z0 = jnp.einsum("bldghvk,hi->bldghkiv", st, eye_g).reshape(bs, n_rwkv, 2, ng, GW, GW)
    init_rwkv = [[z0[:, jj, dd] for dd in range(2)] for jj in range(n_rwkv)]
    y_s, _, _ = _trunk(x_sample.reshape(bs * ts, d), mods_s, init_lru, init_rwkv, W,
                       batch=bs, seq_len=ts, grid_mode=True, want_state=False)
    return (y_p.reshape(bp, tp, d), y_s.reshape(bs, ts, d), new_state_lru, new_state_rwkv)
```

```python
import functools

import jax
import jax.numpy as jnp
from jax import lax
from jax.experimental import pallas as pl
from jax.experimental.pallas import tpu as pltpu

F32 = jnp.float32
BF = jnp.bfloat16

NORM_EPS = 1e-6
GN_EPS = 64e-5
LRU_C = 8.0
GRID_W = 64
CONV_W = 4
CONV_LEFT = 2

HD = 64
GH = 4
GW = GH * HD
CH = 64
INV_BASE = 16
SCAN_GROUPS = 8
LB = 256
SUBLANES = 8
VMEM_LIMIT = 56 * 1024 * 1024


def _cparams(*sem):
    return pltpu.CompilerParams(dimension_semantics=sem, vmem_limit_bytes=VMEM_LIMIT)


def _dot(a, b):
    return jnp.dot(a, b, preferred_element_type=F32)


def _dot_nt(a, b):
    return lax.dot_general(a, b, (((1,), (1,)), ((), ())), preferred_element_type=F32)


def _dot_tn(a, b):
    return lax.dot_general(a, b, (((0,), (0,)), ((), ())), preferred_element_type=F32)


def _split3(x):
    h1 = x.astype(BF)
    r1 = x - h1.astype(F32)
    h2 = r1.astype(BF)
    h3 = (r1 - h2.astype(F32)).astype(BF)
    return h1, h2, h3


def _modnorm(x, g, mod):
    y = x * lax.rsqrt(jnp.mean(x * x, axis=-1, keepdims=True) + NORM_EPS) * g
    return y * (1.0 + mod[1:2]) + mod[0:1]


def _mm_body(*refs, act, a_pre, n_extra):
    a_ref, w_ref, b_ref = refs[:3]
    extras = refs[3:3 + n_extra]
    o_ref = refs[3 + n_extra]
    a = a_ref[...]
    if a_pre is not None:
        a = a_pre(a.astype(F32))
    acc = _dot(a.astype(BF), w_ref[...]) + b_ref[...]
    if act is not None:
        acc = act(acc, *[e[...] for e in extras])
    o_ref[...] = acc.astype(o_ref.dtype)


def _mm(a, w3, li, bias, *, tm, tn, m_tiles, a_imap=None, o_shape=None, o_imap=None,
        extras=(), act=None, a_pre=None, out_dtype=F32, name="mm"):
    _, K, N = w3.shape
    tn = min(tn, N)
    assert N % tn == 0 and K % 128 == 0
    n_tiles = N // tn
    if a_imap is None:
        a_imap = lambda n, m: (m, 0)
    if o_imap is None:
        o_imap = lambda n, m: (m, n)
    if o_shape is None:
        o_shape = (m_tiles * tm, N)
    if bias is None:
        bias = jnp.zeros((1, N), F32)
    in_specs = [
        pl.BlockSpec((tm, K), a_imap),
        pl.BlockSpec((None, K, tn), lambda n, m: (li, 0, n)),
        pl.BlockSpec((1, tn), lambda n, m: (0, n)),
    ] + [pl.BlockSpec((tm, tn), lambda n, m: (m, n)) for _ in extras]
    return pl.pallas_call(
        functools.partial(_mm_body, act=act, a_pre=a_pre, n_extra=len(extras)),
        out_shape=jax.ShapeDtypeStruct(o_shape, out_dtype),
        grid=(n_tiles, m_tiles),
        in_specs=in_specs,
        out_specs=pl.BlockSpec((tm, tn), o_imap),
        compiler_params=_cparams("parallel", "parallel"),
        name=name,
    )(a, w3, bias, *extras)


def _prenorm_body(x_ref, g_ref, mod_ref, o_ref):
    o_ref[...] = _modnorm(x_ref[...], g_ref[...], mod_ref[0]).astype(o_ref.dtype)


def _prenorm(x, g3, layer, mod, *, tm, rows_per_cond):
    rows, d = x.shape
    return pl.pallas_call(
        _prenorm_body,
        out_shape=jax.ShapeDtypeStruct((rows, d), BF),
        grid=(rows // tm,),
        in_specs=[
            pl.BlockSpec((tm, d), lambda i: (i, 0)),
            pl.BlockSpec((None, 1, d), lambda i: (layer, 0, 0)),
            pl.BlockSpec((1, 3, d), lambda i: ((i * tm) // rows_per_cond, 0, 0)),
        ],
        out_specs=pl.BlockSpec((tm, d), lambda i: (i, 0)),
        compiler_params=_cparams("parallel"),
        name="prenorm",
    )(x, g3, mod)


def _postnorm_body(x_ref, m_ref, g_ref, mod_ref, o_ref):
    m = m_ref[...]
    y = m * lax.rsqrt(jnp.mean(m * m, axis=-1, keepdims=True) + NORM_EPS) * g_ref[...]
    o_ref[...] = x_ref[...] + mod_ref[0][2:3] * y


def _postnorm(x, m, g3, layer, mod, *, tm, rows_per_cond):
    rows, d = x.shape
    return pl.pallas_call(
        _postnorm_body,
        out_shape=jax.ShapeDtypeStruct((rows, d), F32),
        grid=(rows // tm,),
        in_specs=[
            pl.BlockSpec((tm, d), lambda i: (i, 0)),
            pl.BlockSpec((tm, d), lambda i: (i, 0)),
            pl.BlockSpec((None, 1, d), lambda i: (layer, 0, 0)),
            pl.BlockSpec((1, 3, d), lambda i: ((i * tm) // rows_per_cond, 0, 0)),
        ],
        out_specs=pl.BlockSpec((tm, d), lambda i: (i, 0)),
        compiler_params=_cparams("parallel"),
        name="postnorm",
    )(x, m, g3, mod)


def _write_mixes(h, shifted, mu_ref, out_refs):
    xx = shifted - h
    out_refs[0][...] = h.astype(BF)
    for m in range(5):
        out_refs[m + 1][...] = (h + xx * mu_ref[m:m + 1]).astype(BF)


def _shiftmix_seq_body(x_ref, g_ref, mod_ref, mu_ref, *out_refs):
    h = _modnorm(x_ref[...], g_ref[...], mod_ref[0])
    t_len, d = h.shape
    half = d // 2
    t = lax.broadcasted_iota(jnp.int32, (t_len, 1), 0)
    prev = jnp.where(t == 0, 0.0, pltpu.roll(h[:, :half], 1, 0))
    nxt = jnp.where(t == t_len - 1, 0.0, pltpu.roll(h[:, half:], t_len - 1, 0))
    _write_mixes(h, jnp.concatenate([prev, nxt], axis=1), mu_ref, out_refs)


def _shiftmix_grid_body(x_ref, up_ref, dn_ref, g_ref, mod_ref, mu_ref, *out_refs, tiles_per_batch):
    ti = pl.program_id(0) % tiles_per_batch
    g = g_ref[...]
    mod = mod_ref[0]
    h = _modnorm(x_ref[...], g, mod)
    tr, d = h.shape
    q = d // 4
    h_up = jnp.where(ti > 0, _modnorm(up_ref[...], g, mod)[:, 2 * q:3 * q], 0.0)
    h_dn = jnp.where(ti < tiles_per_batch - 1, _modnorm(dn_ref[...], g, mod)[:, 3 * q:], 0.0)
    col = lax.broadcasted_iota(jnp.int32, (tr, 1), 0) % GRID_W
    left = jnp.where(col == 0, 0.0, pltpu.roll(h[:, :q], 1, 0))
    right = jnp.where(col == GRID_W - 1, 0.0, pltpu.roll(h[:, q:2 * q], tr - 1, 0))
    up = jnp.concatenate([h_up, h[:tr - GRID_W, 2 * q:3 * q]], axis=0)
    down = jnp.concatenate([h[GRID_W:, 3 * q:], h_dn], axis=0)
    _write_mixes(h, jnp.concatenate([left, right, up, down], axis=1), mu_ref, out_refs)


def _shiftmix(x, g3, layer, mod, mu, *, seq_len, cond_rows, grid_mode, tm):
    rows, d = x.shape
    outs = [jax.ShapeDtypeStruct((rows, d), BF)] * 6
    common = [
        pl.BlockSpec((None, 1, d), lambda i: (layer, 0, 0)),
        pl.BlockSpec((1, 3, d), lambda i: ((i * tm) // cond_rows, 0, 0)),
        pl.BlockSpec((5, d), lambda i: (0, 0)),
    ]
    out_specs = [pl.BlockSpec((tm, d), lambda i: (i, 0))] * 6
    if not grid_mode:
        assert tm == seq_len
        return pl.pallas_call(
            _shiftmix_seq_body, out_shape=outs, grid=(rows // tm,),
            in_specs=[pl.BlockSpec((tm, d), lambda i: (i, 0))] + common,
            out_specs=out_specs, compiler_params=_cparams("parallel"), name="shiftmix_seq",
        )(x, g3, mod, mu)
    per = tm // GRID_W
    n_halo = rows // GRID_W
    return pl.pallas_call(
        functools.partial(_shiftmix_grid_body, tiles_per_batch=seq_len // tm),
        out_shape=outs, grid=(rows // tm,),
        in_specs=[
            pl.BlockSpec((tm, d), lambda i: (i, 0)),
            pl.BlockSpec((GRID_W, d), lambda i: (jnp.maximum(i * per - 1, 0), 0)),
            pl.BlockSpec((GRID_W, d), lambda i: (jnp.minimum((i + 1) * per, n_halo - 1), 0)),
        ] + common,
        out_specs=out_specs, compiler_params=_cparams("parallel"), name="shiftmix_grid",
    )(x, x, x, g3, mod, mu)


def _lru_body(*refs, rb, reverse, combine, n_tiles):
    if combine:
        (x_ref, xp_ref, xn_ref, cw_ref, cb_ref, gw_ref, gb_ref, lam_ref, h0_ref, yp_ref, z_ref,
         y_ref, hf_ref, xs, a_s, u_s, hc) = refs
    else:
        (x_ref, xp_ref, xn_ref, cw_ref, cb_ref, gw_ref, gb_ref, lam_ref, h0_ref,
         y_ref, hf_ref, xs, a_s, u_s, hc) = refs
    tr = x_ref.shape[0]
    hb = xp_ref.shape[0]
    step = pl.program_id(1)
    t_idx = (n_tiles - 1 - step) if reverse else step

    @pl.when(step == 0)
    def _():
        hc[...] = h0_ref[...]

    xs[hb:hb + tr, :] = x_ref[...].astype(F32)
    xs[0:hb, :] = jnp.where(t_idx > 0, xp_ref[...].astype(F32), 0.0)
    xs[hb + tr:hb + tr + hb, :] = jnp.where(t_idx < n_tiles - 1, xn_ref[...].astype(F32), 0.0)
    xc = cb_ref[...]
    for tap in range(CONV_W):
        off = hb + (tap - CONV_LEFT) * rb
        xc = xc + xs[off:off + tr, :] * cw_ref[tap:tap + 1]
    g = _dot(xc.astype(BF), gw_ref[...]) + gb_ref[...]
    r = jax.nn.sigmoid(g[:, :LB])
    i = jax.nn.sigmoid(g[:, LB:])
    log_a = r * (-LRU_C * jax.nn.softplus(-lam_ref[...]))
    a_s[...] = jnp.exp(log_a)
    th = jnp.tanh(log_a)
    u_s[...] = jnp.sqrt(-2.0 * th / (1.0 - th)) * (i * xc)

    if rb % SUBLANES == 0:
        n_steps = tr // rb

        def body(s, h):
            row = pl.multiple_of(((n_steps - 1 - s) if reverse else s) * rb, rb)
            h = a_s[pl.ds(row, rb), :] * h + u_s[pl.ds(row, rb), :]
            u_s[pl.ds(row, rb), :] = h
            return h

        h = lax.fori_loop(0, n_steps, body, hc[...], unroll=4)
    else:
        assert 2 * rb == SUBLANES
        n_steps = tr // SUBLANES
        first = lax.broadcasted_iota(jnp.int32, (SUBLANES, 1), 0) < rb

        def body(s, h):
            row = pl.multiple_of(((n_steps - 1 - s) if reverse else s) * SUBLANES, SUBLANES)
            a = a_s[pl.ds(row, SUBLANES), :]
            u = u_s[pl.ds(row, SUBLANES), :]
            h1 = a * h + u
            h2 = a * pltpu.roll(h1, rb, 0) + u
            u_s[pl.ds(row, SUBLANES), :] = jnp.where(first != reverse, h1, h2)
            return pltpu.roll(h2, rb, 0)

        h = lax.fori_loop(0, n_steps, body, hc[...], unroll=4)
    hc[...] = h
    hf_ref[...] = h
    if combine:
        z = z_ref[...].astype(F32)
        y_ref[...] = ((yp_ref[...] + u_s[...]) * (z * jax.nn.sigmoid(z))).astype(y_ref.dtype)
    else:
        y_ref[...] = u_s[...]


def _lru_core(xz, lru, j, d, h0, rb, *, tr, y_prev=None):
    rows, e2 = xz.shape
    e = e2 // 2
    nblk = e // LB
    n_tiles = rows // tr
    hb = max(2 * rb, 2 * SUBLANES)
    per = tr // hb
    n_halo = rows // hb
    hr = h0.shape[0]
    reverse = d == 1
    combine = y_prev is not None

    def tidx(s):
        return (n_tiles - 1 - s) if reverse else s

    in_specs = [
        pl.BlockSpec((tr, LB), lambda c, s: (tidx(s), c)),
        pl.BlockSpec((hb, LB), lambda c, s: (jnp.maximum(tidx(s) * per - 1, 0), c)),
        pl.BlockSpec((hb, LB), lambda c, s: (jnp.minimum((tidx(s) + 1) * per, n_halo - 1), c)),
        pl.BlockSpec((None, CONV_W, LB), lambda c, s: (j, 0, c)),
        pl.BlockSpec((None, 1, LB), lambda c, s: (j, 0, c)),
        pl.BlockSpec((None, LB, 2 * LB), lambda c, s: ((j * 2 + d) * nblk + c, 0, 0)),
        pl.BlockSpec((None, 1, 2 * LB), lambda c, s: ((j * 2 + d) * nblk + c, 0, 0)),
        pl.BlockSpec((None, 1, LB), lambda c, s: (j * 2 + d, 0, c)),
        pl.BlockSpec((hr, LB), lambda c, s: (0, c)),
    ]
    args = [xz, xz, xz, lru["conv_w"], lru["conv_b"], lru["gate_w"], lru["gate_b"], lru["lam"], h0]
    if combine:
        in_specs += [
            pl.BlockSpec((tr, LB), lambda c, s: (tidx(s), c)),
            pl.BlockSpec((tr, LB), lambda c, s: (tidx(s), nblk + c)),
        ]
        args += [y_prev, xz]
    return pl.pallas_call(
        functools.partial(_lru_body, rb=rb, reverse=reverse, combine=combine, n_tiles=n_tiles),
        out_shape=[jax.ShapeDtypeStruct((rows, e), BF if combine else F32),
                   jax.ShapeDtypeStruct((hr, e), F32)],
        grid=(nblk, n_tiles),
        in_specs=in_specs,
        out_specs=[pl.BlockSpec((tr, LB), lambda c, s: (tidx(s), c)),
                   pl.BlockSpec((hr, LB), lambda c, s: (0, c))],
        scratch_shapes=[pltpu.VMEM((tr + 2 * hb, LB), F32), pltpu.VMEM((tr, LB), F32),
                        pltpu.VMEM((tr, LB), F32), pltpu.VMEM((hr, LB), F32)],
        compiler_params=_cparams("parallel", "arbitrary"),
        name="lru_core",
    )(*args)


def _rwkv_scan_body(*refs, reverse, zero_init, want_state, n_chunks):
    it = iter(refs)
    (r_ref, k_ref, v_ref, lw_ref, as_ref, kp_ref, bm_ref, eye_ref, tri_ref, cat_ref,
     lvl_ref) = (next(it) for _ in range(11))
    sel_ref = next(it) if want_state else None
    z0_ref = None if zero_init else next(it)
    y_ref = next(it)
    sf_ref = next(it) if want_state else None
    z_s = next(it)
    c = pl.program_id(2)

    @pl.when(c == 0)
    def _():
        z_s[...] = jnp.zeros_like(z_s) if zero_init else z0_ref[...]

    groups = range(z_s.shape[0])
    lanes = [slice(g * GW, (g + 1) * GW) for g in groups]
    ys, z_new = _rwkv_chunk(
        [r_ref[:, l] for l in lanes], [k_ref[:, l] for l in lanes], [v_ref[:, l] for l in lanes],
        [lw_ref[:, l] for l in lanes], [as_ref[:, l] for l in lanes], [kp_ref[:, l] for l in lanes],
        bm_ref[...], eye_ref[...], tri_ref[...], cat_ref, lvl_ref, [z_s[g] for g in groups],
        reverse=reverse)
    for g in groups:
        y_ref[:, lanes[g]] = ys[g]
        z_s[g] = z_new[g]
    if want_state:
        @pl.when(c == n_chunks - 1)
        def _():
            sel = sel_ref[...]
            for g in groups:
                sf_ref[lanes[g], :] = sum(_dot_tn(p, sel) for p in _split3(z_new[g]))


def _vmap(fn, *lists):
    return [fn(*args) for args in zip(*lists)]


def _rwkv_chunk(r, k, v, lw, asg, kp, bm, eye_bd, tri, cat_ref, lvl_ref, z, *, reverse):
    incl = cat_ref[0]
    strict = cat_ref[1]
    eye_c = incl - strict
    bm_b = bm.astype(BF)

    def headsum(x):
        return sum(_dot(p, bm_b) for p in _split3(x))

    def bdm(x):
        return jnp.concatenate([x.astype(BF)] * GH, axis=0) * bm_b

    def mm(a, b_bd):
        return _vmap(lambda x, y: _dot(x.astype(BF), y), a, b_bd)

    def stack2(a, b):
        return _vmap(lambda x, y: jnp.concatenate([x, y], axis=0), a, b)

    def top(a):
        return [x[:CH] for x in a]

    def bot(a):
        return [x[CH:] for x in a]

    def masked(a, m):
        return [jnp.where(m > 0, x, 0.0) for x in a]

    kk = _vmap(lambda k_, p: k_ * p[0:1], k, kp)
    ss = _vmap(lambda x: headsum(x * x), kk)
    kk = _vmap(lambda x, s: x / jnp.maximum(jnp.sqrt(s), 1e-12), kk, ss)
    kd = _vmap(lambda k_, a, p: k_ * (1.0 + (a - 1.0) * p[1:2]), k, asg, kp)
    bvec = _vmap(lambda x, a: x * a, kk, asg)

    cum = _vmap(lambda x: sum(_dot(tri, p) for p in _split3(x)), lw)
    tot = _vmap(lambda x: x[0:1] if reverse else x[CH - 1:CH], cum)
    e_pos = _vmap(jnp.exp, cum)
    e_neg = _vmap(lambda x: jnp.exp(-x), cum)
    e_rem = _vmap(lambda t, en: jnp.exp(t) * en, tot, e_neg)
    a_t = _vmap(lambda x, ep, l: -x * (ep * jnp.exp(-l)), kk, e_pos, lw)
    r_t = _vmap(lambda x, ep: x * ep, r, e_pos)
    ar = [x.astype(BF) for x in stack2(a_t, r_t)]
    bt_bd = _vmap(lambda x, en: bdm(x * en), bvec, e_neg)
    kt_bd = _vmap(lambda x, en: bdm(x * en), kd, e_neg)
    ab = _vmap(_dot_nt, ar, bt_bd)
    ak = _vmap(_dot_nt, ar, kt_bd)
    a_ab, a_rb = masked(top(ab), strict), masked(bot(ab), incl)
    a_ak, a_rk = masked(top(ak), strict), masked(bot(ak), incl)

    p = [n * lvl_ref[0] for n in a_ab]
    x = [eye_c + n for n in p]
    pw = mm(p, _vmap(bdm, p))
    e = 2
    while e < INV_BASE:
        rhs = _vmap(bdm, pw)
        if 2 * e < INV_BASE:
            both = mm(stack2(x, pw), rhs)
            x = _vmap(jnp.add, x, top(both))
            pw = bot(both)
        else:
            x = _vmap(jnp.add, x, mm(x, rhs))
        e *= 2
    for lvl in range(1, lvl_ref.shape[0]):
        off = [n * lvl_ref[lvl] for n in a_ab]
        x = _vmap(jnp.add, x, mm(x, _vmap(bdm, mm(off, _vmap(bdm, x)))))

    s_v = mm(stack2(a_ak, a_rk), _vmap(bdm, v))
    s_z = _vmap(lambda a, z_: _dot(a, z_.astype(BF)), ar, z)
    u = mm(x, _vmap(lambda a, b_: bdm(a + b_), top(s_z), top(s_v)))
    y = _vmap(lambda a, b_, c_: a + b_ + c_, bot(s_z), bot(s_v), mm(a_rb, _vmap(bdm, u)))
    upd = _vmap(
        lambda b_, k_, er, u_, v_: _dot_tn(jnp.concatenate([b_ * er, k_ * er], axis=0).astype(BF),
                                           jnp.concatenate([u_, v_], axis=0).astype(BF)) * bm,
        bvec, kd, e_rem, u, v)
    z_new = _vmap(lambda z_, t, up: z_ * jnp.sum(eye_bd * jnp.exp(t), axis=1, keepdims=True) + up,
                  z, tot, upd)
    return y, z_new


def _rwkv_scan(r, k, v, lw, asg, kp, consts, d, *, batch, seq_len, z0=None, want_state=False):
    rows, e = r.shape
    ngs = min(SCAN_GROUPS, e // GW)
    sw = ngs * GW
    ng = e // sw
    nc = seq_len // CH
    reverse = d == 1
    zero_init = z0 is None

    def cidx(c):
        return (nc - 1 - c) if reverse else c

    tok = pl.BlockSpec((CH, sw), lambda b, g, c: (b * nc + cidx(c), g))
    in_specs = [tok, tok, tok, tok, tok,
                pl.BlockSpec((2, sw), lambda b, g, c: (0, g)),
                pl.BlockSpec((GW, GW), lambda b, g, c: (0, 0)),
                pl.BlockSpec((GW, GW), lambda b, g, c: (0, 0)),
                pl.BlockSpec((None, CH, CH), lambda b, g, c: (d, 0, 0)),
                pl.BlockSpec((None, 2, CH, GW), lambda b, g, c: (d, 0, 0, 0)),
                pl.BlockSpec(consts["lvl"].shape, lambda b, g, c: (0, 0, 0))]
    args = [r, k, v, lw, asg, kp, consts["bm"], consts["eye"], consts["tri"], consts["cat"],
            consts["lvl"]]
    if want_state:
        in_specs.append(pl.BlockSpec((GW, HD), lambda b, g, c: (0, 0)))
        args.append(consts["sel"])
    if not zero_init:
        in_specs.append(pl.BlockSpec((None, ngs, GW, GW), lambda b, g, c: (b, g, 0, 0)))
        args.append(z0)
    out_shape = [jax.ShapeDtypeStruct((rows, e), F32)]
    out_specs = [tok]
    if want_state:
        out_shape.append(jax.ShapeDtypeStruct((batch, e, HD), F32))
        out_specs.append(pl.BlockSpec((None, sw, HD), lambda b, g, c: (b, g, 0)))
    res = pl.pallas_call(
        functools.partial(_rwkv_scan_body, reverse=reverse, zero_init=zero_init,
                          want_state=want_state, n_chunks=nc),
        out_shape=out_shape,
        grid=(batch, ng, nc),
        in_specs=in_specs,
        out_specs=out_specs,
        scratch_shapes=[pltpu.VMEM((ngs, GW, GW), F32)],
        compiler_params=_cparams("parallel", "parallel", "arbitrary"),
        name="rwkv_scan",
    )(*args)
    return res if want_state else (res[0], None)


def _rwkv_out_body(y0_ref, y1_ref, r_ref, k_ref, v_ref, a0_ref, a1_ref, z_ref, hp_ref, bm_ref, o_ref):
    bm_b = bm_ref[...].astype(BF)

    def headsum(x):
        return sum(_dot(p, bm_b) for p in _split3(x))

    y = y0_ref[...] + y1_ref[...]
    mean = headsum(y) * (1.0 / HD)
    yc = y - mean
    var = headsum(yc * yc) * (1.0 / HD)
    out = yc * lax.rsqrt(var + GN_EPS) * hp_ref[2:3] + hp_ref[3:4]
    r = r_ref[...]
    k = k_ref[...]
    v = v_ref[...]
    rk = r * k * hp_ref[1:2]
    k_a = hp_ref[0:1]
    for a_ref in (a0_ref, a1_ref):
        out = out + headsum(rk * (1.0 + (a_ref[...] - 1.0) * k_a)) * v
    z = z_ref[...]
    o_ref[...] = (out * (z * jax.nn.sigmoid(z))).astype(o_ref.dtype)


def _rwkv_out(y0, y1, r, k, v, a0, a1, z, hp, bm, *, tm):
    rows, e = r.shape
    tok = pl.BlockSpec((tm, GW), lambda i, g: (i, g))
    return pl.pallas_call(
        _rwkv_out_body,
        out_shape=jax.ShapeDtypeStruct((rows, e), BF),
        grid=(rows // tm, e // GW),
        in_specs=[tok] * 8 + [pl.BlockSpec((4, GW), lambda i, g: (0, g)),
                              pl.BlockSpec((GW, GW), lambda i, g: (0, 0))],
        out_specs=tok,
        compiler_params=_cparams("parallel", "parallel"),
        name="rwkv_out",
    )(y0, y1, r, k, v, a0, a1, z, hp, bm)


def _silu(x):
    return x * jax.nn.sigmoid(x)


def _act_tanh(x):
    return jnp.tanh(x)


def _act_sigmoid(x):
    return jax.nn.sigmoid(x)


def _act_logdecay(x):
    return -jnp.exp(-0.5) * jax.nn.sigmoid(x)


def _act_vmix(acc, v_first, mix):
    return acc + (v_first - acc) * mix


def _trunk(x, mods, init_lru, init_rwkv, W, *, batch, seq_len, grid_mode, want_state):
    rows, d = x.shape
    e = W["lru_w_out"].shape[1]
    depth = W["norm_pre"].shape[0]
    tm_row = min(256, seq_len)
    tm = min(1024, seq_len)
    nb = seq_len // tm
    m_tiles = rows // tm
    rb = batch
    lru_tr = max(2048 // rb, 1) * rb if rb % SUBLANES == 0 else 2048
    lru_tr = min(lru_tr, rows)
    cond_rows = rows // mods[0].shape[0]
    tn_in = min(1024, 2 * e)
    fin_lru, fin_rwkv = [], []
    v_first = None

    for i in range(depth):
        mod = mods[i]
        j = i // 2
        if i % 2 == 0:
            h = _prenorm(x, W["norm_pre"], i, mod, tm=tm_row, rows_per_cond=cond_rows)
            xz = _mm(h, W["lru_w_in"], j, None, tm=tm, tn=tn_in, m_tiles=m_tiles,
                     o_shape=(seq_len, batch * 2 * e),
                     o_imap=lambda n, m: (m % nb, (m // nb) * (2 * e // tn_in) + n), out_dtype=BF,
                     name="lru_in")
            xz = xz.reshape(seq_len * batch, 2 * e)
            y0, hf0 = _lru_core(xz, W["lru"], j, 0, init_lru[j][0], rb, tr=lru_tr)
            y, hf1 = _lru_core(xz, W["lru"], j, 1, init_lru[j][1], rb, tr=lru_tr, y_prev=y0)
            fin_lru.append((hf0, hf1))
            y = y.reshape(seq_len, batch * e)
            m = _mm(y, W["lru_w_out"], j, None, tm=tm, tn=1024, m_tiles=m_tiles,
                    a_imap=lambda n, m: (m % nb, m // nb), name="lru_out")
        else:
            h, xr, xw, xk, xv, xa = _shiftmix(
                x, W["norm_pre"], i, mod, W["rwkv_mu"][j], seq_len=seq_len, cond_rows=cond_rows, grid_mode=grid_mode,
                tm=seq_len if not grid_mode else min(512, seq_len))
            mk = dict(tm=tm, m_tiles=m_tiles)
            r = _mm(xr, W["rwkv_w_r"], j, None, tn=1024, name="rwkv_r", **mk)
            k = _mm(xk, W["rwkv_w_k"], j, None, tn=1024, name="rwkv_k", **mk)
            z = _mm(h, W["rwkv_w_g"], j, None, tn=1024, name="rwkv_g", **mk)
            if j == 0:
                v = _mm(xv, W["rwkv_w_v"], j, None, tn=1024, name="rwkv_v", **mk)
                v_first = v
            else:
                tv = _mm(xv, W["rwkv_v1"], j - 1, None, tn=128, out_dtype=BF, name="rwkv_v1", **mk)
                mix = _mm(tv, W["rwkv_v2"], j - 1, W["rwkv_v0"][j - 1], tn=1024, act=_act_sigmoid,
                          name="rwkv_v2", **mk)
                v = _mm(xv, W["rwkv_w_v"], j, None, tn=1024, extras=(v_first, mix), act=_act_vmix,
                        name="rwkv_v", **mk)
            tw = _mm(xw, W["rwkv_w1"], j, None, tn=256, act=_act_tanh, out_dtype=BF, name="rwkv_w1", **mk)
            ta = _mm(xa, W["rwkv_a1"], j, None, tn=256, out_dtype=BF, name="rwkv_a1", **mk)
            ys, fins, asgs = [], [], []
            for dd in range(2):
                sel_k = (lambda dd_: (lambda n, m: (m, dd_)))(dd)
                lw = _mm(tw, W["rwkv_w2"], j * 2 + dd, W["rwkv_w0"][j * 2 + dd], tn=1024, a_imap=sel_k,
                         act=_act_logdecay, name="rwkv_w2", **mk)
                asg = _mm(ta, W["rwkv_a2"], j * 2 + dd, W["rwkv_a0"][j * 2 + dd], tn=1024, a_imap=sel_k,
                          act=_act_sigmoid, name="rwkv_a2", **mk)
                z0 = None if init_rwkv is None else init_rwkv[j][dd]
                y_d, s_d = _rwkv_scan(r, k, v, lw, asg, W["rwkv_kp"][j], W["consts"], dd, batch=batch,
                                      seq_len=seq_len, z0=z0, want_state=want_state)
                ys.append(y_d)
                fins.append(s_d)
                asgs.append(asg)
            fin_rwkv.append(fins)
            yo = _rwkv_out(ys[0], ys[1], r, k, v, asgs[0], asgs[1], z, W["rwkv_hp"][j], W["consts"]["bm"],
                           tm=min(1024, rows))
            m = _mm(yo, W["rwkv_w_o"], j, None, tn=1024, name="rwkv_o", **mk)
        x = _postnorm(x, m, W["norm_post"], i, mod, tm=tm_row, rows_per_cond=cond_rows)
    return x, fin_lru, fin_rwkv


def _scan_consts():
    idx = jnp.arange(GW)
    bm = ((idx[:, None] // HD) == (idx[None, :] // HD)).astype(F32)
    eye = (idx[:, None] == idx[None, :]).astype(F32)
    t_r, t_c = jnp.arange(CH)[:, None], idx[None, :] % CH
    cats, tris = [], []
    ic = jnp.arange(CH)
    for d in range(2):
        ge = (t_r >= t_c) if d == 0 else (t_r <= t_c)
        gt = (t_r > t_c) if d == 0 else (t_r < t_c)
        cats.append(jnp.stack([ge.astype(F32), gt.astype(F32)]))
        tris.append(((ic[:, None] >= ic[None, :]) if d == 0 else (ic[:, None] <= ic[None, :])).astype(BF))
    sel = (idx[:, None] % HD == jnp.arange(HD)[None, :]).astype(BF)
    lvl = [t_r // INV_BASE == t_c // INV_BASE]
    size = INV_BASE
    while size < CH:
        lvl.append((t_r // (2 * size) == t_c // (2 * size)) & (t_r // size != t_c // size))
        size *= 2
    return dict(bm=bm, eye=eye, cat=jnp.stack(cats), tri=jnp.stack(tris), sel=sel,
                lvl=jnp.stack(lvl).astype(F32))


def kernel(x_prompt, x_sample, state_lru, state_rwkv, c, c_ctx, ada_w, ada_b, norm_pre, norm_post, lru_w_in, lru_conv_w, lru_conv_b, lru_gate_w, lru_gate_b, lru_lambda, lru_w_out, rwkv_mu, rwkv_w_r, rwkv_w_k, rwkv_w_v, rwkv_w_g, rwkv_w_o, rwkv_w0, rwkv_w1, rwkv_w2, rwkv_a0, rwkv_a1, rwkv_a2, rwkv_k_k, rwkv_k_a, rwkv_r_k, rwkv_ln_w, rwkv_ln_b, rwkv_v0, rwkv_v1, rwkv_v2):
    bp, tp, d = x_prompt.shape
    bs, ts, _ = x_sample.shape
    depth = ada_w.shape[0]
    n_lru, _, e2 = lru_w_in.shape
    e = e2 // 2
    n_rwkv = rwkv_w_r.shape[0]
    nblk = e // LB
    heads = e // HD
    ng = e // GW

    def merge_lead(w, k):
        return w.reshape((-1,) + w.shape[k:])

    gate_w = jnp.concatenate([lru_gate_w[:, :, 0], lru_gate_w[:, :, 1]], axis=-1)
    gate_b = lru_gate_b.reshape(n_lru, 2, 2, nblk, LB).transpose(0, 1, 3, 2, 4)
    lora_v = rwkv_v1.shape[-1]
    pad_v = (-lora_v) % 128
    W = dict(
        norm_pre=norm_pre.reshape(depth, 1, d), norm_post=norm_post.reshape(depth, 1, d),
        lru_w_in=lru_w_in.astype(BF), lru_w_out=lru_w_out.astype(BF),
        lru=dict(conv_w=lru_conv_w, conv_b=lru_conv_b.reshape(n_lru, 1, e),
                 gate_w=merge_lead(gate_w, 3).astype(BF),
                 gate_b=gate_b.reshape(n_lru * 2 * nblk, 1, 2 * LB),
                 lam=lru_lambda.reshape(n_lru * 2, 1, e)),
        rwkv_mu=rwkv_mu,
        rwkv_w_r=rwkv_w_r.astype(BF), rwkv_w_k=rwkv_w_k.astype(BF), rwkv_w_v=rwkv_w_v.astype(BF),
        rwkv_w_g=rwkv_w_g.astype(BF), rwkv_w_o=rwkv_w_o.astype(BF),
        rwkv_w1=jnp.concatenate([rwkv_w1[:, 0], rwkv_w1[:, 1]], axis=-1).astype(BF),
        rwkv_a1=jnp.concatenate([rwkv_a1[:, 0], rwkv_a1[:, 1]], axis=-1).astype(BF),
        rwkv_w2=merge_lead(rwkv_w2, 2).astype(BF), rwkv_a2=merge_lead(rwkv_a2, 2).astype(BF),
        rwkv_w0=rwkv_w0.reshape(n_rwkv * 2, 1, e), rwkv_a0=rwkv_a0.reshape(n_rwkv * 2, 1, e),
        rwkv_v0=rwkv_v0.reshape(-1, 1, e),
        rwkv_v1=jnp.pad(rwkv_v1, ((0, 0), (0, 0), (0, pad_v))).astype(BF),
        rwkv_v2=jnp.pad(rwkv_v2, ((0, 0), (0, pad_v), (0, 0))).astype(BF),
        rwkv_kp=jnp.stack([rwkv_k_k, rwkv_k_a], axis=1),
        rwkv_hp=jnp.stack([rwkv_k_a, rwkv_r_k, rwkv_ln_w, rwkv_ln_b], axis=1),
        consts=_scan_consts(),
    )

    n_cond = 1 + bs
    cond = jnp.zeros((SUBLANES * pl.cdiv(n_cond, SUBLANES), d), F32)
    cond = cond.at[0].set(c_ctx).at[1:n_cond].set(c)
    ada_wb = ada_w.astype(BF)
    mods_p, mods_s = [], []
    for i in range(depth):
        mod = _mm(cond, ada_wb, i, ada_b[i][None], tm=cond.shape[0], tn=256, m_tiles=1,
                  a_pre=_silu, name="adaln")
        mod = mod.reshape(cond.shape[0], 3, d)
        mods_p.append(mod[0:1])
        mods_s.append(mod[1:n_cond])

    zero_lru = [(jnp.zeros((bp, e), F32), jnp.zeros((bp, e), F32))] * n_lru
    y_p, fin_lru, fin_rwkv = _trunk(x_prompt.reshape(bp * tp, d), mods_p, zero_lru, None, W,
                                    batch=bp, seq_len=tp, grid_mode=False, want_state=True)
    new_state_lru = jnp.stack([jnp.stack(f, axis=1) for f in fin_lru], axis=1)
    new_state_rwkv = jnp.stack(
        [jnp.stack([s.reshape(bp, heads, HD, HD) for s in f], axis=1) for f in fin_rwkv], axis=1)

    rep = max(SUBLANES // bs, 1)
    init_lru = [tuple(jnp.tile(state_lru[:, jj, dd], (rep, 1)) for dd in range(2)) for jj in range(n_lru)]
    eye_g = jnp.eye(GH, dtype=F32)
    st = state_rwkv.reshape(bs, n_rwkv, 2, ng, GH, HD, HD)
    z0 = jnp.einsum("bldghvk,hi->bldghkiv", st, eye_g).reshape(bs, n_rwkv, 2, ng, GW, GW)
    init_rwkv = [[z0[:, jj, dd] for dd in range(2)] for jj in range(n_rwkv)]
    y_s, _, _ = _trunk(x_sample.reshape(bs * ts, d), mods_s, init_lru, init_rwkv, W,
                       batch=bs, seq_len=ts, grid_mode=True, want_state=False)
    return (y_p.reshape(bp, tp, d), y_s.reshape(bs, ts, d), new_state_lru, new_state_rwkv)
```

```python
import functools

import jax
import jax.numpy as jnp
from jax import lax
from jax.experimental import pallas as pl
from jax.experimental.pallas import tpu as pltpu

F32 = jnp.float32
BF = jnp.bfloat16

NORM_EPS = 1e-6
GN_EPS = 64e-5
LRU_C = 8.0
GRID_W = 64
CONV_W = 4
CONV_LEFT = 2

HD = 64
GH = 4
GW = GH * HD
CH = 64
INV_BASE = 16
SCAN_GROUPS = 16
LB = 256
SUBLANES = 8
VMEM_LIMIT = 56 * 1024 * 1024


def _cparams(*sem):
    return pltpu.CompilerParams(dimension_semantics=sem, vmem_limit_bytes=VMEM_LIMIT)


def _dot(a, b):
    return jnp.dot(a, b, preferred_element_type=F32)


def _dot_nt(a, b):
    return lax.dot_general(a, b, (((1,), (1,)), ((), ())), preferred_element_type=F32)


def _dot_tn(a, b):
    return lax.dot_general(a, b, (((0,), (0,)), ((), ())), preferred_element_type=F32)


def _split3(x):
    h1 = x.astype(BF)
    r1 = x - h1.astype(F32)
    h2 = r1.astype(BF)
    h3 = (r1 - h2.astype(F32)).astype(BF)
    return h1, h2, h3


def _modnorm(x, g, mod):
    y = x * lax.rsqrt(jnp.mean(x * x, axis=-1, keepdims=True) + NORM_EPS) * g
    return y * (1.0 + mod[1:2]) + mod[0:1]


def _mm_body(*refs, act, a_pre, n_extra):
    a_ref, w_ref, b_ref = refs[:3]
    extras = refs[3:3 + n_extra]
    o_ref = refs[3 + n_extra]
    a = a_ref[...]
    if a_pre is not None:
        a = a_pre(a.astype(F32))
    acc = _dot(a.astype(BF), w_ref[...]) + b_ref[...]
    if act is not None:
        acc = act(acc, *[e[...] for e in extras])
    o_ref[...] = acc.astype(o_ref.dtype)


def _mm(a, w3, li, bias, *, tm, tn, m_tiles, a_imap=None, o_shape=None, o_imap=None,
        extras=(), act=None, a_pre=None, out_dtype=F32, name="mm"):
    _, K, N = w3.shape
    tn = min(tn, N)
    assert N % tn == 0 and K % 128 == 0
    n_tiles = N // tn
    if a_imap is None:
        a_imap = lambda n, m: (m, 0)
    if o_imap is None:
        o_imap = lambda n, m: (m, n)
    if o_shape is None:
        o_shape = (m_tiles * tm, N)
    if bias is None:
        bias = jnp.zeros((1, N), F32)
    in_specs = [
        pl.BlockSpec((tm, K), a_imap),
        pl.BlockSpec((None, K, tn), lambda n, m: (li, 0, n)),
        pl.BlockSpec((1, tn), lambda n, m: (0, n)),
    ] + [pl.BlockSpec((tm, tn), lambda n, m: (m, n)) for _ in extras]
    return pl.pallas_call(
        functools.partial(_mm_body, act=act, a_pre=a_pre, n_extra=len(extras)),
        out_shape=jax.ShapeDtypeStruct(o_shape, out_dtype),
        grid=(n_tiles, m_tiles),
        in_specs=in_specs,
        out_specs=pl.BlockSpec((tm, tn), o_imap),
        compiler_params=_cparams("parallel", "parallel"),
        name=name,
    )(a, w3, bias, *extras)


def _prenorm_body(x_ref, g_ref, mod_ref, o_ref):
    o_ref[...] = _modnorm(x_ref[...], g_ref[...], mod_ref[0]).astype(o_ref.dtype)


def _prenorm(x, g3, layer, mod, *, tm, rows_per_cond):
    rows, d = x.shape
    return pl.pallas_call(
        _prenorm_body,
        out_shape=jax.ShapeDtypeStruct((rows, d), BF),
        grid=(rows // tm,),
        in_specs=[
            pl.BlockSpec((tm, d), lambda i: (i, 0)),
            pl.BlockSpec((None, 1, d), lambda i: (layer, 0, 0)),
            pl.BlockSpec((1, 3, d), lambda i: ((i * tm) // rows_per_cond, 0, 0)),
        ],
        out_specs=pl.BlockSpec((tm, d), lambda i: (i, 0)),
        compiler_params=_cparams("parallel"),
        name="prenorm",
    )(x, g3, mod)


def _postnorm_body(x_ref, m_ref, g_ref, mod_ref, o_ref):
    m = m_ref[...]
    y = m * lax.rsqrt(jnp.mean(m * m, axis=-1, keepdims=True) + NORM_EPS) * g_ref[...]
    o_ref[...] = x_ref[...] + mod_ref[0][2:3] * y


def _postnorm(x, m, g3, layer, mod, *, tm, rows_per_cond):
    rows, d = x.shape
    return pl.pallas_call(
        _postnorm_body,
        out_shape=jax.ShapeDtypeStruct((rows, d), F32),
        grid=(rows // tm,),
        in_specs=[
            pl.BlockSpec((tm, d), lambda i: (i, 0)),
            pl.BlockSpec((tm, d), lambda i: (i, 0)),
            pl.BlockSpec((None, 1, d), lambda i: (layer, 0, 0)),
            pl.BlockSpec((1, 3, d), lambda i: ((i * tm) // rows_per_cond, 0, 0)),
        ],
        out_specs=pl.BlockSpec((tm, d), lambda i: (i, 0)),
        compiler_params=_cparams("parallel"),
        name="postnorm",
    )(x, m, g3, mod)


def _write_mixes(h, shifted, mu_ref, out_refs):
    xx = shifted - h
    out_refs[0][...] = h.astype(BF)
    for m in range(5):
        out_refs[m + 1][...] = (h + xx * mu_ref[m:m + 1]).astype(BF)


def _shiftmix_seq_body(x_ref, g_ref, mod_ref, mu_ref, *out_refs):
    h = _modnorm(x_ref[...], g_ref[...], mod_ref[0])
    t_len, d = h.shape
    half = d // 2
    t = lax.broadcasted_iota(jnp.int32, (t_len, 1), 0)
    prev = jnp.where(t == 0, 0.0, pltpu.roll(h[:, :half], 1, 0))
    nxt = jnp.where(t == t_len - 1, 0.0, pltpu.roll(h[:, half:], t_len - 1, 0))
    _write_mixes(h, jnp.concatenate([prev, nxt], axis=1), mu_ref, out_refs)


def _shiftmix_grid_body(x_ref, up_ref, dn_ref, g_ref, mod_ref, mu_ref, *out_refs, tiles_per_batch):
    ti = pl.program_id(0) % tiles_per_batch
    g = g_ref[...]
    mod = mod_ref[0]
    h = _modnorm(x_ref[...], g, mod)
    tr, d = h.shape
    q = d // 4
    h_up = jnp.where(ti > 0, _modnorm(up_ref[...], g, mod)[:, 2 * q:3 * q], 0.0)
    h_dn = jnp.where(ti < tiles_per_batch - 1, _modnorm(dn_ref[...], g, mod)[:, 3 * q:], 0.0)
    col = lax.broadcasted_iota(jnp.int32, (tr, 1), 0) % GRID_W
    left = jnp.where(col == 0, 0.0, pltpu.roll(h[:, :q], 1, 0))
    right = jnp.where(col == GRID_W - 1, 0.0, pltpu.roll(h[:, q:2 * q], tr - 1, 0))
    up = jnp.concatenate([h_up, h[:tr - GRID_W, 2 * q:3 * q]], axis=0)
    down = jnp.concatenate([h[GRID_W:, 3 * q:], h_dn], axis=0)
    _write_mixes(h, jnp.concatenate([left, right, up, down], axis=1), mu_ref, out_refs)


def _shiftmix(x, g3, layer, mod, mu, *, seq_len, cond_rows, grid_mode, tm):
    rows, d = x.shape
    outs = [jax.ShapeDtypeStruct((rows, d), BF)] * 6
    common = [
        pl.BlockSpec((None, 1, d), lambda i: (layer, 0, 0)),
        pl.BlockSpec((1, 3, d), lambda i: ((i * tm) // cond_rows, 0, 0)),
        pl.BlockSpec((5, d), lambda i: (0, 0)),
    ]
    out_specs = [pl.BlockSpec((tm, d), lambda i: (i, 0))] * 6
    if not grid_mode:
        assert tm == seq_len
        return pl.pallas_call(
            _shiftmix_seq_body, out_shape=outs, grid=(rows // tm,),
            in_specs=[pl.BlockSpec((tm, d), lambda i: (i, 0))] + common,
            out_specs=out_specs, compiler_params=_cparams("parallel"), name="shiftmix_seq",
        )(x, g3, mod, mu)
    per = tm // GRID_W
    n_halo = rows // GRID_W
    return pl.pallas_call(
        functools.partial(_shiftmix_grid_body, tiles_per_batch=seq_len // tm),
        out_shape=outs, grid=(rows // tm,),
        in_specs=[
            pl.BlockSpec((tm, d), lambda i: (i, 0)),
            pl.BlockSpec((GRID_W, d), lambda i: (jnp.maximum(i * per - 1, 0), 0)),
            pl.BlockSpec((GRID_W, d), lambda i: (jnp.minimum((i + 1) * per, n_halo - 1), 0)),
        ] + common,
        out_specs=out_specs, compiler_params=_cparams("parallel"), name="shiftmix_grid",
    )(x, x, x, g3, mod, mu)


def _lru_body(*refs, rb, reverse, combine, n_tiles):
    if combine:
        (x_ref, xp_ref, xn_ref, cw_ref, cb_ref, gw_ref, gb_ref, lam_ref, h0_ref, yp_ref, z_ref,
         y_ref, hf_ref, xs, a_s, u_s, hc) = refs
    else:
        (x_ref, xp_ref, xn_ref, cw_ref, cb_ref, gw_ref, gb_ref, lam_ref, h0_ref,
         y_ref, hf_ref, xs, a_s, u_s, hc) = refs
    tr = x_ref.shape[0]
    hb = xp_ref.shape[0]
    step = pl.program_id(1)
    t_idx = (n_tiles - 1 - step) if reverse else step

    @pl.when(step == 0)
    def _():
        hc[...] = h0_ref[...]

    xs[hb:hb + tr, :] = x_ref[...].astype(F32)
    xs[0:hb, :] = jnp.where(t_idx > 0, xp_ref[...].astype(F32), 0.0)
    xs[hb + tr:hb + tr + hb, :] = jnp.where(t_idx < n_tiles - 1, xn_ref[...].astype(F32), 0.0)
    xc = cb_ref[...]
    for tap in range(CONV_W):
        off = hb + (tap - CONV_LEFT) * rb
        xc = xc + xs[off:off + tr, :] * cw_ref[tap:tap + 1]
    g = _dot(xc.astype(BF), gw_ref[...]) + gb_ref[...]
    sg = 0.5 * jnp.tanh(0.5 * g) + 0.5
    r = sg[:, :LB]
    i = sg[:, LB:]
    log_a = r * (-LRU_C * jax.nn.softplus(-lam_ref[...]))
    a_s[...] = jnp.exp(log_a)
    th = jnp.tanh(log_a)
    u_s[...] = jnp.sqrt(-2.0 * th / (1.0 - th)) * (i * xc)

    if rb % SUBLANES == 0:
        n_steps = tr // rb

        def body(s, h):
            row = pl.multiple_of(((n_steps - 1 - s) if reverse else s) * rb, rb)
            h = a_s[pl.ds(row, rb), :] * h + u_s[pl.ds(row, rb), :]
            u_s[pl.ds(row, rb), :] = h
            return h

        h = lax.fori_loop(0, n_steps, body, hc[...], unroll=4)
    else:
        assert 2 * rb == SUBLANES
        n_steps = tr // SUBLANES
        first = lax.broadcasted_iota(jnp.int32, (SUBLANES, 1), 0) < rb

        def body(s, h):
            row = pl.multiple_of(((n_steps - 1 - s) if reverse else s) * SUBLANES, SUBLANES)
            a = a_s[pl.ds(row, SUBLANES), :]
            u = u_s[pl.ds(row, SUBLANES), :]
            h1 = a * h + u
            h2 = a * pltpu.roll(h1, rb, 0) + u
            u_s[pl.ds(row, SUBLANES), :] = jnp.where(first != reverse, h1, h2)
            return pltpu.roll(h2, rb, 0)

        h = lax.fori_loop(0, n_steps, body, hc[...], unroll=4)
    hc[...] = h
    hf_ref[...] = h
    if combine:
        z = z_ref[...].astype(F32)
        y_ref[...] = ((yp_ref[...] + u_s[...]) * (z * jax.nn.sigmoid(z))).astype(y_ref.dtype)
    else:
        y_ref[...] = u_s[...]


def _lru_core(xz, lru, j, d, h0, rb, *, tr, y_prev=None):
    rows, e2 = xz.shape
    e = e2 // 2
    nblk = e // LB
    n_tiles = rows // tr
    hb = max(2 * rb, 2 * SUBLANES)
    per = tr // hb
    n_halo = rows // hb
    hr = h0.shape[0]
    reverse = d == 1
    combine = y_prev is not None

    def tidx(s):
        return (n_tiles - 1 - s) if reverse else s

    in_specs = [
        pl.BlockSpec((tr, LB), lambda c, s: (tidx(s), c)),
        pl.BlockSpec((hb, LB), lambda c, s: (jnp.maximum(tidx(s) * per - 1, 0), c)),
        pl.BlockSpec((hb, LB), lambda c, s: (jnp.minimum((tidx(s) + 1) * per, n_halo - 1), c)),
        pl.BlockSpec((None, CONV_W, LB), lambda c, s: (j, 0, c)),
        pl.BlockSpec((None, 1, LB), lambda c, s: (j, 0, c)),
        pl.BlockSpec((None, LB, 2 * LB), lambda c, s: ((j * 2 + d) * nblk + c, 0, 0)),
        pl.BlockSpec((None, 1, 2 * LB), lambda c, s: ((j * 2 + d) * nblk + c, 0, 0)),
        pl.BlockSpec((None, 1, LB), lambda c, s: (j * 2 + d, 0, c)),
        pl.BlockSpec((hr, LB), lambda c, s: (0, c)),
    ]
    args = [xz, xz, xz, lru["conv_w"], lru["conv_b"], lru["gate_w"], lru["gate_b"], lru["lam"], h0]
    if combine:
        in_specs += [
            pl.BlockSpec((tr, LB), lambda c, s: (tidx(s), c)),
            pl.BlockSpec((tr, LB), lambda c, s: (tidx(s), nblk + c)),
        ]
        args += [y_prev, xz]
    return pl.pallas_call(
        functools.partial(_lru_body, rb=rb, reverse=reverse, combine=combine, n_tiles=n_tiles),
        out_shape=[jax.ShapeDtypeStruct((rows, e), BF if combine else F32),
                   jax.ShapeDtypeStruct((hr, e), F32)],
        grid=(nblk, n_tiles),
        in_specs=in_specs,
        out_specs=[pl.BlockSpec((tr, LB), lambda c, s: (tidx(s), c)),
                   pl.BlockSpec((hr, LB), lambda c, s: (0, c))],
        scratch_shapes=[pltpu.VMEM((tr + 2 * hb, LB), F32), pltpu.VMEM((tr, LB), F32),
                        pltpu.VMEM((tr, LB), F32), pltpu.VMEM((hr, LB), F32)],
        compiler_params=_cparams("parallel", "arbitrary"),
        name="lru_core",
    )(*args)


def _rwkv_scan_body(*refs, reverse, zero_init, want_state, finish, n_chunks):
    it = iter(refs)
    (r_ref, k_ref, v_ref, tw_ref, ta_ref, w2_ref, w0_ref, a2_ref, a0_ref, kp_ref, bm_ref, eye_ref,
     tri_ref, cat_ref, lvl_ref) = (next(it) for _ in range(15))
    if finish:
        tao_ref, a2o_ref, a0o_ref, yo_ref, zg_ref, hp_ref = (next(it) for _ in range(6))
    sel_ref = next(it) if want_state else None
    z0_ref = None if zero_init else next(it)
    y_ref = next(it)
    sf_ref = next(it) if want_state else None
    z_s = next(it)
    c = pl.program_id(2)

    @pl.when(c == 0)
    def _():
        z_s[...] = jnp.zeros_like(z_s) if zero_init else z0_ref[...]

    lw_all = _act_logdecay(_dot(tw_ref[...], w2_ref[...]) + w0_ref[...])
    as_all = jax.nn.sigmoid(_dot(ta_ref[...], a2_ref[...]) + a0_ref[...])

    groups = range(z_s.shape[0])
    lanes = [slice(g * GW, (g + 1) * GW) for g in groups]
    bm = bm_ref[...]
    rs = [r_ref[:, l] for l in lanes]
    ks = [k_ref[:, l] for l in lanes]
    vs = [v_ref[:, l] for l in lanes]
    ys, z_new = _rwkv_chunk(
        rs, ks, vs, [lw_all[:, l] for l in lanes], [as_all[:, l] for l in lanes],
        [kp_ref[:, l] for l in lanes], bm, eye_ref[...], tri_ref[...], cat_ref, lvl_ref,
        [z_s[g] for g in groups], reverse=reverse)
    for g in groups:
        z_s[g] = z_new[g]
    if not finish:
        for g in groups:
            y_ref[:, lanes[g]] = ys[g]
    else:
        bm_b = bm.astype(BF)
        aso_all = jax.nn.sigmoid(_dot(tao_ref[...], a2o_ref[...]) + a0o_ref[...])

        def headsum(x):
            return sum(_dot(p, bm_b) for p in _split3(x)[:2])

        ysum = [ys[g] + yo_ref[:, lanes[g]] for g in groups]
        arg = [rs[g] * ks[g] * hp_ref[1:2, lanes[g]]
               * (2.0 + hp_ref[0:1, lanes[g]] * (as_all[:, lanes[g]] + aso_all[:, lanes[g]] - 2.0))
               for g in groups]
        sums = [headsum(jnp.concatenate([ysum[g], arg[g]], axis=0)) for g in groups]
        yc = [ysum[g] - sums[g][:CH] * (1.0 / HD) for g in groups]
        var = [headsum(yc[g] * yc[g]) * (1.0 / HD) for g in groups]
        for g in groups:
            out = (yc[g] * lax.rsqrt(var[g] + GN_EPS) * hp_ref[2:3, lanes[g]] + hp_ref[3:4, lanes[g]]
                   + sums[g][CH:] * vs[g])
            zg = zg_ref[:, lanes[g]]
            y_ref[:, lanes[g]] = (out * (zg * jax.nn.sigmoid(zg))).astype(y_ref.dtype)
    if want_state:
        @pl.when(c == n_chunks - 1)
        def _():
            sel = sel_ref[...]
            for g in groups:
                sf_ref[lanes[g], :] = sum(_dot_tn(p, sel) for p in _split3(z_new[g])[:2])


def _vmap(fn, *lists):
    return [fn(*args) for args in zip(*lists)]


def _rwkv_chunk(r, k, v, lw, asg, kp, bm, eye_bd, tri, cat_ref, lvl_ref, z, *, reverse):
    incl = cat_ref[0]
    strict = cat_ref[1]
    eye_c = incl - strict
    bm_b = bm.astype(BF)

    def headsum(x):
        return sum(_dot(p, bm_b) for p in _split3(x))

    def bdm(x):
        return jnp.concatenate([x.astype(BF)] * GH, axis=0) * bm_b

    def mm(a, b_bd):
        return _vmap(lambda x, y: _dot(x.astype(BF), y), a, b_bd)

    def stack2(a, b):
        return _vmap(lambda x, y: jnp.concatenate([x, y], axis=0), a, b)

    def top(a):
        return [x[:CH] for x in a]

    def bot(a):
        return [x[CH:] for x in a]

    def masked(a, m):
        return [jnp.where(m > 0, x, 0.0) for x in a]

    kk = _vmap(lambda k_, p: k_ * p[0:1], k, kp)
    ss = _vmap(lambda x: headsum(x * x), kk)
    kk = _vmap(lambda x, s: x / jnp.maximum(jnp.sqrt(s), 1e-12), kk, ss)
    kd = _vmap(lambda k_, a, p: k_ * (1.0 + (a - 1.0) * p[1:2]), k, asg, kp)
    bvec = _vmap(lambda x, a: x * a, kk, asg)

    cum = _vmap(lambda x: sum(_dot(tri, p) for p in _split3(x)), lw)
    tot = _vmap(lambda x: x[0:1] if reverse else x[CH - 1:CH], cum)
    e_pos = _vmap(jnp.exp, cum)
    e_neg = _vmap(lambda x: jnp.exp(-x), cum)
    e_rem = _vmap(lambda t, en: jnp.exp(t) * en, tot, e_neg)
    a_t = _vmap(lambda x, ep, l: -x * (ep * jnp.exp(-l)), kk, e_pos, lw)
    r_t = _vmap(lambda x, ep: x * ep, r, e_pos)
    ar = [x.astype(BF) for x in stack2(a_t, r_t)]
    bt_bd = _vmap(lambda x, en: bdm(x * en), bvec, e_neg)
    kt_bd = _vmap(lambda x, en: bdm(x * en), kd, e_neg)
    ab = _vmap(_dot_nt, ar, bt_bd)
    ak = _vmap(_dot_nt, ar, kt_bd)
    a_ab, a_rb = masked(top(ab), strict), masked(bot(ab), incl)
    a_ak, a_rk = masked(top(ak), strict), masked(bot(ak), incl)

    p = [n * lvl_ref[0] for n in a_ab]
    x = [eye_c + n for n in p]
    pw = mm(p, _vmap(bdm, p))
    e = 2
    while e < INV_BASE:
        rhs = _vmap(bdm, pw)
        if 2 * e < INV_BASE:
            both = mm(stack2(x, pw), rhs)
            x = _vmap(jnp.add, x, top(both))
            pw = bot(both)
        else:
            x = _vmap(jnp.add, x, mm(x, rhs))
        e *= 2
    for lvl in range(1, lvl_ref.shape[0]):
        off = [n * lvl_ref[lvl] for n in a_ab]
        x = _vmap(jnp.add, x, mm(x, _vmap(bdm, mm(off, _vmap(bdm, x)))))

    s_v = mm(stack2(a_ak, a_rk), _vmap(bdm, v))
    s_z = _vmap(lambda a, z_: _dot(a, z_.astype(BF)), ar, z)
    u = mm(x, _vmap(lambda a, b_: bdm(a + b_), top(s_z), top(s_v)))
    y = _vmap(lambda a, b_, c_: a + b_ + c_, bot(s_z), bot(s_v), mm(a_rb, _vmap(bdm, u)))
    upd = _vmap(
        lambda b_, k_, er, u_, v_: _dot_tn(jnp.concatenate([b_ * er, k_ * er], axis=0).astype(BF),
                                           jnp.concatenate([u_, v_], axis=0).astype(BF)) * bm,
        bvec, kd, e_rem, u, v)
    z_new = _vmap(lambda z_, t, up: z_ * jnp.sum(eye_bd * jnp.exp(t), axis=1, keepdims=True) + up,
                  z, tot, upd)
    return y, z_new


def _rwkv_scan(r, k, v, tw, ta, lora, jd, kp, consts, d, *, batch, seq_len, z0=None, want_state=False,
               finish=None):
    rows, e = r.shape
    ngs = min(SCAN_GROUPS, e // GW)
    sw = ngs * GW
    ng = e // sw
    nc = seq_len // CH
    reverse = d == 1
    zero_init = z0 is None
    rank = lora["w2"].shape[1]

    def cidx(c):
        return (nc - 1 - c) if reverse else c

    def lora_specs(dd, jdd):
        return [pl.BlockSpec((CH, rank), lambda b, g, c: (b * nc + cidx(c), dd)),
                pl.BlockSpec((None, rank, sw), lambda b, g, c: (jdd, 0, g)),
                pl.BlockSpec((None, 1, sw), lambda b, g, c: (jdd, 0, g))]

    tok = pl.BlockSpec((CH, sw), lambda b, g, c: (b * nc + cidx(c), g))
    w_specs, a_specs = lora_specs(d, jd), lora_specs(d, jd)
    in_specs = [tok, tok, tok, w_specs[0], a_specs[0], w_specs[1], w_specs[2], a_specs[1], a_specs[2],
                pl.BlockSpec((2, sw), lambda b, g, c: (0, g)),
                pl.BlockSpec((GW, GW), lambda b, g, c: (0, 0)),
                pl.BlockSpec((GW, GW), lambda b, g, c: (0, 0)),
                pl.BlockSpec((None, CH, CH), lambda b, g, c: (d, 0, 0)),
                pl.BlockSpec((None, 2, CH, GW), lambda b, g, c: (d, 0, 0, 0)),
                pl.BlockSpec(consts["lvl"].shape, lambda b, g, c: (0, 0, 0))]
    args = [r, k, v, tw, ta, lora["w2"], lora["w0"], lora["a2"], lora["a0"], kp, consts["bm"],
            consts["eye"], consts["tri"], consts["cat"], consts["lvl"]]
    if finish is not None:
        y_other, zg, hp = finish
        in_specs += lora_specs(1 - d, jd + 1 - 2 * d) + [tok, tok, pl.BlockSpec((4, sw), lambda b, g, c: (0, g))]
        args += [ta, lora["a2"], lora["a0"], y_other, zg, hp]
    if want_state:
        in_specs.append(pl.BlockSpec((GW, HD), lambda b, g, c: (0, 0)))
        args.append(consts["sel"])
    if not zero_init:
        in_specs.append(pl.BlockSpec((None, ngs, GW, GW), lambda b, g, c: (b, g, 0, 0)))
        args.append(z0)
    out_shape = [jax.ShapeDtypeStruct((rows, e), F32 if finish is None else BF)]
    out_specs = [tok]
    if want_state:
        out_shape.append(jax.ShapeDtypeStruct((batch, e, HD), F32))
        out_specs.append(pl.BlockSpec((None, sw, HD), lambda b, g, c: (b, g, 0)))
    res = pl.pallas_call(
        functools.partial(_rwkv_scan_body, reverse=reverse, zero_init=zero_init,
                          want_state=want_state, finish=finish is not None, n_chunks=nc),
        out_shape=out_shape,
        grid=(batch, ng, nc),
        in_specs=in_specs,
        out_specs=out_specs,
        scratch_shapes=[pltpu.VMEM((ngs, GW, GW), F32)],
        compiler_params=_cparams("parallel", "parallel", "arbitrary"),
        name="rwkv_scan",
    )(*args)
    return res if want_state else (res[0], None)


def _silu(x):
    return x * jax.nn.sigmoid(x)


def _act_tanh(x):
    return jnp.tanh(x)


def _act_sigmoid(x):
    return jax.nn.sigmoid(x)


def _act_logdecay(x):
    return -jnp.exp(-0.5) * jax.nn.sigmoid(x)


def _act_vmix(acc, v_first, mix):
    return acc + (v_first - acc) * mix


def _trunk(x, mods, init_lru, init_rwkv, W, *, batch, seq_len, grid_mode, want_state):
    rows, d = x.shape
    e = W["lru_w_out"].shape[1]
    depth = W["norm_pre"].shape[0]
    tm_row = min(256, seq_len)
    tm = min(1024, seq_len)
    nb = seq_len // tm
    m_tiles = rows // tm
    rb = batch
    lru_tr = max(2048 // rb, 1) * rb if rb % SUBLANES == 0 else 2048
    lru_tr = min(lru_tr, rows)
    cond_rows = rows // mods[0].shape[0]
    tn_in = min(1024, 2 * e)
    fin_lru, fin_rwkv = [], []
    v_first = None

    for i in range(depth):
        mod = mods[i]
        j = i // 2
        if i % 2 == 0:
            h = _prenorm(x, W["norm_pre"], i, mod, tm=tm_row, rows_per_cond=cond_rows)
            xz = _mm(h, W["lru_w_in"], j, None, tm=tm, tn=tn_in, m_tiles=m_tiles,
                     o_shape=(seq_len, batch * 2 * e),
                     o_imap=lambda n, m: (m % nb, (m // nb) * (2 * e // tn_in) + n), out_dtype=BF,
                     name="lru_in")
            xz = xz.reshape(seq_len * batch, 2 * e)
            y0, hf0 = _lru_core(xz, W["lru"], j, 0, init_lru[j][0], rb, tr=lru_tr)
            y, hf1 = _lru_core(xz, W["lru"], j, 1, init_lru[j][1], rb, tr=lru_tr, y_prev=y0)
            fin_lru.append((hf0, hf1))
            y = y.reshape(seq_len, batch * e)
            m = _mm(y, W["lru_w_out"], j, None, tm=tm, tn=1024, m_tiles=m_tiles,
                    a_imap=lambda n, m: (m % nb, m // nb), name="lru_out")
        else:
            h, xr, xw, xk, xv, xa = _shiftmix(
                x, W["norm_pre"], i, mod, W["rwkv_mu"][j], seq_len=seq_len, cond_rows=cond_rows, grid_mode=grid_mode,
                tm=seq_len if not grid_mode else min(512, seq_len))
            tm_plain = min(1024, rows)
            mk = dict(tm=tm_plain, m_tiles=rows // tm_plain)
            r = _mm(xr, W["rwkv_w_r"], j, None, tn=1024, name="rwkv_r", **mk)
            k = _mm(xk, W["rwkv_w_k"], j, None, tn=1024, name="rwkv_k", **mk)
            z = _mm(h, W["rwkv_w_g"], j, None, tn=1024, name="rwkv_g", **mk)
            if j == 0:
                v = _mm(xv, W["rwkv_w_v"], j, None, tn=1024, name="rwkv_v", **mk)
                v_first = v
            else:
                tv = _mm(xv, W["rwkv_v1"], j - 1, None, tn=128, out_dtype=BF, name="rwkv_v1", **mk)
                mix = _mm(tv, W["rwkv_v2"], j - 1, W["rwkv_v0"][j - 1], tn=1024, act=_act_sigmoid,
                          name="rwkv_v2", **mk)
                v = _mm(xv, W["rwkv_w_v"], j, None, tn=1024, extras=(v_first, mix), act=_act_vmix,
                        name="rwkv_v", **mk)
            tw = _mm(xw, W["rwkv_w1"], j, None, tn=256, act=_act_tanh, out_dtype=BF, name="rwkv_w1", **mk)
            ta = _mm(xa, W["rwkv_a1"], j, None, tn=256, out_dtype=BF, name="rwkv_a1", **mk)
            fins = []
            y_d = None
            for dd in range(2):
                z0 = None if init_rwkv is None else init_rwkv[j][dd]
                finish = None if dd == 0 else (y_d, z, W["rwkv_hp"][j])
                y_d, s_d = _rwkv_scan(r, k, v, tw, ta, W["rwkv_lora"], j * 2 + dd, W["rwkv_kp"][j],
                                      W["consts"], dd, batch=batch, seq_len=seq_len, z0=z0,
                                      want_state=want_state, finish=finish)
                fins.append(s_d)
            fin_rwkv.append(fins)
            m = _mm(y_d, W["rwkv_w_o"], j, None, tn=1024, name="rwkv_o", **mk)
        x = _postnorm(x, m, W["norm_post"], i, mod, tm=tm_row, rows_per_cond=cond_rows)
    return x, fin_lru, fin_rwkv


def _scan_consts():
    idx = jnp.arange(GW)
    bm = ((idx[:, None] // HD) == (idx[None, :] // HD)).astype(F32)
    eye = (idx[:, None] == idx[None, :]).astype(F32)
    t_r, t_c = jnp.arange(CH)[:, None], idx[None, :] % CH
    cats, tris = [], []
    ic = jnp.arange(CH)
    for d in range(2):
        ge = (t_r >= t_c) if d == 0 else (t_r <= t_c)
        gt = (t_r > t_c) if d == 0 else (t_r < t_c)
        cats.append(jnp.stack([ge.astype(F32), gt.astype(F32)]))
        tris.append(((ic[:, None] >= ic[None, :]) if d == 0 else (ic[:, None] <= ic[None, :])).astype(BF))
    sel = (idx[:, None] % HD == jnp.arange(HD)[None, :]).astype(BF)
    lvl = [t_r // INV_BASE == t_c // INV_BASE]
    size = INV_BASE
    while size < CH:
        lvl.append((t_r // (2 * size) == t_c // (2 * size)) & (t_r // size != t_c // size))
        size *= 2
    return dict(bm=bm, eye=eye, cat=jnp.stack(cats), tri=jnp.stack(tris), sel=sel,
                lvl=jnp.stack(lvl).astype(F32))


def kernel(x_prompt, x_sample, state_lru, state_rwkv, c, c_ctx, ada_w, ada_b, norm_pre, norm_post, lru_w_in, lru_conv_w, lru_conv_b, lru_gate_w, lru_gate_b, lru_lambda, lru_w_out, rwkv_mu, rwkv_w_r, rwkv_w_k, rwkv_w_v, rwkv_w_g, rwkv_w_o, rwkv_w0, rwkv_w1, rwkv_w2, rwkv_a0, rwkv_a1, rwkv_a2, rwkv_k_k, rwkv_k_a, rwkv_r_k, rwkv_ln_w, rwkv_ln_b, rwkv_v0, rwkv_v1, rwkv_v2):
    bp, tp, d = x_prompt.shape
    bs, ts, _ = x_sample.shape
    depth = ada_w.shape[0]
    n_lru, _, e2 = lru_w_in.shape
    e = e2 // 2
    n_rwkv = rwkv_w_r.shape[0]
    nblk = e // LB
    heads = e // HD
    ng = e // GW

    def merge_lead(w, k):
        return w.reshape((-1,) + w.shape[k:])

    gate_w = jnp.concatenate([lru_gate_w[:, :, 0], lru_gate_w[:, :, 1]], axis=-1)
    gate_b = lru_gate_b.reshape(n_lru, 2, 2, nblk, LB).transpose(0, 1, 3, 2, 4)
    lora_v = rwkv_v1.shape[-1]
    pad_v = (-lora_v) % 128
    W = dict(
        norm_pre=norm_pre.reshape(depth, 1, d), norm_post=norm_post.reshape(depth, 1, d),
        lru_w_in=lru_w_in.astype(BF), lru_w_out=lru_w_out.astype(BF),
        lru=dict(conv_w=lru_conv_w, conv_b=lru_conv_b.reshape(n_lru, 1, e),
                 gate_w=merge_lead(gate_w, 3).astype(BF),
                 gate_b=gate_b.reshape(n_lru * 2 * nblk, 1, 2 * LB),
                 lam=lru_lambda.reshape(n_lru * 2, 1, e)),
        rwkv_mu=rwkv_mu,
        rwkv_w_r=rwkv_w_r.astype(BF), rwkv_w_k=rwkv_w_k.astype(BF), rwkv_w_v=rwkv_w_v.astype(BF),
        rwkv_w_g=rwkv_w_g.astype(BF), rwkv_w_o=rwkv_w_o.astype(BF),
        rwkv_w1=jnp.concatenate([rwkv_w1[:, 0], rwkv_w1[:, 1]], axis=-1).astype(BF),
        rwkv_a1=jnp.concatenate([rwkv_a1[:, 0], rwkv_a1[:, 1]], axis=-1).astype(BF),
        rwkv_lora=dict(w2=merge_lead(rwkv_w2, 2).astype(BF), a2=merge_lead(rwkv_a2, 2).astype(BF),
                       w0=rwkv_w0.reshape(n_rwkv * 2, 1, e), a0=rwkv_a0.reshape(n_rwkv * 2, 1, e)),
        rwkv_v0=rwkv_v0.reshape(-1, 1, e),
        rwkv_v1=jnp.pad(rwkv_v1, ((0, 0), (0, 0), (0, pad_v))).astype(BF),
        rwkv_v2=jnp.pad(rwkv_v2, ((0, 0), (0, pad_v), (0, 0))).astype(BF),
        rwkv_kp=jnp.stack([rwkv_k_k, rwkv_k_a], axis=1),
        rwkv_hp=jnp.stack([rwkv_k_a, rwkv_r_k, rwkv_ln_w, rwkv_ln_b], axis=1),
        consts=_scan_consts(),
    )

    n_cond = 1 + bs
    cond = jnp.zeros((SUBLANES * pl.cdiv(n_cond, SUBLANES), d), F32)
    cond = cond.at[0].set(c_ctx).at[1:n_cond].set(c)
    ada_wb = ada_w.astype(BF)
    mods_p, mods_s = [], []
    for i in range(depth):
        mod = _mm(cond, ada_wb, i, ada_b[i][None], tm=cond.shape[0], tn=256, m_tiles=1,
                  a_pre=_silu, name="adaln")
        mod = mod.reshape(cond.shape[0], 3, d)
        mods_p.append(mod[0:1])
        mods_s.append(mod[1:n_cond])

    zero_lru = [(jnp.zeros((bp, e), F32), jnp.zeros((bp, e), F32))] * n_lru
    y_p, fin_lru, fin_rwkv = _trunk(x_prompt.reshape(bp * tp, d), mods_p, zero_lru, None, W,
                                    batch=bp, seq_len=tp, grid_mode=False, want_state=True)
    new_state_lru = jnp.stack([jnp.stack(f, axis=1) for f in fin_lru], axis=1)
    new_state_rwkv = jnp.stack(
        [jnp.stack([s.reshape(bp, heads, HD, HD) for s in f], axis=1) for f in fin_rwkv], axis=1)

    rep = max(SUBLANES // bs, 1)
    init_lru = [tuple(jnp.tile(state_lru[:, jj, dd], (rep, 1)) for dd in range(2)) for jj in range(n_lru)]
    eye_g = jnp.eye(GH, dtype=F32)
    st = state_rwkv.reshape(bs, n_rwkv, 2, ng, GH, HD, HD)
    z0 = jnp.einsum("bldghvk,hi->bldghkiv", st, eye_g).reshape(bs, n_rwkv, 2, ng, GW, GW)
    init_rwkv = [[z0[:, jj, dd] for dd in range(2)] for jj in range(n_rwkv)]
    y_s, _, _ = _trunk(x_sample.reshape(bs * ts, d), mods_s, init_lru, init_rwkv, W,
                       batch=bs, seq_len=ts, grid_mode=True, want_state=False)
    return (y_p.reshape(bp, tp, d), y_s.reshape(bs, ts, d), new_state_lru, new_state_rwkv)
```

```python
import functools

import jax
import jax.numpy as jnp
from jax import lax
from jax.experimental import pallas as pl
from jax.experimental.pallas import tpu as pltpu

F32 = jnp.float32
BF = jnp.bfloat16

NORM_EPS = 1e-6
GN_EPS = 64e-5
LRU_C = 8.0
GRID_W = 64
CONV_W = 4
CONV_LEFT = 2

HD = 64
GH = 4
GW = GH * HD
CH = 64
INV_BASE = 16
SCAN_GROUPS = 16
LB = 256
SUBLANES = 8
LANES = 128
HEADS_PER_TILE = LANES // HD
VMEM_LIMIT = 56 * 1024 * 1024


def _cparams(*sem):
    return pltpu.CompilerParams(dimension_semantics=sem, vmem_limit_bytes=VMEM_LIMIT)


def _dot(a, b):
    return jnp.dot(a, b, preferred_element_type=F32)


def _dot_nt(a, b):
    return lax.dot_general(a, b, (((1,), (1,)), ((), ())), preferred_element_type=F32)


def _dot_tn(a, b):
    return lax.dot_general(a, b, (((0,), (0,)), ((), ())), preferred_element_type=F32)


def _split3(x):
    h1 = x.astype(BF)
    r1 = x - h1.astype(F32)
    h2 = r1.astype(BF)
    h3 = (r1 - h2.astype(F32)).astype(BF)
    return h1, h2, h3


def _modnorm(x, g, mod):
    y = x * lax.rsqrt(jnp.mean(x * x, axis=-1, keepdims=True) + NORM_EPS) * g
    return y * (1.0 + mod[1:2]) + mod[0:1]


def _mm_body(*refs, act, a_pre, n_extra):
    a_ref, w_ref, b_ref = refs[:3]
    extras = refs[3:3 + n_extra]
    o_ref = refs[3 + n_extra]
    a = a_ref[...]
    if a_pre is not None:
        a = a_pre(a.astype(F32))
    acc = _dot(a.astype(BF), w_ref[...]) + b_ref[...]
    if act is not None:
        acc = act(acc, *[e[...] for e in extras])
    o_ref[...] = acc.astype(o_ref.dtype)


def _mm(a, w3, li, bias, *, tm, tn, m_tiles, a_imap=None, o_shape=None, o_imap=None,
        extras=(), act=None, a_pre=None, out_dtype=F32, name="mm"):
    _, K, N = w3.shape
    tn = min(tn, N)
    assert N % tn == 0 and K % 128 == 0
    n_tiles = N // tn
    if a_imap is None:
        a_imap = lambda n, m: (m, 0)
    if o_imap is None:
        o_imap = lambda n, m: (m, n)
    if o_shape is None:
        o_shape = (m_tiles * tm, N)
    if bias is None:
        bias = jnp.zeros((1, N), F32)
    in_specs = [
        pl.BlockSpec((tm, K), a_imap),
        pl.BlockSpec((None, K, tn), lambda n, m: (li, 0, n)),
        pl.BlockSpec((1, tn), lambda n, m: (0, n)),
    ] + [pl.BlockSpec((tm, tn), lambda n, m: (m, n)) for _ in extras]
    return pl.pallas_call(
        functools.partial(_mm_body, act=act, a_pre=a_pre, n_extra=len(extras)),
        out_shape=jax.ShapeDtypeStruct(o_shape, out_dtype),
        grid=(n_tiles, m_tiles),
        in_specs=in_specs,
        out_specs=pl.BlockSpec((tm, tn), o_imap),
        compiler_params=_cparams("parallel", "parallel"),
        name=name,
    )(a, w3, bias, *extras)


def _prenorm_body(x_ref, g_ref, mod_ref, o_ref):
    o_ref[...] = _modnorm(x_ref[...], g_ref[...], mod_ref[0]).astype(o_ref.dtype)


def _prenorm(x, g3, layer, mod, *, tm, rows_per_cond):
    rows, d = x.shape
    return pl.pallas_call(
        _prenorm_body,
        out_shape=jax.ShapeDtypeStruct((rows, d), BF),
        grid=(rows // tm,),
        in_specs=[
            pl.BlockSpec((tm, d), lambda i: (i, 0)),
            pl.BlockSpec((None, 1, d), lambda i: (layer, 0, 0)),
            pl.BlockSpec((1, 3, d), lambda i: ((i * tm) // rows_per_cond, 0, 0)),
        ],
        out_specs=pl.BlockSpec((tm, d), lambda i: (i, 0)),
        compiler_params=_cparams("parallel"),
        name="prenorm",
    )(x, g3, mod)


def _postnorm_body(x_ref, m_ref, g_ref, mod_ref, o_ref):
    m = m_ref[...]
    y = m * lax.rsqrt(jnp.mean(m * m, axis=-1, keepdims=True) + NORM_EPS) * g_ref[...]
    o_ref[...] = x_ref[...] + mod_ref[0][2:3] * y


def _postnorm(x, m, g3, layer, mod, *, tm, rows_per_cond):
    rows, d = x.shape
    return pl.pallas_call(
        _postnorm_body,
        out_shape=jax.ShapeDtypeStruct((rows, d), F32),
        grid=(rows // tm,),
        in_specs=[
            pl.BlockSpec((tm, d), lambda i: (i, 0)),
            pl.BlockSpec((tm, d), lambda i: (i, 0)),
            pl.BlockSpec((None, 1, d), lambda i: (layer, 0, 0)),
            pl.BlockSpec((1, 3, d), lambda i: ((i * tm) // rows_per_cond, 0, 0)),
        ],
        out_specs=pl.BlockSpec((tm, d), lambda i: (i, 0)),
        compiler_params=_cparams("parallel"),
        name="postnorm",
    )(x, m, g3, mod)


def _write_mixes(h, shifted, mu_ref, out_refs):
    xx = shifted - h
    out_refs[0][...] = h.astype(BF)
    for m in range(5):
        out_refs[m + 1][...] = (h + xx * mu_ref[m:m + 1]).astype(BF)


def _shiftmix_seq_body(x_ref, g_ref, mod_ref, mu_ref, *out_refs):
    h = _modnorm(x_ref[...], g_ref[...], mod_ref[0])
    t_len, d = h.shape
    half = d // 2
    t = lax.broadcasted_iota(jnp.int32, (t_len, 1), 0)
    prev = jnp.where(t == 0, 0.0, pltpu.roll(h[:, :half], 1, 0))
    nxt = jnp.where(t == t_len - 1, 0.0, pltpu.roll(h[:, half:], t_len - 1, 0))
    _write_mixes(h, jnp.concatenate([prev, nxt], axis=1), mu_ref, out_refs)


def _shiftmix_grid_body(x_ref, up_ref, dn_ref, g_ref, mod_ref, mu_ref, *out_refs, tiles_per_batch):
    ti = pl.program_id(0) % tiles_per_batch
    g = g_ref[...]
    mod = mod_ref[0]
    h = _modnorm(x_ref[...], g, mod)
    tr, d = h.shape
    q = d // 4
    h_up = jnp.where(ti > 0, _modnorm(up_ref[...], g, mod)[:, 2 * q:3 * q], 0.0)
    h_dn = jnp.where(ti < tiles_per_batch - 1, _modnorm(dn_ref[...], g, mod)[:, 3 * q:], 0.0)
    col = lax.broadcasted_iota(jnp.int32, (tr, 1), 0) % GRID_W
    left = jnp.where(col == 0, 0.0, pltpu.roll(h[:, :q], 1, 0))
    right = jnp.where(col == GRID_W - 1, 0.0, pltpu.roll(h[:, q:2 * q], tr - 1, 0))
    up = jnp.concatenate([h_up, h[:tr - GRID_W, 2 * q:3 * q]], axis=0)
    down = jnp.concatenate([h[GRID_W:, 3 * q:], h_dn], axis=0)
    _write_mixes(h, jnp.concatenate([left, right, up, down], axis=1), mu_ref, out_refs)


def _shiftmix(x, g3, layer, mod, mu, *, seq_len, cond_rows, grid_mode, tm):
    rows, d = x.shape
    outs = [jax.ShapeDtypeStruct((rows, d), BF)] * 6
    common = [
        pl.BlockSpec((None, 1, d), lambda i: (layer, 0, 0)),
        pl.BlockSpec((1, 3, d), lambda i: ((i * tm) // cond_rows, 0, 0)),
        pl.BlockSpec((5, d), lambda i: (0, 0)),
    ]
    out_specs = [pl.BlockSpec((tm, d), lambda i: (i, 0))] * 6
    if not grid_mode:
        assert tm == seq_len
        return pl.pallas_call(
            _shiftmix_seq_body, out_shape=outs, grid=(rows // tm,),
            in_specs=[pl.BlockSpec((tm, d), lambda i: (i, 0))] + common,
            out_specs=out_specs, compiler_params=_cparams("parallel"), name="shiftmix_seq",
        )(x, g3, mod, mu)
    per = tm // GRID_W
    n_halo = rows // GRID_W
    return pl.pallas_call(
        functools.partial(_shiftmix_grid_body, tiles_per_batch=seq_len // tm),
        out_shape=outs, grid=(rows // tm,),
        in_specs=[
            pl.BlockSpec((tm, d), lambda i: (i, 0)),
            pl.BlockSpec((GRID_W, d), lambda i: (jnp.maximum(i * per - 1, 0), 0)),
            pl.BlockSpec((GRID_W, d), lambda i: (jnp.minimum((i + 1) * per, n_halo - 1), 0)),
        ] + common,
        out_specs=out_specs, compiler_params=_cparams("parallel"), name="shiftmix_grid",
    )(x, x, x, g3, mod, mu)


def _lru_body(*refs, rb, reverse, combine, n_tiles):
    if combine:
        (x_ref, xp_ref, xn_ref, cw_ref, cb_ref, gw_ref, gb_ref, lam_ref, h0_ref, yp_ref, z_ref,
         y_ref, hf_ref, xs, a_s, u_s, hc) = refs
    else:
        (x_ref, xp_ref, xn_ref, cw_ref, cb_ref, gw_ref, gb_ref, lam_ref, h0_ref,
         y_ref, hf_ref, xs, a_s, u_s, hc) = refs
    tr = x_ref.shape[0]
    hb = xp_ref.shape[0]
    step = pl.program_id(1)
    t_idx = (n_tiles - 1 - step) if reverse else step

    @pl.when(step == 0)
    def _():
        hc[...] = h0_ref[...]

    xs[hb:hb + tr, :] = x_ref[...].astype(F32)
    xs[0:hb, :] = jnp.where(t_idx > 0, xp_ref[...].astype(F32), 0.0)
    xs[hb + tr:hb + tr + hb, :] = jnp.where(t_idx < n_tiles - 1, xn_ref[...].astype(F32), 0.0)
    xc = cb_ref[...]
    for tap in range(CONV_W):
        off = hb + (tap - CONV_LEFT) * rb
        xc = xc + xs[off:off + tr, :] * cw_ref[tap:tap + 1]
    g = _dot(xc.astype(BF), gw_ref[...]) + gb_ref[...]
    sg = 0.5 * jnp.tanh(0.5 * g) + 0.5
    r = sg[:, :LB]
    i = sg[:, LB:]
    log_a = r * (-LRU_C * jax.nn.softplus(-lam_ref[...]))
    a_s[...] = jnp.exp(log_a)
    th = jnp.tanh(log_a)
    u_s[...] = jnp.sqrt(-2.0 * th / (1.0 - th)) * (i * xc)

    if rb % SUBLANES == 0:
        n_steps = tr // rb

        def body(s, h):
            row = pl.multiple_of(((n_steps - 1 - s) if reverse else s) * rb, rb)
            h = a_s[pl.ds(row, rb), :] * h + u_s[pl.ds(row, rb), :]
            u_s[pl.ds(row, rb), :] = h
            return h

        h = lax.fori_loop(0, n_steps, body, hc[...], unroll=4)
    else:
        assert 2 * rb == SUBLANES
        n_steps = tr // SUBLANES
        first = lax.broadcasted_iota(jnp.int32, (SUBLANES, 1), 0) < rb

        def body(s, h):
            row = pl.multiple_of(((n_steps - 1 - s) if reverse else s) * SUBLANES, SUBLANES)
            a = a_s[pl.ds(row, SUBLANES), :]
            u = u_s[pl.ds(row, SUBLANES), :]
            h1 = a * h + u
            h2 = a * pltpu.roll(h1, rb, 0) + u
            u_s[pl.ds(row, SUBLANES), :] = jnp.where(first != reverse, h1, h2)
            return pltpu.roll(h2, rb, 0)

        h = lax.fori_loop(0, n_steps, body, hc[...], unroll=4)
    hc[...] = h
    hf_ref[...] = h
    if combine:
        z = z_ref[...].astype(F32)
        y_ref[...] = ((yp_ref[...] + u_s[...]) * (z * jax.nn.sigmoid(z))).astype(y_ref.dtype)
    else:
        y_ref[...] = u_s[...]


def _lru_core(xz, lru, j, d, h0, rb, *, tr, y_prev=None):
    rows, e2 = xz.shape
    e = e2 // 2
    nblk = e // LB
    n_tiles = rows // tr
    hb = max(2 * rb, 2 * SUBLANES)
    per = tr // hb
    n_halo = rows // hb
    hr = h0.shape[0]
    reverse = d == 1
    combine = y_prev is not None

    def tidx(s):
        return (n_tiles - 1 - s) if reverse else s

    in_specs = [
        pl.BlockSpec((tr, LB), lambda c, s: (tidx(s), c)),
        pl.BlockSpec((hb, LB), lambda c, s: (jnp.maximum(tidx(s) * per - 1, 0), c)),
        pl.BlockSpec((hb, LB), lambda c, s: (jnp.minimum((tidx(s) + 1) * per, n_halo - 1), c)),
        pl.BlockSpec((None, CONV_W, LB), lambda c, s: (j, 0, c)),
        pl.BlockSpec((None, 1, LB), lambda c, s: (j, 0, c)),
        pl.BlockSpec((None, LB, 2 * LB), lambda c, s: ((j * 2 + d) * nblk + c, 0, 0)),
        pl.BlockSpec((None, 1, 2 * LB), lambda c, s: ((j * 2 + d) * nblk + c, 0, 0)),
        pl.BlockSpec((None, 1, LB), lambda c, s: (j * 2 + d, 0, c)),
        pl.BlockSpec((hr, LB), lambda c, s: (0, c)),
    ]
    args = [xz, xz, xz, lru["conv_w"], lru["conv_b"], lru["gate_w"], lru["gate_b"], lru["lam"], h0]
    if combine:
        in_specs += [
            pl.BlockSpec((tr, LB), lambda c, s: (tidx(s), c)),
            pl.BlockSpec((tr, LB), lambda c, s: (tidx(s), nblk + c)),
        ]
        args += [y_prev, xz]
    return pl.pallas_call(
        functools.partial(_lru_body, rb=rb, reverse=reverse, combine=combine, n_tiles=n_tiles),
        out_shape=[jax.ShapeDtypeStruct((rows, e), BF if combine else F32),
                   jax.ShapeDtypeStruct((hr, e), F32)],
        grid=(nblk, n_tiles),
        in_specs=in_specs,
        out_specs=[pl.BlockSpec((tr, LB), lambda c, s: (tidx(s), c)),
                   pl.BlockSpec((hr, LB), lambda c, s: (0, c))],
        scratch_shapes=[pltpu.VMEM((tr + 2 * hb, LB), F32), pltpu.VMEM((tr, LB), F32),
                        pltpu.VMEM((tr, LB), F32), pltpu.VMEM((hr, LB), F32)],
        compiler_params=_cparams("parallel", "arbitrary"),
        name="lru_core",
    )(*args)


def _rwkv_scan_body(*refs, reverse, zero_init, want_state, finish, n_chunks):
    it = iter(refs)
    (r_ref, k_ref, v_ref, tw_ref, ta_ref, w2_ref, w0_ref, a2_ref, a0_ref, kp_ref, bm_ref,
     tri_ref, cat_ref, lvl_ref) = (next(it) for _ in range(14))
    if finish:
        tao_ref, a2o_ref, a0o_ref, yo_ref, zg_ref, hp_ref = (next(it) for _ in range(6))
    sel_ref = next(it) if want_state else None
    z0_ref = None if zero_init else next(it)
    y_ref = next(it)
    sf_ref = next(it) if want_state else None
    z_s = next(it)
    c = pl.program_id(2)

    @pl.when(c == 0)
    def _():
        z_s[...] = jnp.zeros_like(z_s) if zero_init else z0_ref[...]

    lw_all = _act_logdecay(_dot(tw_ref[...], w2_ref[...]) + w0_ref[...])
    as_all = jax.nn.sigmoid(_dot(ta_ref[...], a2_ref[...]) + a0_ref[...])

    groups = range(z_s.shape[0])
    lanes = [slice(g * GW, (g + 1) * GW) for g in groups]
    bm = bm_ref[...]
    rs = [r_ref[:, l] for l in lanes]
    ks = [k_ref[:, l] for l in lanes]
    vs = [v_ref[:, l] for l in lanes]
    ys, z_new = _rwkv_chunk(
        rs, ks, vs, [lw_all[:, l] for l in lanes], [as_all[:, l] for l in lanes],
        [kp_ref[:, l] for l in lanes], bm, tri_ref[...], cat_ref, lvl_ref,
        [[z_s[g, t] for t in range(z_s.shape[1])] for g in groups], reverse=reverse)
    for g in groups:
        for t in range(z_s.shape[1]):
            z_s[g, t] = z_new[g][t]
    if not finish:
        for g in groups:
            y_ref[:, lanes[g]] = ys[g]
    else:
        bm_b = bm.astype(BF)
        aso_all = jax.nn.sigmoid(_dot(tao_ref[...], a2o_ref[...]) + a0o_ref[...])

        def headsum(x):
            return sum(_dot(p, bm_b) for p in _split3(x)[:2])

        ysum = [ys[g] + yo_ref[:, lanes[g]] for g in groups]
        arg = [rs[g] * ks[g] * hp_ref[1:2, lanes[g]]
               * (2.0 + hp_ref[0:1, lanes[g]] * (as_all[:, lanes[g]] + aso_all[:, lanes[g]] - 2.0))
               for g in groups]
        sums = [headsum(jnp.concatenate([ysum[g], arg[g]], axis=0)) for g in groups]
        yc = [ysum[g] - sums[g][:CH] * (1.0 / HD) for g in groups]
        var = [headsum(yc[g] * yc[g]) * (1.0 / HD) for g in groups]
        for g in groups:
            out = (yc[g] * lax.rsqrt(var[g] + GN_EPS) * hp_ref[2:3, lanes[g]] + hp_ref[3:4, lanes[g]]
                   + sums[g][CH:] * vs[g])
            zg = zg_ref[:, lanes[g]]
            y_ref[:, lanes[g]] = (out * (zg * jax.nn.sigmoid(zg))).astype(y_ref.dtype)
    if want_state:
        @pl.when(c == n_chunks - 1)
        def _():
            sel = sel_ref[...]
            for g in groups:
                for t in range(z_s.shape[1]):
                    sf_ref[g * GW + t * LANES:g * GW + (t + 1) * LANES, :] = sum(
                        _dot_tn(p, sel) for p in _split3(z_new[g][t])[:2])


def _vmap(fn, *lists):
    return [fn(*args) for args in zip(*lists)]


def _rwkv_chunk(r, k, v, lw, asg, kp, bm, tri, cat_ref, lvl_ref, z, *, reverse):
    incl = cat_ref[0]
    strict = cat_ref[1]
    eye_c = incl - strict
    bm_b = bm.astype(BF)

    def headsum(x):
        return sum(_dot(p, bm_b) for p in _split3(x)[:2])

    lane_head = lax.broadcasted_iota(jnp.int32, (1, LANES), 1) // HD
    zero_tile = jnp.zeros((CH, LANES), BF)

    def bdm(x):
        xb = x.astype(BF)
        blocks = []
        for h in range(GH):
            j = h // HEADS_PER_TILE
            tile = xb[:, j * LANES:(j + 1) * LANES]
            keep = jnp.where(lane_head == h % HEADS_PER_TILE, tile, jnp.zeros_like(tile))
            blocks.append(jnp.concatenate([keep if t == j else zero_tile for t in range(GW // LANES)],
                                          axis=1))
        return jnp.concatenate(blocks, axis=0)

    def mm(a, b_bd):
        return _vmap(lambda x, y: _dot(x.astype(BF), y), a, b_bd)

    def stack2(a, b):
        return _vmap(lambda x, y: jnp.concatenate([x, y], axis=0), a, b)

    def top(a):
        return [x[:CH] for x in a]

    def bot(a):
        return [x[CH:] for x in a]

    def masked(a, keep):
        return [jnp.where(keep, x, 0.0) for x in a]

    kk = _vmap(lambda k_, p: k_ * p[0:1], k, kp)
    ss = _vmap(lambda x: headsum(x * x), kk)
    kk = _vmap(lambda x, s: x * lax.rsqrt(jnp.maximum(s, 1e-24)), kk, ss)
    kd = _vmap(lambda k_, a, p: k_ * (1.0 + (a - 1.0) * p[1:2]), k, asg, kp)
    bvec = _vmap(lambda x, a: x * a, kk, asg)

    cum = _vmap(lambda x: sum(_dot(tri, p) for p in _split3(x)[:2]), lw)
    tot = _vmap(lambda x: x[0:1] if reverse else x[CH - 1:CH], cum)
    e_pos = _vmap(jnp.exp, cum)
    e_neg = _vmap(lambda x: jnp.exp(-x), cum)
    e_rem = _vmap(lambda t, en: jnp.exp(t) * en, tot, e_neg)
    a_t = _vmap(lambda x, ep, l: -x * (ep * jnp.exp(-l)), kk, e_pos, lw)
    r_t = _vmap(lambda x, ep: x * ep, r, e_pos)
    ar = [x.astype(BF) for x in stack2(a_t, r_t)]
    bt_bd = _vmap(lambda x, en: bdm(x * en), bvec, e_neg)
    kt_bd = _vmap(lambda x, en: bdm(x * en), kd, e_neg)
    ab = _vmap(_dot_nt, ar, bt_bd)
    ak = _vmap(_dot_nt, ar, kt_bd)
    keep_strict, keep_incl = strict > 0, incl > 0
    a_ab, a_rb = masked(top(ab), keep_strict), masked(bot(ab), keep_incl)
    a_ak, a_rk = masked(top(ak), keep_strict), masked(bot(ak), keep_incl)

    p = [n * lvl_ref[0] for n in a_ab]
    x = [eye_c + n for n in p]
    pw = mm(p, _vmap(bdm, p))
    e = 2
    while e < INV_BASE:
        rhs = _vmap(bdm, pw)
        if 2 * e < INV_BASE:
            both = mm(stack2(x, pw), rhs)
            x = _vmap(jnp.add, x, top(both))
            pw = bot(both)
        else:
            x = _vmap(jnp.add, x, mm(x, rhs))
        e *= 2
    for lvl in range(1, lvl_ref.shape[0]):
        off = [n * lvl_ref[lvl] for n in a_ab]
        x = _vmap(jnp.add, x, mm(x, _vmap(bdm, mm(off, _vmap(bdm, x)))))

    s_v = mm(stack2(a_ak, a_rk), _vmap(bdm, v))
    def z_rhs(zq):
        n_t = len(zq)
        zero = jnp.zeros((LANES, LANES), BF)
        return jnp.concatenate(
            [jnp.concatenate([zq[i].astype(BF) if i == t else zero for t in range(n_t)], axis=1)
             for i in range(n_t)], axis=0)

    s_z = _vmap(lambda a, zq: _dot(a, z_rhs(zq)), ar, z)
    u = mm(x, _vmap(lambda a, b_: bdm(a + b_), top(s_z), top(s_v)))
    y = _vmap(lambda a, b_, c_: a + b_ + c_, bot(s_z), bot(s_v), mm(a_rb, _vmap(bdm, u)))
    row_id = lax.broadcasted_iota(jnp.int32, (LANES, 1), 0)
    lane_id = lax.broadcasted_iota(jnp.int32, (1, LANES), 1)
    same_head = (row_id // HD) == (lane_id // HD)
    diag = row_id == lane_id
    lhs = _vmap(lambda b_, k_, er: jnp.concatenate([b_ * er, k_ * er], axis=0).astype(BF), bvec, kd, e_rem)
    rhs = _vmap(lambda u_, v_: jnp.concatenate([u_, v_], axis=0).astype(BF), u, v)
    decay = _vmap(jnp.exp, tot)
    z_new = [[] for _ in z]
    for t in range(GW // LANES):
        cols = slice(t * LANES, (t + 1) * LANES)
        for g in range(len(z)):
            upd = jnp.where(same_head, _dot_tn(lhs[g][:, cols], rhs[g][:, cols]), 0.0)
            g_col = jnp.sum(jnp.where(diag, decay[g][:, cols], 0.0), axis=1, keepdims=True)
            z_new[g].append(z[g][t] * g_col + upd)
    return y, z_new


def _rwkv_scan(r, k, v, tw, ta, lora, jd, kp, consts, d, *, batch, seq_len, z0=None, want_state=False,
               finish=None):
    rows, e = r.shape
    ngs = min(SCAN_GROUPS, e // GW)
    sw = ngs * GW
    ng = e // sw
    nc = seq_len // CH
    reverse = d == 1
    zero_init = z0 is None
    rank = lora["w2"].shape[1]

    def cidx(c):
        return (nc - 1 - c) if reverse else c

    def lora_specs(dd, jdd):
        return [pl.BlockSpec((CH, rank), lambda b, g, c: (b * nc + cidx(c), dd)),
                pl.BlockSpec((None, rank, sw), lambda b, g, c: (jdd, 0, g)),
                pl.BlockSpec((None, 1, sw), lambda b, g, c: (jdd, 0, g))]

    tok = pl.BlockSpec((CH, sw), lambda b, g, c: (b * nc + cidx(c), g))
    w_specs, a_specs = lora_specs(d, jd), lora_specs(d, jd)
    in_specs = [tok, tok, tok, w_specs[0], a_specs[0], w_specs[1], w_specs[2], a_specs[1], a_specs[2],
                pl.BlockSpec((2, sw), lambda b, g, c: (0, g)),
                pl.BlockSpec((GW, GW), lambda b, g, c: (0, 0)),
                pl.BlockSpec((None, CH, CH), lambda b, g, c: (d, 0, 0)),
                pl.BlockSpec((None, 2, CH, GW), lambda b, g, c: (d, 0, 0, 0)),
                pl.BlockSpec(consts["lvl"].shape, lambda b, g, c: (0, 0, 0))]
    args = [r, k, v, tw, ta, lora["w2"], lora["w0"], lora["a2"], lora["a0"], kp, consts["bm"],
            consts["tri"], consts["cat"], consts["lvl"]]
    if finish is not None:
        y_other, zg, hp = finish
        in_specs += lora_specs(1 - d, jd + 1 - 2 * d) + [tok, tok, pl.BlockSpec((4, sw), lambda b, g, c: (0, g))]
        args += [ta, lora["a2"], lora["a0"], y_other, zg, hp]
    if want_state:
        in_specs.append(pl.BlockSpec((LANES, HD), lambda b, g, c: (0, 0)))
        args.append(consts["sel"])
    if not zero_init:
        in_specs.append(pl.BlockSpec((None, ngs, GW // LANES, LANES, LANES), lambda b, g, c: (b, g, 0, 0, 0)))
        args.append(z0)
    out_shape = [jax.ShapeDtypeStruct((rows, e), F32 if finish is None else BF)]
    out_specs = [tok]
    if want_state:
        out_shape.append(jax.ShapeDtypeStruct((batch, e, HD), F32))
        out_specs.append(pl.BlockSpec((None, sw, HD), lambda b, g, c: (b, g, 0)))
    res = pl.pallas_call(
        functools.partial(_rwkv_scan_body, reverse=reverse, zero_init=zero_init,
                          want_state=want_state, finish=finish is not None, n_chunks=nc),
        out_shape=out_shape,
        grid=(batch, ng, nc),
        in_specs=in_specs,
        out_specs=out_specs,
        scratch_shapes=[pltpu.VMEM((ngs, GW // LANES, LANES, LANES), F32)],
        compiler_params=_cparams("parallel", "parallel", "arbitrary"),
        name="rwkv_scan",
    )(*args)
    return res if want_state else (res[0], None)


def _silu(x):
    return x * jax.nn.sigmoid(x)


def _act_tanh(x):
    return jnp.tanh(x)


def _act_sigmoid(x):
    return jax.nn.sigmoid(x)


def _act_logdecay(x):
    return -jnp.exp(-0.5) * jax.nn.sigmoid(x)


def _act_vmix(acc, v_first, mix):
    return acc + (v_first - acc) * mix


def _trunk(x, mods, init_lru, init_rwkv, W, *, batch, seq_len, grid_mode, want_state):
    rows, d = x.shape
    e = W["lru_w_out"].shape[1]
    depth = W["norm_pre"].shape[0]
    tm_row = min(256, seq_len)
    tm = min(1024, seq_len)
    nb = seq_len // tm
    m_tiles = rows // tm
    rb = batch
    lru_tr = max(2048 // rb, 1) * rb if rb % SUBLANES == 0 else 2048
    lru_tr = min(lru_tr, rows)
    cond_rows = rows // mods[0].shape[0]
    tn_in = min(1024, 2 * e)
    fin_lru, fin_rwkv = [], []
    v_first = None

    for i in range(depth):
        mod = mods[i]
        j = i // 2
        if i % 2 == 0:
            h = _prenorm(x, W["norm_pre"], i, mod, tm=tm_row, rows_per_cond=cond_rows)
            xz = _mm(h, W["lru_w_in"], j, None, tm=tm, tn=tn_in, m_tiles=m_tiles,
                     o_shape=(seq_len, batch * 2 * e),
                     o_imap=lambda n, m: (m % nb, (m // nb) * (2 * e // tn_in) + n), out_dtype=BF,
                     name="lru_in")
            xz = xz.reshape(seq_len * batch, 2 * e)
            y0, hf0 = _lru_core(xz, W["lru"], j, 0, init_lru[j][0], rb, tr=lru_tr)
            y, hf1 = _lru_core(xz, W["lru"], j, 1, init_lru[j][1], rb, tr=lru_tr, y_prev=y0)
            fin_lru.append((hf0, hf1))
            y = y.reshape(seq_len, batch * e)
            m = _mm(y, W["lru_w_out"], j, None, tm=tm, tn=1024, m_tiles=m_tiles,
                    a_imap=lambda n, m: (m % nb, m // nb), name="lru_out")
        else:
            h, xr, xw, xk, xv, xa = _shiftmix(
                x, W["norm_pre"], i, mod, W["rwkv_mu"][j], seq_len=seq_len, cond_rows=cond_rows, grid_mode=grid_mode,
                tm=seq_len if not grid_mode else min(512, seq_len))
            tm_plain = min(1024, rows)
            mk = dict(tm=tm_plain, m_tiles=rows // tm_plain)
            r = _mm(xr, W["rwkv_w_r"], j, None, tn=1024, name="rwkv_r", **mk)
            k = _mm(xk, W["rwkv_w_k"], j, None, tn=1024, name="rwkv_k", **mk)
            z = _mm(h, W["rwkv_w_g"], j, None, tn=1024, name="rwkv_g", **mk)
            if j == 0:
                v = _mm(xv, W["rwkv_w_v"], j, None, tn=1024, name="rwkv_v", **mk)
                v_first = v
            else:
                tv = _mm(xv, W["rwkv_v1"], j - 1, None, tn=128, out_dtype=BF, name="rwkv_v1", **mk)
                mix = _mm(tv, W["rwkv_v2"], j - 1, W["rwkv_v0"][j - 1], tn=1024, act=_act_sigmoid,
                          name="rwkv_v2", **mk)
                v = _mm(xv, W["rwkv_w_v"], j, None, tn=1024, extras=(v_first, mix), act=_act_vmix,
                        name="rwkv_v", **mk)
            tw = _mm(xw, W["rwkv_w1"], j, None, tn=256, act=_act_tanh, out_dtype=BF, name="rwkv_w1", **mk)
            ta = _mm(xa, W["rwkv_a1"], j, None, tn=256, out_dtype=BF, name="rwkv_a1", **mk)
            fins = []
            y_d = None
            for dd in range(2):
                z0 = None if init_rwkv is None else init_rwkv[j][dd]
                finish = None if dd == 0 else (y_d, z, W["rwkv_hp"][j])
                y_d, s_d = _rwkv_scan(r, k, v, tw, ta, W["rwkv_lora"], j * 2 + dd, W["rwkv_kp"][j],
                                      W["consts"], dd, batch=batch, seq_len=seq_len, z0=z0,
                                      want_state=want_state, finish=finish)
                fins.append(s_d)
            fin_rwkv.append(fins)
            m = _mm(y_d, W["rwkv_w_o"], j, None, tn=1024, name="rwkv_o", **mk)
        x = _postnorm(x, m, W["norm_post"], i, mod, tm=tm_row, rows_per_cond=cond_rows)
    return x, fin_lru, fin_rwkv


def _scan_consts():
    idx = jnp.arange(GW)
    bm = ((idx[:, None] // HD) == (idx[None, :] // HD)).astype(F32)
    t_r, t_c = jnp.arange(CH)[:, None], idx[None, :] % CH
    cats, tris = [], []
    ic = jnp.arange(CH)
    for d in range(2):
        ge = (t_r >= t_c) if d == 0 else (t_r <= t_c)
        gt = (t_r > t_c) if d == 0 else (t_r < t_c)
        cats.append(jnp.stack([ge.astype(F32), gt.astype(F32)]))
        tris.append(((ic[:, None] >= ic[None, :]) if d == 0 else (ic[:, None] <= ic[None, :])).astype(BF))
    sel = (idx[:, None] % HD == jnp.arange(HD)[None, :]).astype(BF)
    lvl = [t_r // INV_BASE == t_c // INV_BASE]
    size = INV_BASE
    while size < CH:
        lvl.append((t_r // (2 * size) == t_c // (2 * size)) & (t_r // size != t_c // size))
        size *= 2
    return dict(bm=bm, cat=jnp.stack(cats), tri=jnp.stack(tris), sel=sel,
                lvl=jnp.stack(lvl).astype(F32))


def kernel(x_prompt, x_sample, state_lru, state_rwkv, c, c_ctx, ada_w, ada_b, norm_pre, norm_post, lru_w_in, lru_conv_w, lru_conv_b, lru_gate_w, lru_gate_b, lru_lambda, lru_w_out, rwkv_mu, rwkv_w_r, rwkv_w_k, rwkv_w_v, rwkv_w_g, rwkv_w_o, rwkv_w0, rwkv_w1, rwkv_w2, rwkv_a0, rwkv_a1, rwkv_a2, rwkv_k_k, rwkv_k_a, rwkv_r_k, rwkv_ln_w, rwkv_ln_b, rwkv_v0, rwkv_v1, rwkv_v2):
    bp, tp, d = x_prompt.shape
    bs, ts, _ = x_sample.shape
    depth = ada_w.shape[0]
    n_lru, _, e2 = lru_w_in.shape
    e = e2 // 2
    n_rwkv = rwkv_w_r.shape[0]
    nblk = e // LB
    heads = e // HD
    ng = e // GW

    def merge_lead(w, k):
        return w.reshape((-1,) + w.shape[k:])

    gate_w = jnp.concatenate([lru_gate_w[:, :, 0], lru_gate_w[:, :, 1]], axis=-1)
    gate_b = lru_gate_b.reshape(n_lru, 2, 2, nblk, LB).transpose(0, 1, 3, 2, 4)
    lora_v = rwkv_v1.shape[-1]
    pad_v = (-lora_v) % 128
    W = dict(
        norm_pre=norm_pre.reshape(depth, 1, d), norm_post=norm_post.reshape(depth, 1, d),
        lru_w_in=lru_w_in.astype(BF), lru_w_out=lru_w_out.astype(BF),
        lru=dict(conv_w=lru_conv_w, conv_b=lru_conv_b.reshape(n_lru, 1, e),
                 gate_w=merge_lead(gate_w, 3).astype(BF),
                 gate_b=gate_b.reshape(n_lru * 2 * nblk, 1, 2 * LB),
                 lam=lru_lambda.reshape(n_lru * 2, 1, e)),
        rwkv_mu=rwkv_mu,
        rwkv_w_r=rwkv_w_r.astype(BF), rwkv_w_k=rwkv_w_k.astype(BF), rwkv_w_v=rwkv_w_v.astype(BF),
        rwkv_w_g=rwkv_w_g.astype(BF), rwkv_w_o=rwkv_w_o.astype(BF),
        rwkv_w1=jnp.concatenate([rwkv_w1[:, 0], rwkv_w1[:, 1]], axis=-1).astype(BF),
        rwkv_a1=jnp.concatenate([rwkv_a1[:, 0], rwkv_a1[:, 1]], axis=-1).astype(BF),
        rwkv_lora=dict(w2=merge_lead(rwkv_w2, 2).astype(BF), a2=merge_lead(rwkv_a2, 2).astype(BF),
                       w0=rwkv_w0.reshape(n_rwkv * 2, 1, e), a0=rwkv_a0.reshape(n_rwkv * 2, 1, e)),
        rwkv_v0=rwkv_v0.reshape(-1, 1, e),
        rwkv_v1=jnp.pad(rwkv_v1, ((0, 0), (0, 0), (0, pad_v))).astype(BF),
        rwkv_v2=jnp.pad(rwkv_v2, ((0, 0), (0, pad_v), (0, 0))).astype(BF),
        rwkv_kp=jnp.stack([rwkv_k_k, rwkv_k_a], axis=1),
        rwkv_hp=jnp.stack([rwkv_k_a, rwkv_r_k, rwkv_ln_w, rwkv_ln_b], axis=1),
        consts=_scan_consts(),
    )

    n_cond = 1 + bs
    cond = jnp.zeros((SUBLANES * pl.cdiv(n_cond, SUBLANES), d), F32)
    cond = cond.at[0].set(c_ctx).at[1:n_cond].set(c)
    ada_wb = ada_w.astype(BF)
    mods_p, mods_s = [], []
    for i in range(depth):
        mod = _mm(cond, ada_wb, i, ada_b[i][None], tm=cond.shape[0], tn=256, m_tiles=1,
                  a_pre=_silu, name="adaln")
        mod = mod.reshape(cond.shape[0], 3, d)
        mods_p.append(mod[0:1])
        mods_s.append(mod[1:n_cond])

    zero_lru = [(jnp.zeros((bp, e), F32), jnp.zeros((bp, e), F32))] * n_lru
    y_p, fin_lru, fin_rwkv = _trunk(x_prompt.reshape(bp * tp, d), mods_p, zero_lru, None, W,
                                    batch=bp, seq_len=tp, grid_mode=False, want_state=True)
    new_state_lru = jnp.stack([jnp.stack(f, axis=1) for f in fin_lru], axis=1)
    new_state_rwkv = jnp.stack(
        [jnp.stack([s.reshape(bp, heads, HD, HD) for s in f], axis=1) for f in fin_rwkv], axis=1)

    rep = max(SUBLANES // bs, 1)
    init_lru = [tuple(jnp.tile(state_lru[:, jj, dd], (rep, 1)) for dd in range(2)) for jj in range(n_lru)]
    eye_t = jnp.eye(HEADS_PER_TILE, dtype=F32)
    n_tiles = e // LANES
    st = state_rwkv.reshape(bs, n_rwkv, 2, n_tiles, HEADS_PER_TILE, HD, HD)
    z0 = jnp.einsum("bldthvk,hi->bldthkiv", st, eye_t).reshape(
        bs, n_rwkv, 2, ng, GW // LANES, LANES, LANES)
    init_rwkv = [[z0[:, jj, dd] for dd in range(2)] for jj in range(n_rwkv)]
    y_s, _, _ = _trunk(x_sample.reshape(bs * ts, d), mods_s, init_lru, init_rwkv, W,
                       batch=bs, seq_len=ts, grid_mode=True, want_state=False)
    return (y_p.reshape(bp, tp, d), y_s.reshape(bs, ts, d), new_state_lru, new_state_rwkv)
```

```python
import functools

import jax
import jax.numpy as jnp
from jax import lax
from jax.experimental import pallas as pl
from jax.experimental.pallas import tpu as pltpu

F32 = jnp.float32
BF = jnp.bfloat16

NORM_EPS = 1e-6
GN_EPS = 64e-5
LRU_C = 8.0
GRID_W = 64
CONV_W = 4
CONV_LEFT = 2

HD = 64
GH = 4
GW = GH * HD
CH = 64
INV_BASE = 16
SCAN_GROUPS = 16
LB = 256
SUBLANES = 8
LANES = 128
HEADS_PER_TILE = LANES // HD
VMEM_LIMIT = 56 * 1024 * 1024


def _cparams(*sem):
    return pltpu.CompilerParams(dimension_semantics=sem, vmem_limit_bytes=VMEM_LIMIT)


def _dot(a, b):
    return jnp.dot(a, b, preferred_element_type=F32)


def _dot_nt(a, b):
    return lax.dot_general(a, b, (((1,), (1,)), ((), ())), preferred_element_type=F32)


def _dot_tn(a, b):
    return lax.dot_general(a, b, (((0,), (0,)), ((), ())), preferred_element_type=F32)


def _split3(x):
    h1 = x.astype(BF)
    r1 = x - h1.astype(F32)
    h2 = r1.astype(BF)
    h3 = (r1 - h2.astype(F32)).astype(BF)
    return h1, h2, h3


def _modnorm(x, g, mod):
    y = x * lax.rsqrt(jnp.mean(x * x, axis=-1, keepdims=True) + NORM_EPS) * g
    return y * (1.0 + mod[1:2]) + mod[0:1]


def _mm_body(*refs, act, a_pre, n_extra):
    a_ref, w_ref, b_ref = refs[:3]
    extras = refs[3:3 + n_extra]
    o_ref = refs[3 + n_extra]
    a = a_ref[...]
    if a_pre is not None:
        a = a_pre(a.astype(F32))
    acc = _dot(a.astype(BF), w_ref[...]) + b_ref[...]
    if act is not None:
        acc = act(acc, *[e[...] for e in extras])
    o_ref[...] = acc.astype(o_ref.dtype)


def _mm(a, w3, li, bias, *, tm, tn, m_tiles, a_imap=None, o_shape=None, o_imap=None,
        extras=(), act=None, a_pre=None, out_dtype=F32, name="mm"):
    _, K, N = w3.shape
    tn = min(tn, N)
    assert N % tn == 0 and K % 128 == 0
    n_tiles = N // tn
    if a_imap is None:
        a_imap = lambda n, m: (m, 0)
    if o_imap is None:
        o_imap = lambda n, m: (m, n)
    if o_shape is None:
        o_shape = (m_tiles * tm, N)
    if bias is None:
        bias = jnp.zeros((1, N), F32)
    in_specs = [
        pl.BlockSpec((tm, K), a_imap),
        pl.BlockSpec((None, K, tn), lambda n, m: (li, 0, n)),
        pl.BlockSpec((1, tn), lambda n, m: (0, n)),
    ] + [pl.BlockSpec((tm, tn), lambda n, m: (m, n)) for _ in extras]
    return pl.pallas_call(
        functools.partial(_mm_body, act=act, a_pre=a_pre, n_extra=len(extras)),
        out_shape=jax.ShapeDtypeStruct(o_shape, out_dtype),
        grid=(n_tiles, m_tiles),
        in_specs=in_specs,
        out_specs=pl.BlockSpec((tm, tn), o_imap),
        compiler_params=_cparams("parallel", "parallel"),
        name=name,
    )(a, w3, bias, *extras)


def _prenorm_body(x_ref, g_ref, mod_ref, o_ref):
    o_ref[...] = _modnorm(x_ref[...], g_ref[...], mod_ref[0]).astype(o_ref.dtype)


def _prenorm(x, g3, layer, mod, *, tm, rows_per_cond):
    rows, d = x.shape
    return pl.pallas_call(
        _prenorm_body,
        out_shape=jax.ShapeDtypeStruct((rows, d), BF),
        grid=(rows // tm,),
        in_specs=[
            pl.BlockSpec((tm, d), lambda i: (i, 0)),
            pl.BlockSpec((None, 1, d), lambda i: (layer, 0, 0)),
            pl.BlockSpec((1, 3, d), lambda i: ((i * tm) // rows_per_cond, 0, 0)),
        ],
        out_specs=pl.BlockSpec((tm, d), lambda i: (i, 0)),
        compiler_params=_cparams("parallel"),
        name="prenorm",
    )(x, g3, mod)


def _postnorm_body(x_ref, m_ref, g_ref, mod_ref, o_ref):
    m = m_ref[...]
    y = m * lax.rsqrt(jnp.mean(m * m, axis=-1, keepdims=True) + NORM_EPS) * g_ref[...]
    o_ref[...] = x_ref[...] + mod_ref[0][2:3] * y


def _postnorm(x, m, g3, layer, mod, *, tm, rows_per_cond):
    rows, d = x.shape
    return pl.pallas_call(
        _postnorm_body,
        out_shape=jax.ShapeDtypeStruct((rows, d), F32),
        grid=(rows // tm,),
        in_specs=[
            pl.BlockSpec((tm, d), lambda i: (i, 0)),
            pl.BlockSpec((tm, d), lambda i: (i, 0)),
            pl.BlockSpec((None, 1, d), lambda i: (layer, 0, 0)),
            pl.BlockSpec((1, 3, d), lambda i: ((i * tm) // rows_per_cond, 0, 0)),
        ],
        out_specs=pl.BlockSpec((tm, d), lambda i: (i, 0)),
        compiler_params=_cparams("parallel"),
        name="postnorm",
    )(x, m, g3, mod)


def _write_mixes(h, shifted, mu_ref, out_refs):
    xx = shifted - h
    out_refs[0][...] = h.astype(BF)
    for m in range(5):
        out_refs[m + 1][...] = (h + xx * mu_ref[m:m + 1]).astype(BF)


def _shiftmix_seq_body(x_ref, g_ref, mod_ref, mu_ref, *out_refs):
    h = _modnorm(x_ref[...], g_ref[...], mod_ref[0])
    t_len, d = h.shape
    half = d // 2
    t = lax.broadcasted_iota(jnp.int32, (t_len, 1), 0)
    prev = jnp.where(t == 0, 0.0, pltpu.roll(h[:, :half], 1, 0))
    nxt = jnp.where(t == t_len - 1, 0.0, pltpu.roll(h[:, half:], t_len - 1, 0))
    _write_mixes(h, jnp.concatenate([prev, nxt], axis=1), mu_ref, out_refs)


def _shiftmix_grid_body(x_ref, up_ref, dn_ref, g_ref, mod_ref, mu_ref, *out_refs, tiles_per_batch):
    ti = pl.program_id(0) % tiles_per_batch
    g = g_ref[...]
    mod = mod_ref[0]
    h = _modnorm(x_ref[...], g, mod)
    tr, d = h.shape
    q = d // 4
    h_up = jnp.where(ti > 0, _modnorm(up_ref[...], g, mod)[:, 2 * q:3 * q], 0.0)
    h_dn = jnp.where(ti < tiles_per_batch - 1, _modnorm(dn_ref[...], g, mod)[:, 3 * q:], 0.0)
    col = lax.broadcasted_iota(jnp.int32, (tr, 1), 0) % GRID_W
    left = jnp.where(col == 0, 0.0, pltpu.roll(h[:, :q], 1, 0))
    right = jnp.where(col == GRID_W - 1, 0.0, pltpu.roll(h[:, q:2 * q], tr - 1, 0))
    up = jnp.concatenate([h_up, h[:tr - GRID_W, 2 * q:3 * q]], axis=0)
    down = jnp.concatenate([h[GRID_W:, 3 * q:], h_dn], axis=0)
    _write_mixes(h, jnp.concatenate([left, right, up, down], axis=1), mu_ref, out_refs)


def _shiftmix(x, g3, layer, mod, mu, *, seq_len, cond_rows, grid_mode, tm):
    rows, d = x.shape
    outs = [jax.ShapeDtypeStruct((rows, d), BF)] * 6
    common = [
        pl.BlockSpec((None, 1, d), lambda i: (layer, 0, 0)),
        pl.BlockSpec((1, 3, d), lambda i: ((i * tm) // cond_rows, 0, 0)),
        pl.BlockSpec((5, d), lambda i: (0, 0)),
    ]
    out_specs = [pl.BlockSpec((tm, d), lambda i: (i, 0))] * 6
    if not grid_mode:
        assert tm == seq_len
        return pl.pallas_call(
            _shiftmix_seq_body, out_shape=outs, grid=(rows // tm,),
            in_specs=[pl.BlockSpec((tm, d), lambda i: (i, 0))] + common,
            out_specs=out_specs, compiler_params=_cparams("parallel"), name="shiftmix_seq",
        )(x, g3, mod, mu)
    per = tm // GRID_W
    n_halo = rows // GRID_W
    return pl.pallas_call(
        functools.partial(_shiftmix_grid_body, tiles_per_batch=seq_len // tm),
        out_shape=outs, grid=(rows // tm,),
        in_specs=[
            pl.BlockSpec((tm, d), lambda i: (i, 0)),
            pl.BlockSpec((GRID_W, d), lambda i: (jnp.maximum(i * per - 1, 0), 0)),
            pl.BlockSpec((GRID_W, d), lambda i: (jnp.minimum((i + 1) * per, n_halo - 1), 0)),
        ] + common,
        out_specs=out_specs, compiler_params=_cparams("parallel"), name="shiftmix_grid",
    )(x, x, x, g3, mod, mu)


def _lru_body(*refs, rb, reverse, combine, n_tiles):
    if combine:
        (x_ref, xp_ref, xn_ref, cw_ref, cb_ref, gw_ref, gb_ref, lam_ref, h0_ref, yp_ref, z_ref,
         y_ref, hf_ref, xs, a_s, u_s, hc) = refs
    else:
        (x_ref, xp_ref, xn_ref, cw_ref, cb_ref, gw_ref, gb_ref, lam_ref, h0_ref,
         y_ref, hf_ref, xs, a_s, u_s, hc) = refs
    tr = x_ref.shape[0]
    hb = xp_ref.shape[0]
    step = pl.program_id(1)
    t_idx = (n_tiles - 1 - step) if reverse else step

    @pl.when(step == 0)
    def _():
        hc[...] = h0_ref[...]

    xs[hb:hb + tr, :] = x_ref[...].astype(F32)
    xs[0:hb, :] = jnp.where(t_idx > 0, xp_ref[...].astype(F32), 0.0)
    xs[hb + tr:hb + tr + hb, :] = jnp.where(t_idx < n_tiles - 1, xn_ref[...].astype(F32), 0.0)
    xc = cb_ref[...]
    for tap in range(CONV_W):
        off = hb + (tap - CONV_LEFT) * rb
        xc = xc + xs[off:off + tr, :] * cw_ref[tap:tap + 1]
    h = _lru_gates_scan(xc, gw_ref, gb_ref, lam_ref, a_s, u_s, hc[...], rb=rb, reverse=reverse)
    hc[...] = h
    hf_ref[...] = h
    if combine:
        z = z_ref[...].astype(F32)
        y_ref[...] = ((yp_ref[...] + u_s[...]) * (z * jax.nn.sigmoid(z))).astype(y_ref.dtype)
    else:
        y_ref[...] = u_s[...]


def _lru_gates_scan(xc, gw_ref, gb_ref, lam_ref, a_s, u_s, h, *, rb, reverse):
    tr = xc.shape[0]
    xb = xc.astype(BF)

    def gate(n):
        return 0.5 * jnp.tanh(0.5 * (_dot(xb, gw_ref[n].astype(BF)) + gb_ref[n:n + 1])) + 0.5

    r = gate(0)
    i = gate(1)
    log_a = r * (-LRU_C * jax.nn.softplus(-lam_ref[...]))
    a_s[...] = jnp.exp(log_a)
    th = jnp.tanh(log_a)
    u_s[...] = jnp.sqrt(-2.0 * th / (1.0 - th)) * (i * xc)

    if rb % SUBLANES == 0:
        n_steps = tr // rb

        def body(s, h):
            row = pl.multiple_of(((n_steps - 1 - s) if reverse else s) * rb, rb)
            h = a_s[pl.ds(row, rb), :] * h + u_s[pl.ds(row, rb), :]
            u_s[pl.ds(row, rb), :] = h
            return h

        h = lax.fori_loop(0, n_steps, body, h, unroll=4)
    else:
        assert 2 * rb == SUBLANES
        n_steps = tr // SUBLANES
        first = lax.broadcasted_iota(jnp.int32, (SUBLANES, 1), 0) < rb

        def body(s, h):
            row = pl.multiple_of(((n_steps - 1 - s) if reverse else s) * SUBLANES, SUBLANES)
            a = a_s[pl.ds(row, SUBLANES), :]
            u = u_s[pl.ds(row, SUBLANES), :]
            h1 = a * h + u
            h2 = a * pltpu.roll(h1, rb, 0) + u
            u_s[pl.ds(row, SUBLANES), :] = jnp.where(first != reverse, h1, h2)
            return pltpu.roll(h2, rb, 0)

        h = lax.fori_loop(0, n_steps, body, h, unroll=4)
    return h


def _lru_rows_body(*refs, nb, reverse, combine, n_tiles):
    if combine:
        (x_ref, xp_ref, xn_ref, cw_ref, cb_ref, gw_ref, gb_ref, lam_ref, h0_ref, yp_ref, z_ref,
         y_ref, hf_ref, xs, a_s, u_s, hc, zs) = refs
    else:
        (x_ref, xp_ref, xn_ref, cw_ref, cb_ref, gw_ref, gb_ref, lam_ref, h0_ref,
         y_ref, hf_ref, xs, a_s, u_s, hc) = refs
    tt = x_ref.shape[1]
    ht = xp_ref.shape[1]
    tr, hb = tt * nb, ht * nb
    tiles = [slice(t * LANES, (t + 1) * LANES) for t in range(LB // LANES)]
    step = pl.program_id(1)
    t_idx = (n_tiles - 1 - step) if reverse else step

    @pl.when(step == 0)
    def _():
        hc[...] = h0_ref[...]

    for b in range(nb):
        for t, ln in enumerate(tiles):
            xs[t, pl.ds(hb + b, tt, stride=nb), :] = x_ref[b, :, ln].astype(F32)
            xs[t, pl.ds(b, ht, stride=nb), :] = jnp.where(t_idx > 0, xp_ref[b, :, ln].astype(F32), 0.0)
            xs[t, pl.ds(hb + tr + b, ht, stride=nb), :] = jnp.where(
                t_idx < n_tiles - 1, xn_ref[b, :, ln].astype(F32), 0.0)
    parts = []
    for t, ln in enumerate(tiles):
        acc = cb_ref[:, ln]
        for tap in range(CONV_W):
            off = hb + (tap - CONV_LEFT) * nb
            acc = acc + xs[t, off:off + tr, :] * cw_ref[tap:tap + 1, ln]
        parts.append(acc)
    xc = jnp.concatenate(parts, axis=1)
    h = _lru_gates_scan(xc, gw_ref, gb_ref, lam_ref, a_s, u_s, hc[...], rb=nb, reverse=reverse)
    hc[...] = h
    hf_ref[...] = h
    if not combine:
        y_ref[...] = u_s[...]
        return
    for b in range(nb):
        for t, ln in enumerate(tiles):
            zs[t, pl.ds(b, tt, stride=nb), :] = z_ref[b, :, ln].astype(F32)
    z = jnp.concatenate([zs[t] for t in range(len(tiles))], axis=1)
    res = (yp_ref[...] + u_s[...]) * (z * jax.nn.sigmoid(z))
    for t, ln in enumerate(tiles):
        zs[t] = res[:, ln]
    for b in range(nb):
        for t, ln in enumerate(tiles):
            y_ref[b, :, ln] = zs[t, pl.ds(b, tt, stride=nb), :].astype(y_ref.dtype)


def _lru_core_rows(xz3, lru, j, d, h0, *, tt, y_prev=None):
    nb, seq_len, e2 = xz3.shape
    e = e2 // 2
    nblk = e // LB
    n_tiles = seq_len // tt
    ht = 2 * SUBLANES
    per = tt // ht
    n_halo = seq_len // ht
    tr, hb = tt * nb, ht * nb
    hr = h0.shape[0]
    reverse = d == 1
    combine = y_prev is not None

    def tidx(s):
        return (n_tiles - 1 - s) if reverse else s

    in_specs = [
        pl.BlockSpec((nb, tt, LB), lambda c, s: (0, tidx(s), c)),
        pl.BlockSpec((nb, ht, LB), lambda c, s: (0, jnp.maximum(tidx(s) * per - 1, 0), c)),
        pl.BlockSpec((nb, ht, LB), lambda c, s: (0, jnp.minimum((tidx(s) + 1) * per, n_halo - 1), c)),
        pl.BlockSpec((None, CONV_W, LB), lambda c, s: (j, 0, c)),
        pl.BlockSpec((None, 1, LB), lambda c, s: (j, 0, c)),
        pl.BlockSpec((None, 2, None, LB, LB), lambda c, s: (j * 2 + d, 0, c, 0, 0)),
        pl.BlockSpec((None, 2, LB), lambda c, s: (j * 2 + d, 0, c)),
        pl.BlockSpec((None, 1, LB), lambda c, s: (j * 2 + d, 0, c)),
        pl.BlockSpec((hr, LB), lambda c, s: (0, c)),
    ]
    args = [xz3, xz3, xz3, lru["conv_w"], lru["conv_b"], lru["gate_w"], lru["gate_b"], lru["lam"], h0]
    lt = LB // LANES
    scratch = [pltpu.VMEM((lt, tr + 2 * hb, LANES), F32), pltpu.VMEM((tr, LB), F32),
               pltpu.VMEM((tr, LB), F32), pltpu.VMEM((hr, LB), F32)]
    if combine:
        in_specs += [
            pl.BlockSpec((tr, LB), lambda c, s: (tidx(s), c)),
            pl.BlockSpec((nb, tt, LB), lambda c, s: (0, tidx(s), nblk + c)),
        ]
        args += [y_prev, xz3]
        scratch.append(pltpu.VMEM((lt, tr, LANES), F32))
        y_shape = jax.ShapeDtypeStruct((nb, seq_len, e), BF)
        y_spec = pl.BlockSpec((nb, tt, LB), lambda c, s: (0, tidx(s), c))
    else:
        y_shape = jax.ShapeDtypeStruct((seq_len * nb, e), F32)
        y_spec = pl.BlockSpec((tr, LB), lambda c, s: (tidx(s), c))
    return pl.pallas_call(
        functools.partial(_lru_rows_body, nb=nb, reverse=reverse, combine=combine, n_tiles=n_tiles),
        out_shape=[y_shape, jax.ShapeDtypeStruct((hr, e), F32)],
        grid=(nblk, n_tiles),
        in_specs=in_specs,
        out_specs=[y_spec, pl.BlockSpec((hr, LB), lambda c, s: (0, c))],
        scratch_shapes=scratch,
        compiler_params=_cparams("parallel", "arbitrary"),
        name="lru_core_rows",
    )(*args)


def _lru_core(xz, lru, j, d, h0, rb, *, tr, y_prev=None):
    rows, e2 = xz.shape
    e = e2 // 2
    nblk = e // LB
    n_tiles = rows // tr
    hb = max(2 * rb, 2 * SUBLANES)
    per = tr // hb
    n_halo = rows // hb
    hr = h0.shape[0]
    reverse = d == 1
    combine = y_prev is not None

    def tidx(s):
        return (n_tiles - 1 - s) if reverse else s

    in_specs = [
        pl.BlockSpec((tr, LB), lambda c, s: (tidx(s), c)),
        pl.BlockSpec((hb, LB), lambda c, s: (jnp.maximum(tidx(s) * per - 1, 0), c)),
        pl.BlockSpec((hb, LB), lambda c, s: (jnp.minimum((tidx(s) + 1) * per, n_halo - 1), c)),
        pl.BlockSpec((None, CONV_W, LB), lambda c, s: (j, 0, c)),
        pl.BlockSpec((None, 1, LB), lambda c, s: (j, 0, c)),
        pl.BlockSpec((None, 2, None, LB, LB), lambda c, s: (j * 2 + d, 0, c, 0, 0)),
        pl.BlockSpec((None, 2, LB), lambda c, s: (j * 2 + d, 0, c)),
        pl.BlockSpec((None, 1, LB), lambda c, s: (j * 2 + d, 0, c)),
        pl.BlockSpec((hr, LB), lambda c, s: (0, c)),
    ]
    args = [xz, xz, xz, lru["conv_w"], lru["conv_b"], lru["gate_w"], lru["gate_b"], lru["lam"], h0]
    if combine:
        in_specs += [
            pl.BlockSpec((tr, LB), lambda c, s: (tidx(s), c)),
            pl.BlockSpec((tr, LB), lambda c, s: (tidx(s), nblk + c)),
        ]
        args += [y_prev, xz]
    return pl.pallas_call(
        functools.partial(_lru_body, rb=rb, reverse=reverse, combine=combine, n_tiles=n_tiles),
        out_shape=[jax.ShapeDtypeStruct((rows, e), BF if combine else F32),
                   jax.ShapeDtypeStruct((hr, e), F32)],
        grid=(nblk, n_tiles),
        in_specs=in_specs,
        out_specs=[pl.BlockSpec((tr, LB), lambda c, s: (tidx(s), c)),
                   pl.BlockSpec((hr, LB), lambda c, s: (0, c))],
        scratch_shapes=[pltpu.VMEM((tr + 2 * hb, LB), F32), pltpu.VMEM((tr, LB), F32),
                        pltpu.VMEM((tr, LB), F32), pltpu.VMEM((hr, LB), F32)],
        compiler_params=_cparams("parallel", "arbitrary"),
        name="lru_core",
    )(*args)


def _rwkv_scan_body(*refs, reverse, zero_init, want_state, finish, n_chunks):
    it = iter(refs)
    (r_ref, k_ref, v_ref, tw_ref, ta_ref, w2_ref, w0_ref, a2_ref, a0_ref, kp_ref, bm_ref,
     tri_ref, cat_ref, lvl_ref) = (next(it) for _ in range(14))
    if finish:
        tao_ref, a2o_ref, a0o_ref, yo_ref, zg_ref, hp_ref = (next(it) for _ in range(6))
    sel_ref = next(it) if want_state else None
    z0_ref = None if zero_init else next(it)
    y_ref = next(it)
    sf_ref = next(it) if want_state else None
    z_s = next(it)
    c = pl.program_id(2)

    @pl.when(c == 0)
    def _():
        z_s[...] = jnp.zeros_like(z_s) if zero_init else z0_ref[...]

    lw_all = _act_logdecay(_dot(tw_ref[...], w2_ref[...]) + w0_ref[...])
    as_all = jax.nn.sigmoid(_dot(ta_ref[...], a2_ref[...]) + a0_ref[...])

    groups = range(z_s.shape[0])
    lanes = [slice(g * GW, (g + 1) * GW) for g in groups]
    bm = bm_ref[...]
    rs = [r_ref[:, l] for l in lanes]
    ks = [k_ref[:, l] for l in lanes]
    vs = [v_ref[:, l] for l in lanes]
    ys, z_new = _rwkv_chunk(
        rs, ks, vs, [lw_all[:, l] for l in lanes], [as_all[:, l] for l in lanes],
        [kp_ref[:, l] for l in lanes], bm, tri_ref[...], cat_ref, lvl_ref,
        [[z_s[g, t] for t in range(z_s.shape[1])] for g in groups], reverse=reverse)
    for g in groups:
        for t in range(z_s.shape[1]):
            z_s[g, t] = z_new[g][t]
    if not finish:
        for g in groups:
            y_ref[:, lanes[g]] = ys[g]
    else:
        bm_b = bm.astype(BF)
        aso_all = jax.nn.sigmoid(_dot(tao_ref[...], a2o_ref[...]) + a0o_ref[...])

        def headsum(x):
            return sum(_dot(p, bm_b) for p in _split3(x)[:2])

        ysum = [ys[g] + yo_ref[:, lanes[g]] for g in groups]
        arg = [rs[g] * ks[g] * hp_ref[1:2, lanes[g]]
               * (2.0 + hp_ref[0:1, lanes[g]] * (as_all[:, lanes[g]] + aso_all[:, lanes[g]] - 2.0))
               for g in groups]
        sums = [headsum(jnp.concatenate([ysum[g], arg[g]], axis=0)) for g in groups]
        yc = [ysum[g] - sums[g][:CH] * (1.0 / HD) for g in groups]
        var = [headsum(yc[g] * yc[g]) * (1.0 / HD) for g in groups]
        for g in groups:
            out = (yc[g] * lax.rsqrt(var[g] + GN_EPS) * hp_ref[2:3, lanes[g]] + hp_ref[3:4, lanes[g]]
                   + sums[g][CH:] * vs[g])
            zg = zg_ref[:, lanes[g]]
            y_ref[:, lanes[g]] = (out * (zg * jax.nn.sigmoid(zg))).astype(y_ref.dtype)
    if want_state:
        @pl.when(c == n_chunks - 1)
        def _():
            sel = sel_ref[...]
            for g in groups:
                for t in range(z_s.shape[1]):
                    sf_ref[g * GW + t * LANES:g * GW + (t + 1) * LANES, :] = sum(
                        _dot_tn(p, sel) for p in _split3(z_new[g][t])[:2])


def _vmap(fn, *lists):
    return [fn(*args) for args in zip(*lists)]


def _rwkv_chunk(r, k, v, lw, asg, kp, bm, tri, cat_ref, lvl_ref, z, *, reverse):
    incl = cat_ref[0]
    strict = cat_ref[1]
    eye_c = incl - strict
    bm_b = bm.astype(BF)

    def headsum(x):
        return sum(_dot(p, bm_b) for p in _split3(x)[:2])

    lane_head = lax.broadcasted_iota(jnp.int32, (1, LANES), 1) // HD
    zero_tile = jnp.zeros((CH, LANES), BF)

    def bdm(x):
        xb = x.astype(BF)
        blocks = []
        for h in range(GH):
            j = h // HEADS_PER_TILE
            tile = xb[:, j * LANES:(j + 1) * LANES]
            keep = jnp.where(lane_head == h % HEADS_PER_TILE, tile, jnp.zeros_like(tile))
            blocks.append(jnp.concatenate([keep if t == j else zero_tile for t in range(GW // LANES)],
                                          axis=1))
        return jnp.concatenate(blocks, axis=0)

    def mm(a, b_bd):
        return _vmap(lambda x, y: _dot(x.astype(BF), y), a, b_bd)

    def stack2(a, b):
        return _vmap(lambda x, y: jnp.concatenate([x, y], axis=0), a, b)

    def top(a):
        return [x[:CH] for x in a]

    def bot(a):
        return [x[CH:] for x in a]

    def masked(a, keep):
        return [jnp.where(keep, x, 0.0) for x in a]

    kk = _vmap(lambda k_, p: k_ * p[0:1], k, kp)
    ss = _vmap(lambda x: headsum(x * x), kk)
    kk = _vmap(lambda x, s: x * lax.rsqrt(jnp.maximum(s, 1e-24)), kk, ss)
    kd = _vmap(lambda k_, a, p: k_ * (1.0 + (a - 1.0) * p[1:2]), k, asg, kp)
    bvec = _vmap(lambda x, a: x * a, kk, asg)

    cum = _vmap(lambda x: sum(_dot(tri, p) for p in _split3(x)[:2]), lw)
    tot = _vmap(lambda x: x[0:1] if reverse else x[CH - 1:CH], cum)
    e_pos = _vmap(jnp.exp, cum)
    e_neg = _vmap(lambda x: jnp.exp(-x), cum)
    e_rem = _vmap(lambda t, en: jnp.exp(t) * en, tot, e_neg)
    a_t = _vmap(lambda x, ep, l: -x * (ep * jnp.exp(-l)), kk, e_pos, lw)
    r_t = _vmap(lambda x, ep: x * ep, r, e_pos)
    ar = [x.astype(BF) for x in stack2(a_t, r_t)]
    bt_bd = _vmap(lambda x, en: bdm(x * en), bvec, e_neg)
    kt_bd = _vmap(lambda x, en: bdm(x * en), kd, e_neg)
    ab = _vmap(_dot_nt, ar, bt_bd)
    ak = _vmap(_dot_nt, ar, kt_bd)
    keep_strict, keep_incl = strict > 0, incl > 0
    a_ab, a_rb = masked(top(ab), keep_strict), masked(bot(ab), keep_incl)
    a_ak, a_rk = masked(top(ak), keep_strict), masked(bot(ak), keep_incl)

    p = [n * lvl_ref[0] for n in a_ab]
    x = [eye_c + n for n in p]
    pw = mm(p, _vmap(bdm, p))
    e = 2
    while e < INV_BASE:
        rhs = _vmap(bdm, pw)
        if 2 * e < INV_BASE:
            both = mm(stack2(x, pw), rhs)
            x = _vmap(jnp.add, x, top(both))
            pw = bot(both)
        else:
            x = _vmap(jnp.add, x, mm(x, rhs))
        e *= 2
    for lvl in range(1, lvl_ref.shape[0]):
        off = [n * lvl_ref[lvl] for n in a_ab]
        x = _vmap(jnp.add, x, mm(x, _vmap(bdm, mm(off, _vmap(bdm, x)))))

    s_v = mm(stack2(a_ak, a_rk), _vmap(bdm, v))
    def z_rhs(zq):
        n_t = len(zq)
        zero = jnp.zeros((LANES, LANES), BF)
        return jnp.concatenate(
            [jnp.concatenate([zq[i].astype(BF) if i == t else zero for t in range(n_t)], axis=1)
             for i in range(n_t)], axis=0)

    s_z = _vmap(lambda a, zq: _dot(a, z_rhs(zq)), ar, z)
    u = mm(x, _vmap(lambda a, b_: bdm(a + b_), top(s_z), top(s_v)))
    y = _vmap(lambda a, b_, c_: a + b_ + c_, bot(s_z), bot(s_v), mm(a_rb, _vmap(bdm, u)))
    row_id = lax.broadcasted_iota(jnp.int32, (LANES, 1), 0)
    lane_id = lax.broadcasted_iota(jnp.int32, (1, LANES), 1)
    same_head = (row_id // HD) == (lane_id // HD)
    diag = row_id == lane_id
    lhs = _vmap(lambda b_, k_, er: jnp.concatenate([b_ * er, k_ * er], axis=0).astype(BF), bvec, kd, e_rem)
    rhs = _vmap(lambda u_, v_: jnp.concatenate([u_, v_], axis=0).astype(BF), u, v)
    decay = _vmap(jnp.exp, tot)
    z_new = [[] for _ in z]
    for t in range(GW // LANES):
        cols = slice(t * LANES, (t + 1) * LANES)
        for g in range(len(z)):
            upd = jnp.where(same_head, _dot_tn(lhs[g][:, cols], rhs[g][:, cols]), 0.0)
            g_col = jnp.sum(jnp.where(diag, decay[g][:, cols], 0.0), axis=1, keepdims=True)
            z_new[g].append(z[g][t] * g_col + upd)
    return y, z_new


def _rwkv_scan(r, k, v, tw, ta, lora, jd, kp, consts, d, *, batch, seq_len, z0=None, want_state=False,
               finish=None):
    rows, e = r.shape
    ngs = min(SCAN_GROUPS, e // GW)
    sw = ngs * GW
    ng = e // sw
    nc = seq_len // CH
    reverse = d == 1
    zero_init = z0 is None
    rank = lora["w2"].shape[1]

    def cidx(c):
        return (nc - 1 - c) if reverse else c

    def lora_specs(dd, jdd):
        return [pl.BlockSpec((CH, rank), lambda b, g, c: (b * nc + cidx(c), dd)),
                pl.BlockSpec((None, rank, sw), lambda b, g, c: (jdd, 0, g)),
                pl.BlockSpec((None, 1, sw), lambda b, g, c: (jdd, 0, g))]

    tok = pl.BlockSpec((CH, sw), lambda b, g, c: (b * nc + cidx(c), g))
    w_specs, a_specs = lora_specs(d, jd), lora_specs(d, jd)
    in_specs = [tok, tok, tok, w_specs[0], a_specs[0], w_specs[1], w_specs[2], a_specs[1], a_specs[2],
                pl.BlockSpec((2, sw), lambda b, g, c: (0, g)),
                pl.BlockSpec((GW, GW), lambda b, g, c: (0, 0)),
                pl.BlockSpec((None, CH, CH), lambda b, g, c: (d, 0, 0)),
                pl.BlockSpec((None, 2, CH, GW), lambda b, g, c: (d, 0, 0, 0)),
                pl.BlockSpec(consts["lvl"].shape, lambda b, g, c: (0, 0, 0))]
    args = [r, k, v, tw, ta, lora["w2"], lora["w0"], lora["a2"], lora["a0"], kp, consts["bm"],
            consts["tri"], consts["cat"], consts["lvl"]]
    if finish is not None:
        y_other, zg, hp = finish
        in_specs += lora_specs(1 - d, jd + 1 - 2 * d) + [tok, tok, pl.BlockSpec((4, sw), lambda b, g, c: (0, g))]
        args += [ta, lora["a2"], lora["a0"], y_other, zg, hp]
    if want_state:
        in_specs.append(pl.BlockSpec((LANES, HD), lambda b, g, c: (0, 0)))
        args.append(consts["sel"])
    if not zero_init:
        in_specs.append(pl.BlockSpec((None, ngs, GW // LANES, LANES, LANES), lambda b, g, c: (b, g, 0, 0, 0)))
        args.append(z0)
    out_shape = [jax.ShapeDtypeStruct((rows, e), F32 if finish is None else BF)]
    out_specs = [tok]
    if want_state:
        out_shape.append(jax.ShapeDtypeStruct((batch, e, HD), F32))
        out_specs.append(pl.BlockSpec((None, sw, HD), lambda b, g, c: (b, g, 0)))
    res = pl.pallas_call(
        functools.partial(_rwkv_scan_body, reverse=reverse, zero_init=zero_init,
                          want_state=want_state, finish=finish is not None, n_chunks=nc),
        out_shape=out_shape,
        grid=(batch, ng, nc),
        in_specs=in_specs,
        out_specs=out_specs,
        scratch_shapes=[pltpu.VMEM((ngs, GW // LANES, LANES, LANES), F32)],
        compiler_params=_cparams("parallel", "parallel", "arbitrary"),
        name="rwkv_scan",
    )(*args)
    return res if want_state else (res[0], None)


def _silu(x):
    return x * jax.nn.sigmoid(x)


def _act_tanh(x):
    return jnp.tanh(x)


def _act_sigmoid(x):
    return jax.nn.sigmoid(x)


def _act_logdecay(x):
    return -jnp.exp(-0.5) * jax.nn.sigmoid(x)


def _act_vmix(acc, v_first, mix):
    return acc + (v_first - acc) * mix


def _trunk(x, mods, init_lru, init_rwkv, W, *, batch, seq_len, grid_mode, want_state):
    rows, d = x.shape
    e = W["lru_w_out"].shape[1]
    depth = W["norm_pre"].shape[0]
    tm_row = min(256, seq_len)
    tm = min(1024, seq_len)
    nb = seq_len // tm
    m_tiles = rows // tm
    rb = batch
    lru_tr = max(2048 // rb, 1) * rb if rb % SUBLANES == 0 else 2048
    lru_tr = min(lru_tr, rows)
    cond_rows = rows // mods[0].shape[0]
    tn_in = min(1024, 2 * e)
    fin_lru, fin_rwkv = [], []
    v_first = None

    for i in range(depth):
        mod = mods[i]
        j = i // 2
        if i % 2 == 0:
            h = _prenorm(x, W["norm_pre"], i, mod, tm=tm_row, rows_per_cond=cond_rows)
            if rb % SUBLANES == 0:
                xz = _mm(h, W["lru_w_in"], j, None, tm=tm, tn=tn_in, m_tiles=m_tiles,
                         o_shape=(seq_len, batch * 2 * e),
                         o_imap=lambda n, m: (m % nb, (m // nb) * (2 * e // tn_in) + n), out_dtype=BF,
                         name="lru_in")
                xz = xz.reshape(seq_len * batch, 2 * e)
                y0, hf0 = _lru_core(xz, W["lru"], j, 0, init_lru[j][0], rb, tr=lru_tr)
                y, hf1 = _lru_core(xz, W["lru"], j, 1, init_lru[j][1], rb, tr=lru_tr, y_prev=y0)
                y = y.reshape(seq_len, batch * e)
                m = _mm(y, W["lru_w_out"], j, None, tm=tm, tn=1024, m_tiles=m_tiles,
                        a_imap=lambda n, m: (m % nb, m // nb), name="lru_out")
            else:
                tm_plain = min(1024, rows)
                mk = dict(tm=tm_plain, m_tiles=rows // tm_plain)
                xz = _mm(h, W["lru_w_in"], j, None, tn=tn_in, out_dtype=BF, name="lru_in", **mk)
                xz = xz.reshape(batch, seq_len, 2 * e)
                tt = min(lru_tr // batch, seq_len)
                y0, hf0 = _lru_core_rows(xz, W["lru"], j, 0, init_lru[j][0], tt=tt)
                y, hf1 = _lru_core_rows(xz, W["lru"], j, 1, init_lru[j][1], tt=tt, y_prev=y0)
                m = _mm(y.reshape(rows, e), W["lru_w_out"], j, None, tn=1024, name="lru_out", **mk)
            fin_lru.append((hf0, hf1))
        else:
            h, xr, xw, xk, xv, xa = _shiftmix(
                x, W["norm_pre"], i, mod, W["rwkv_mu"][j], seq_len=seq_len, cond_rows=cond_rows, grid_mode=grid_mode,
                tm=seq_len if not grid_mode else min(512, seq_len))
            tm_plain = min(1024, rows)
            mk = dict(tm=tm_plain, m_tiles=rows // tm_plain)
            r = _mm(xr, W["rwkv_w_r"], j, None, tn=1024, name="rwkv_r", **mk)
            k = _mm(xk, W["rwkv_w_k"], j, None, tn=1024, name="rwkv_k", **mk)
            z = _mm(h, W["rwkv_w_g"], j, None, tn=1024, name="rwkv_g", **mk)
            if j == 0:
                v = _mm(xv, W["rwkv_w_v"], j, None, tn=1024, name="rwkv_v", **mk)
                v_first = v
            else:
                tv = _mm(xv, W["rwkv_v1"], j - 1, None, tn=128, out_dtype=BF, name="rwkv_v1", **mk)
                mix = _mm(tv, W["rwkv_v2"], j - 1, W["rwkv_v0"][j - 1], tn=1024, act=_act_sigmoid,
                          name="rwkv_v2", **mk)
                v = _mm(xv, W["rwkv_w_v"], j, None, tn=1024, extras=(v_first, mix), act=_act_vmix,
                        name="rwkv_v", **mk)
            tw = _mm(xw, W["rwkv_w1"], j, None, tn=256, act=_act_tanh, out_dtype=BF, name="rwkv_w1", **mk)
            ta = _mm(xa, W["rwkv_a1"], j, None, tn=256, out_dtype=BF, name="rwkv_a1", **mk)
            fins = []
            y_d = None
            for dd in range(2):
                z0 = None if init_rwkv is None else init_rwkv[j][dd]
                finish = None if dd == 0 else (y_d, z, W["rwkv_hp"][j])
                y_d, s_d = _rwkv_scan(r, k, v, tw, ta, W["rwkv_lora"], j * 2 + dd, W["rwkv_kp"][j],
                                      W["consts"], dd, batch=batch, seq_len=seq_len, z0=z0,
                                      want_state=want_state, finish=finish)
                fins.append(s_d)
            fin_rwkv.append(fins)
            m = _mm(y_d, W["rwkv_w_o"], j, None, tn=1024, name="rwkv_o", **mk)
        x = _postnorm(x, m, W["norm_post"], i, mod, tm=tm_row, rows_per_cond=cond_rows)
    return x, fin_lru, fin_rwkv


def _scan_consts():
    idx = jnp.arange(GW)
    bm = ((idx[:, None] // HD) == (idx[None, :] // HD)).astype(F32)
    t_r, t_c = jnp.arange(CH)[:, None], idx[None, :] % CH
    cats, tris = [], []
    ic = jnp.arange(CH)
    for d in range(2):
        ge = (t_r >= t_c) if d == 0 else (t_r <= t_c)
        gt = (t_r > t_c) if d == 0 else (t_r < t_c)
        cats.append(jnp.stack([ge.astype(F32), gt.astype(F32)]))
        tris.append(((ic[:, None] >= ic[None, :]) if d == 0 else (ic[:, None] <= ic[None, :])).astype(BF))
    sel = (idx[:, None] % HD == jnp.arange(HD)[None, :]).astype(BF)
    lvl = [t_r // INV_BASE == t_c // INV_BASE]
    size = INV_BASE
    while size < CH:
        lvl.append((t_r // (2 * size) == t_c // (2 * size)) & (t_r // size != t_c // size))
        size *= 2
    return dict(bm=bm, cat=jnp.stack(cats), tri=jnp.stack(tris), sel=sel,
                lvl=jnp.stack(lvl).astype(F32))


def kernel(x_prompt, x_sample, state_lru, state_rwkv, c, c_ctx, ada_w, ada_b, norm_pre, norm_post, lru_w_in, lru_conv_w, lru_conv_b, lru_gate_w, lru_gate_b, lru_lambda, lru_w_out, rwkv_mu, rwkv_w_r, rwkv_w_k, rwkv_w_v, rwkv_w_g, rwkv_w_o, rwkv_w0, rwkv_w1, rwkv_w2, rwkv_a0, rwkv_a1, rwkv_a2, rwkv_k_k, rwkv_k_a, rwkv_r_k, rwkv_ln_w, rwkv_ln_b, rwkv_v0, rwkv_v1, rwkv_v2):
    bp, tp, d = x_prompt.shape
    bs, ts, _ = x_sample.shape
    depth = ada_w.shape[0]
    n_lru, _, e2 = lru_w_in.shape
    e = e2 // 2
    n_rwkv = rwkv_w_r.shape[0]
    nblk = e // LB
    heads = e // HD
    ng = e // GW

    def merge_lead(w, k):
        return w.reshape((-1,) + w.shape[k:])

    lora_v = rwkv_v1.shape[-1]
    pad_v = (-lora_v) % 128
    W = dict(
        norm_pre=norm_pre.reshape(depth, 1, d), norm_post=norm_post.reshape(depth, 1, d),
        lru_w_in=lru_w_in.astype(BF), lru_w_out=lru_w_out.astype(BF),
        lru=dict(conv_w=lru_conv_w, conv_b=lru_conv_b.reshape(n_lru, 1, e),
                 gate_w=merge_lead(lru_gate_w, 2),
                 gate_b=merge_lead(lru_gate_b, 2),
                 lam=lru_lambda.reshape(n_lru * 2, 1, e)),
        rwkv_mu=rwkv_mu,
        rwkv_w_r=rwkv_w_r.astype(BF), rwkv_w_k=rwkv_w_k.astype(BF), rwkv_w_v=rwkv_w_v.astype(BF),
        rwkv_w_g=rwkv_w_g.astype(BF), rwkv_w_o=rwkv_w_o.astype(BF),
        rwkv_w1=jnp.concatenate([rwkv_w1[:, 0], rwkv_w1[:, 1]], axis=-1).astype(BF),
        rwkv_a1=jnp.concatenate([rwkv_a1[:, 0], rwkv_a1[:, 1]], axis=-1).astype(BF),
        rwkv_lora=dict(w2=merge_lead(rwkv_w2, 2).astype(BF), a2=merge_lead(rwkv_a2, 2).astype(BF),
                       w0=rwkv_w0.reshape(n_rwkv * 2, 1, e), a0=rwkv_a0.reshape(n_rwkv * 2, 1, e)),
        rwkv_v0=rwkv_v0.reshape(-1, 1, e),
        rwkv_v1=jnp.pad(rwkv_v1, ((0, 0), (0, 0), (0, pad_v))).astype(BF),
        rwkv_v2=jnp.pad(rwkv_v2, ((0, 0), (0, pad_v), (0, 0))).astype(BF),
        rwkv_kp=jnp.stack([rwkv_k_k, rwkv_k_a], axis=1),
        rwkv_hp=jnp.stack([rwkv_k_a, rwkv_r_k, rwkv_ln_w, rwkv_ln_b], axis=1),
        consts=_scan_consts(),
    )

    n_cond = 1 + bs
    cond = jnp.zeros((SUBLANES * pl.cdiv(n_cond, SUBLANES), d), F32)
    cond = cond.at[0].set(c_ctx).at[1:n_cond].set(c)
    ada_wb = ada_w.astype(BF)
    mods_p, mods_s = [], []
    for i in range(depth):
        mod = _mm(cond, ada_wb, i, ada_b[i][None], tm=cond.shape[0], tn=256, m_tiles=1,
                  a_pre=_silu, name="adaln")
        mod = mod.reshape(cond.shape[0], 3, d)
        mods_p.append(mod[0:1])
        mods_s.append(mod[1:n_cond])

    zero_lru = [(jnp.zeros((bp, e), F32), jnp.zeros((bp, e), F32))] * n_lru
    y_p, fin_lru, fin_rwkv = _trunk(x_prompt.reshape(bp * tp, d), mods_p, zero_lru, None, W,
                                    batch=bp, seq_len=tp, grid_mode=False, want_state=True)
    new_state_lru = jnp.stack([jnp.stack(f, axis=1) for f in fin_lru], axis=1)
    new_state_rwkv = jnp.stack(
        [jnp.stack([s.reshape(bp, heads, HD, HD) for s in f], axis=1) for f in fin_rwkv], axis=1)

    rep = max(SUBLANES // bs, 1)
    init_lru = [tuple(jnp.tile(state_lru[:, jj, dd], (rep, 1)) for dd in range(2)) for jj in range(n_lru)]
    eye_t = jnp.eye(HEADS_PER_TILE, dtype=F32)
    n_tiles = e // LANES
    st = state_rwkv.reshape(bs, n_rwkv, 2, n_tiles, HEADS_PER_TILE, HD, HD)
    z0 = jnp.einsum("bldthvk,hi->bldthkiv", st, eye_t).reshape(
        bs, n_rwkv, 2, ng, GW // LANES, LANES, LANES)
    init_rwkv = [[z0[:, jj, dd] for dd in range(2)] for jj in range(n_rwkv)]
    y_s, _, _ = _trunk(x_sample.reshape(bs * ts, d), mods_s, init_lru, init_rwkv, W,
                       batch=bs, seq_len=ts, grid_mode=True, want_state=False)
    return (y_p.reshape(bp, tp, d), y_s.reshape(bs, ts, d), new_state_lru, new_state_rwkv)
```

```python
import functools

import jax
import jax.numpy as jnp
from jax import lax
from jax.experimental import pallas as pl
from jax.experimental.pallas import tpu as pltpu

F32 = jnp.float32
BF = jnp.bfloat16

NORM_EPS = 1e-6
GN_EPS = 64e-5
LRU_C = 8.0
GRID_W = 64
CONV_W = 4
CONV_LEFT = 2

HD = 64
GH = 4
GW = GH * HD
CH = 64
INV_BASE = 16
SCAN_GROUPS = 16
LB = 256
SUBLANES = 8
LANES = 128
HEADS_PER_TILE = LANES // HD
VMEM_LIMIT = 56 * 1024 * 1024
MM_VMEM_BUDGET = 44 * 1024 * 1024


def _cparams(*sem):
    return pltpu.CompilerParams(dimension_semantics=sem, vmem_limit_bytes=VMEM_LIMIT)


def _dot(a, b):
    return jnp.dot(a, b, preferred_element_type=F32)


def _dot_nt(a, b):
    return lax.dot_general(a, b, (((1,), (1,)), ((), ())), preferred_element_type=F32)


def _dot_tn(a, b):
    return lax.dot_general(a, b, (((0,), (0,)), ((), ())), preferred_element_type=F32)


def _split3(x):
    h1 = x.astype(BF)
    r1 = x - h1.astype(F32)
    h2 = r1.astype(BF)
    h3 = (r1 - h2.astype(F32)).astype(BF)
    return h1, h2, h3


def _modnorm(x, g, mod):
    y = x * lax.rsqrt(jnp.mean(x * x, axis=-1, keepdims=True) + NORM_EPS) * g
    return y * (1.0 + mod[1:2]) + mod[0:1]


def _mm_body(*refs, act, a_pre, n_extra, cast_w):
    a_ref, w_ref, b_ref = refs[:3]
    extras = refs[3:3 + n_extra]
    o_ref = refs[3 + n_extra]
    if cast_w:
        wb_ref = refs[4 + n_extra]

        @pl.when(pl.program_id(1) == 0)
        def _():
            wb_ref[...] = w_ref[...].astype(BF)

        w_ref = wb_ref
    a = a_ref[...]
    if a_pre is not None:
        a = a_pre(a.astype(F32))
    acc = _dot(a.astype(BF), w_ref[...]) + b_ref[...]
    if act is not None:
        acc = act(acc, *[e[...] for e in extras])
    o_ref[...] = acc.astype(o_ref.dtype)


def _mm_tn(k, n, tm, tn_max, n_extra, w_itemsize, out_itemsize):
    tn = min(tn_max, n)
    while tn > LANES:
        w_bytes = k * tn * (2 * w_itemsize + (2 if w_itemsize == 4 else 0))
        io_bytes = 2 * tm * (k * 2 + tn * (out_itemsize + 4 * n_extra))
        if n % tn == 0 and w_bytes + io_bytes <= MM_VMEM_BUDGET:
            break
        tn //= 2
    return tn


def _mm(a, w3, li, bias, *, tm, tn, m_tiles, a_imap=None, o_shape=None, o_imap=None,
        extras=(), act=None, a_pre=None, out_dtype=F32, name="mm"):
    _, K, N = w3.shape
    cast_w = w3.dtype != BF
    if o_imap is None:
        tn = _mm_tn(K, N, tm, tn, len(extras), w3.dtype.itemsize, jnp.dtype(out_dtype).itemsize)
    tn = min(tn, N)
    assert N % tn == 0 and K % 128 == 0
    n_tiles = N // tn
    if a_imap is None:
        a_imap = lambda n, m: (m, 0)
    if o_imap is None:
        o_imap = lambda n, m: (m, n)
    if o_shape is None:
        o_shape = (m_tiles * tm, N)
    if bias is None:
        bias = jnp.zeros((1, N), F32)
    in_specs = [
        pl.BlockSpec((tm, K), a_imap),
        pl.BlockSpec((None, K, tn), lambda n, m: (li, 0, n)),
        pl.BlockSpec((1, tn), lambda n, m: (0, n)),
    ] + [pl.BlockSpec((tm, tn), lambda n, m: (m, n)) for _ in extras]
    return pl.pallas_call(
        functools.partial(_mm_body, act=act, a_pre=a_pre, n_extra=len(extras), cast_w=cast_w),
        out_shape=jax.ShapeDtypeStruct(o_shape, out_dtype),
        grid=(n_tiles, m_tiles),
        in_specs=in_specs,
        out_specs=pl.BlockSpec((tm, tn), o_imap),
        scratch_shapes=[pltpu.VMEM((K, tn), BF)] if cast_w else [],
        compiler_params=_cparams("parallel", "arbitrary"),
        name=name,
    )(a, w3, bias, *extras)


def _prenorm_body(x_ref, g_ref, mod_ref, o_ref):
    o_ref[...] = _modnorm(x_ref[...], g_ref[...], mod_ref[0]).astype(o_ref.dtype)


def _prenorm(x, g3, layer, mod, *, tm, rows_per_cond):
    rows, d = x.shape
    return pl.pallas_call(
        _prenorm_body,
        out_shape=jax.ShapeDtypeStruct((rows, d), BF),
        grid=(rows // tm,),
        in_specs=[
            pl.BlockSpec((tm, d), lambda i: (i, 0)),
            pl.BlockSpec((None, 1, d), lambda i: (layer, 0, 0)),
            pl.BlockSpec((1, 3, d), lambda i: ((i * tm) // rows_per_cond, 0, 0)),
        ],
        out_specs=pl.BlockSpec((tm, d), lambda i: (i, 0)),
        compiler_params=_cparams("parallel"),
        name="prenorm",
    )(x, g3, mod)


def _postnorm_body(x_ref, m_ref, g_ref, mod_ref, o_ref):
    m = m_ref[...]
    y = m * lax.rsqrt(jnp.mean(m * m, axis=-1, keepdims=True) + NORM_EPS) * g_ref[...]
    o_ref[...] = x_ref[...] + mod_ref[0][2:3] * y


def _postnorm(x, m, g3, layer, mod, *, tm, rows_per_cond):
    rows, d = x.shape
    return pl.pallas_call(
        _postnorm_body,
        out_shape=jax.ShapeDtypeStruct((rows, d), F32),
        grid=(rows // tm,),
        in_specs=[
            pl.BlockSpec((tm, d), lambda i: (i, 0)),
            pl.BlockSpec((tm, d), lambda i: (i, 0)),
            pl.BlockSpec((None, 1, d), lambda i: (layer, 0, 0)),
            pl.BlockSpec((1, 3, d), lambda i: ((i * tm) // rows_per_cond, 0, 0)),
        ],
        out_specs=pl.BlockSpec((tm, d), lambda i: (i, 0)),
        compiler_params=_cparams("parallel"),
        name="postnorm",
    )(x, m, g3, mod)


def _write_mixes(h, shifted, mu_ref, out_refs):
    xx = shifted - h
    out_refs[0][...] = h.astype(BF)
    for m in range(5):
        out_refs[m + 1][...] = (h + xx * mu_ref[m:m + 1]).astype(BF)


def _shiftmix_seq_body(x_ref, g_ref, mod_ref, mu_ref, *out_refs):
    h = _modnorm(x_ref[...], g_ref[...], mod_ref[0])
    t_len, d = h.shape
    half = d // 2
    t = lax.broadcasted_iota(jnp.int32, (t_len, 1), 0)
    prev = jnp.where(t == 0, 0.0, pltpu.roll(h[:, :half], 1, 0))
    nxt = jnp.where(t == t_len - 1, 0.0, pltpu.roll(h[:, half:], t_len - 1, 0))
    _write_mixes(h, jnp.concatenate([prev, nxt], axis=1), mu_ref, out_refs)


def _shiftmix_grid_body(x_ref, up_ref, dn_ref, g_ref, mod_ref, mu_ref, *out_refs, tiles_per_batch):
    ti = pl.program_id(0) % tiles_per_batch
    g = g_ref[...]
    mod = mod_ref[0]
    h = _modnorm(x_ref[...], g, mod)
    tr, d = h.shape
    q = d // 4
    h_up = jnp.where(ti > 0, _modnorm(up_ref[...], g, mod)[:, 2 * q:3 * q], 0.0)
    h_dn = jnp.where(ti < tiles_per_batch - 1, _modnorm(dn_ref[...], g, mod)[:, 3 * q:], 0.0)
    col = lax.broadcasted_iota(jnp.int32, (tr, 1), 0) % GRID_W
    left = jnp.where(col == 0, 0.0, pltpu.roll(h[:, :q], 1, 0))
    right = jnp.where(col == GRID_W - 1, 0.0, pltpu.roll(h[:, q:2 * q], tr - 1, 0))
    up = jnp.concatenate([h_up, h[:tr - GRID_W, 2 * q:3 * q]], axis=0)
    down = jnp.concatenate([h[GRID_W:, 3 * q:], h_dn], axis=0)
    _write_mixes(h, jnp.concatenate([left, right, up, down], axis=1), mu_ref, out_refs)


def _shiftmix(x, g3, layer, mod, mu, *, seq_len, cond_rows, grid_mode, tm):
    rows, d = x.shape
    outs = [jax.ShapeDtypeStruct((rows, d), BF)] * 6
    common = [
        pl.BlockSpec((None, 1, d), lambda i: (layer, 0, 0)),
        pl.BlockSpec((1, 3, d), lambda i: ((i * tm) // cond_rows, 0, 0)),
        pl.BlockSpec((5, d), lambda i: (0, 0)),
    ]
    out_specs = [pl.BlockSpec((tm, d), lambda i: (i, 0))] * 6
    if not grid_mode:
        assert tm == seq_len
        return pl.pallas_call(
            _shiftmix_seq_body, out_shape=outs, grid=(rows // tm,),
            in_specs=[pl.BlockSpec((tm, d), lambda i: (i, 0))] + common,
            out_specs=out_specs, compiler_params=_cparams("parallel"), name="shiftmix_seq",
        )(x, g3, mod, mu)
    per = tm // GRID_W
    n_halo = rows // GRID_W
    return pl.pallas_call(
        functools.partial(_shiftmix_grid_body, tiles_per_batch=seq_len // tm),
        out_shape=outs, grid=(rows // tm,),
        in_specs=[
            pl.BlockSpec((tm, d), lambda i: (i, 0)),
            pl.BlockSpec((GRID_W, d), lambda i: (jnp.maximum(i * per - 1, 0), 0)),
            pl.BlockSpec((GRID_W, d), lambda i: (jnp.minimum((i + 1) * per, n_halo - 1), 0)),
        ] + common,
        out_specs=out_specs, compiler_params=_cparams("parallel"), name="shiftmix_grid",
    )(x, x, x, g3, mod, mu)


def _lru_body(*refs, rb, reverse, combine, n_tiles):
    if combine:
        (x_ref, xp_ref, xn_ref, cw_ref, cb_ref, gw_ref, gb_ref, lam_ref, h0_ref, yp_ref, z_ref,
         y_ref, hf_ref, xs, a_s, u_s, hc) = refs
    else:
        (x_ref, xp_ref, xn_ref, cw_ref, cb_ref, gw_ref, gb_ref, lam_ref, h0_ref,
         y_ref, hf_ref, xs, a_s, u_s, hc) = refs
    tr = x_ref.shape[0]
    hb = xp_ref.shape[0]
    step = pl.program_id(1)
    t_idx = (n_tiles - 1 - step) if reverse else step

    @pl.when(step == 0)
    def _():
        hc[...] = h0_ref[...]

    xs[hb:hb + tr, :] = x_ref[...].astype(F32)
    xs[0:hb, :] = jnp.where(t_idx > 0, xp_ref[...].astype(F32), 0.0)
    xs[hb + tr:hb + tr + hb, :] = jnp.where(t_idx < n_tiles - 1, xn_ref[...].astype(F32), 0.0)
    xc = cb_ref[...]
    for tap in range(CONV_W):
        off = hb + (tap - CONV_LEFT) * rb
        xc = xc + xs[off:off + tr, :] * cw_ref[tap:tap + 1]
    h = _lru_gates_scan(xc, gw_ref, gb_ref, lam_ref, a_s, u_s, hc[...], rb=rb, reverse=reverse)
    hc[...] = h
    hf_ref[...] = h
    if combine:
        z = z_ref[...].astype(F32)
        y_ref[...] = ((yp_ref[...] + u_s[...]) * (z * jax.nn.sigmoid(z))).astype(y_ref.dtype)
    else:
        y_ref[...] = u_s[...]


def _lru_gates_scan(xc, gw_ref, gb_ref, lam_ref, a_s, u_s, h, *, rb, reverse):
    tr = xc.shape[0]
    xb = xc.astype(BF)

    def gate(n):
        return 0.5 * jnp.tanh(0.5 * (_dot(xb, gw_ref[n].astype(BF)) + gb_ref[n:n + 1])) + 0.5

    r = gate(0)
    i = gate(1)
    log_a = r * (-LRU_C * jax.nn.softplus(-lam_ref[...]))
    a_s[...] = jnp.exp(log_a)
    th = jnp.tanh(log_a)
    u_s[...] = jnp.sqrt(-2.0 * th / (1.0 - th)) * (i * xc)

    if rb % SUBLANES == 0:
        n_steps = tr // rb

        def body(s, h):
            row = pl.multiple_of(((n_steps - 1 - s) if reverse else s) * rb, rb)
            h = a_s[pl.ds(row, rb), :] * h + u_s[pl.ds(row, rb), :]
            u_s[pl.ds(row, rb), :] = h
            return h

        h = lax.fori_loop(0, n_steps, body, h, unroll=4)
    else:
        assert 2 * rb == SUBLANES
        n_steps = tr // SUBLANES
        first = lax.broadcasted_iota(jnp.int32, (SUBLANES, 1), 0) < rb

        def body(s, h):
            row = pl.multiple_of(((n_steps - 1 - s) if reverse else s) * SUBLANES, SUBLANES)
            a = a_s[pl.ds(row, SUBLANES), :]
            u = u_s[pl.ds(row, SUBLANES), :]
            h1 = a * h + u
            h2 = a * pltpu.roll(h1, rb, 0) + u
            u_s[pl.ds(row, SUBLANES), :] = jnp.where(first != reverse, h1, h2)
            return pltpu.roll(h2, rb, 0)

        h = lax.fori_loop(0, n_steps, body, h, unroll=4)
    return h


def _lru_rows_body(*refs, nb, reverse, combine, n_tiles):
    if combine:
        (x_ref, xp_ref, xn_ref, cw_ref, cb_ref, gw_ref, gb_ref, lam_ref, h0_ref, yp_ref, z_ref,
         y_ref, hf_ref, xs, a_s, u_s, hc, zs) = refs
    else:
        (x_ref, xp_ref, xn_ref, cw_ref, cb_ref, gw_ref, gb_ref, lam_ref, h0_ref,
         y_ref, hf_ref, xs, a_s, u_s, hc) = refs
    tt = x_ref.shape[1]
    ht = xp_ref.shape[1]
    tr, hb = tt * nb, ht * nb
    tiles = [slice(t * LANES, (t + 1) * LANES) for t in range(LB // LANES)]
    step = pl.program_id(1)
    t_idx = (n_tiles - 1 - step) if reverse else step

    @pl.when(step == 0)
    def _():
        hc[...] = h0_ref[...]

    for b in range(nb):
        for t, ln in enumerate(tiles):
            xs[t, pl.ds(hb + b, tt, stride=nb), :] = x_ref[b, :, ln].astype(F32)
            xs[t, pl.ds(b, ht, stride=nb), :] = jnp.where(t_idx > 0, xp_ref[b, :, ln].astype(F32), 0.0)
            xs[t, pl.ds(hb + tr + b, ht, stride=nb), :] = jnp.where(
                t_idx < n_tiles - 1, xn_ref[b, :, ln].astype(F32), 0.0)
    parts = []
    for t, ln in enumerate(tiles):
        acc = cb_ref[:, ln]
        for tap in range(CONV_W):
            off = hb + (tap - CONV_LEFT) * nb
            acc = acc + xs[t, off:off + tr, :] * cw_ref[tap:tap + 1, ln]
        parts.append(acc)
    xc = jnp.concatenate(parts, axis=1)
    h = _lru_gates_scan(xc, gw_ref, gb_ref, lam_ref, a_s, u_s, hc[...], rb=nb, reverse=reverse)
    hc[...] = h
    hf_ref[...] = h
    if not combine:
        y_ref[...] = u_s[...]
        return
    for b in range(nb):
        for t, ln in enumerate(tiles):
            zs[t, pl.ds(b, tt, stride=nb), :] = z_ref[b, :, ln].astype(F32)
    z = jnp.concatenate([zs[t] for t in range(len(tiles))], axis=1)
    res = (yp_ref[...] + u_s[...]) * (z * jax.nn.sigmoid(z))
    for t, ln in enumerate(tiles):
        zs[t] = res[:, ln]
    for b in range(nb):
        for t, ln in enumerate(tiles):
            y_ref[b, :, ln] = zs[t, pl.ds(b, tt, stride=nb), :].astype(y_ref.dtype)


def _lru_core_rows(xz3, lru, j, d, h0, *, tt, y_prev=None):
    nb, seq_len, e2 = xz3.shape
    e = e2 // 2
    nblk = e // LB
    n_tiles = seq_len // tt
    ht = 2 * SUBLANES
    per = tt // ht
    n_halo = seq_len // ht
    tr, hb = tt * nb, ht * nb
    hr = h0.shape[0]
    reverse = d == 1
    combine = y_prev is not None

    def tidx(s):
        return (n_tiles - 1 - s) if reverse else s

    in_specs = [
        pl.BlockSpec((nb, tt, LB), lambda c, s: (0, tidx(s), c)),
        pl.BlockSpec((nb, ht, LB), lambda c, s: (0, jnp.maximum(tidx(s) * per - 1, 0), c)),
        pl.BlockSpec((nb, ht, LB), lambda c, s: (0, jnp.minimum((tidx(s) + 1) * per, n_halo - 1), c)),
        pl.BlockSpec((None, CONV_W, LB), lambda c, s: (j, 0, c)),
        pl.BlockSpec((None, 1, LB), lambda c, s: (j, 0, c)),
        pl.BlockSpec((None, 2, None, LB, LB), lambda c, s: (j * 2 + d, 0, c, 0, 0)),
        pl.BlockSpec((None, 2, LB), lambda c, s: (j * 2 + d, 0, c)),
        pl.BlockSpec((None, 1, LB), lambda c, s: (j * 2 + d, 0, c)),
        pl.BlockSpec((hr, LB), lambda c, s: (0, c)),
    ]
    args = [xz3, xz3, xz3, lru["conv_w"], lru["conv_b"], lru["gate_w"], lru["gate_b"], lru["lam"], h0]
    lt = LB // LANES
    scratch = [pltpu.VMEM((lt, tr + 2 * hb, LANES), F32), pltpu.VMEM((tr, LB), F32),
               pltpu.VMEM((tr, LB), F32), pltpu.VMEM((hr, LB), F32)]
    if combine:
        in_specs += [
            pl.BlockSpec((tr, LB), lambda c, s: (tidx(s), c)),
            pl.BlockSpec((nb, tt, LB), lambda c, s: (0, tidx(s), nblk + c)),
        ]
        args += [y_prev, xz3]
        scratch.append(pltpu.VMEM((lt, tr, LANES), F32))
        y_shape = jax.ShapeDtypeStruct((nb, seq_len, e), BF)
        y_spec = pl.BlockSpec((nb, tt, LB), lambda c, s: (0, tidx(s), c))
    else:
        y_shape = jax.ShapeDtypeStruct((seq_len * nb, e), F32)
        y_spec = pl.BlockSpec((tr, LB), lambda c, s: (tidx(s), c))
    return pl.pallas_call(
        functools.partial(_lru_rows_body, nb=nb, reverse=reverse, combine=combine, n_tiles=n_tiles),
        out_shape=[y_shape, jax.ShapeDtypeStruct((hr, e), F32)],
        grid=(nblk, n_tiles),
        in_specs=in_specs,
        out_specs=[y_spec, pl.BlockSpec((hr, LB), lambda c, s: (0, c))],
        scratch_shapes=scratch,
        compiler_params=_cparams("parallel", "arbitrary"),
        name="lru_core_rows",
    )(*args)


def _lru_core(xz, lru, j, d, h0, rb, *, tr, y_prev=None):
    rows, e2 = xz.shape
    e = e2 // 2
    nblk = e // LB
    n_tiles = rows // tr
    hb = max(2 * rb, 2 * SUBLANES)
    per = tr // hb
    n_halo = rows // hb
    hr = h0.shape[0]
    reverse = d == 1
    combine = y_prev is not None

    def tidx(s):
        return (n_tiles - 1 - s) if reverse else s

    in_specs = [
        pl.BlockSpec((tr, LB), lambda c, s: (tidx(s), c)),
        pl.BlockSpec((hb, LB), lambda c, s: (jnp.maximum(tidx(s) * per - 1, 0), c)),
        pl.BlockSpec((hb, LB), lambda c, s: (jnp.minimum((tidx(s) + 1) * per, n_halo - 1), c)),
        pl.BlockSpec((None, CONV_W, LB), lambda c, s: (j, 0, c)),
        pl.BlockSpec((None, 1, LB), lambda c, s: (j, 0, c)),
        pl.BlockSpec((None, 2, None, LB, LB), lambda c, s: (j * 2 + d, 0, c, 0, 0)),
        pl.BlockSpec((None, 2, LB), lambda c, s: (j * 2 + d, 0, c)),
        pl.BlockSpec((None, 1, LB), lambda c, s: (j * 2 + d, 0, c)),
        pl.BlockSpec((hr, LB), lambda c, s: (0, c)),
    ]
    args = [xz, xz, xz, lru["conv_w"], lru["conv_b"], lru["gate_w"], lru["gate_b"], lru["lam"], h0]
    if combine:
        in_specs += [
            pl.BlockSpec((tr, LB), lambda c, s: (tidx(s), c)),
            pl.BlockSpec((tr, LB), lambda c, s: (tidx(s), nblk + c)),
        ]
        args += [y_prev, xz]
    return pl.pallas_call(
        functools.partial(_lru_body, rb=rb, reverse=reverse, combine=combine, n_tiles=n_tiles),
        out_shape=[jax.ShapeDtypeStruct((rows, e), BF if combine else F32),
                   jax.ShapeDtypeStruct((hr, e), F32)],
        grid=(nblk, n_tiles),
        in_specs=in_specs,
        out_specs=[pl.BlockSpec((tr, LB), lambda c, s: (tidx(s), c)),
                   pl.BlockSpec((hr, LB), lambda c, s: (0, c))],
        scratch_shapes=[pltpu.VMEM((tr + 2 * hb, LB), F32), pltpu.VMEM((tr, LB), F32),
                        pltpu.VMEM((tr, LB), F32), pltpu.VMEM((hr, LB), F32)],
        compiler_params=_cparams("parallel", "arbitrary"),
        name="lru_core",
    )(*args)


def _rwkv_scan_body(*refs, reverse, zero_init, want_state, finish, n_chunks):
    it = iter(refs)
    (r_ref, k_ref, v_ref, tw_ref, ta_ref, w2_ref, w0_ref, a2_ref, a0_ref, kp_ref, bm_ref,
     tri_ref, cat_ref, lvl_ref) = (next(it) for _ in range(14))
    if finish:
        tao_ref, a2o_ref, a0o_ref, yo_ref, zg_ref, hp_ref = (next(it) for _ in range(6))
    sel_ref = next(it) if want_state else None
    z0_ref = None if zero_init else next(it)
    y_ref = next(it)
    sf_ref = next(it) if want_state else None
    z_s = next(it)
    c = pl.program_id(2)

    @pl.when(c == 0)
    def _():
        z_s[...] = jnp.zeros_like(z_s) if zero_init else z0_ref[...]

    lw_all = _act_logdecay(_dot(tw_ref[...], w2_ref[...]) + w0_ref[...])
    as_all = jax.nn.sigmoid(_dot(ta_ref[...], a2_ref[...]) + a0_ref[...])

    groups = range(z_s.shape[0])
    lanes = [slice(g * GW, (g + 1) * GW) for g in groups]
    bm = bm_ref[...]
    rs = [r_ref[:, l] for l in lanes]
    ks = [k_ref[:, l] for l in lanes]
    vs = [v_ref[:, l] for l in lanes]
    ys, z_new = _rwkv_chunk(
        rs, ks, vs, [lw_all[:, l] for l in lanes], [as_all[:, l] for l in lanes],
        [kp_ref[:, l] for l in lanes], bm, tri_ref[...], cat_ref, lvl_ref,
        [[z_s[g, t] for t in range(z_s.shape[1])] for g in groups], reverse=reverse)
    for g in groups:
        for t in range(z_s.shape[1]):
            z_s[g, t] = z_new[g][t]
    if not finish:
        for g in groups:
            y_ref[:, lanes[g]] = ys[g]
    else:
        bm_b = bm.astype(BF)
        aso_all = jax.nn.sigmoid(_dot(tao_ref[...], a2o_ref[...]) + a0o_ref[...])

        def headsum(x):
            return sum(_dot(p, bm_b) for p in _split3(x)[:2])

        ysum = [ys[g] + yo_ref[:, lanes[g]] for g in groups]
        arg = [rs[g] * ks[g] * hp_ref[1:2, lanes[g]]
               * (2.0 + hp_ref[0:1, lanes[g]] * (as_all[:, lanes[g]] + aso_all[:, lanes[g]] - 2.0))
               for g in groups]
        sums = [headsum(jnp.concatenate([ysum[g], arg[g]], axis=0)) for g in groups]
        yc = [ysum[g] - sums[g][:CH] * (1.0 / HD) for g in groups]
        var = [headsum(yc[g] * yc[g]) * (1.0 / HD) for g in groups]
        for g in groups:
            out = (yc[g] * lax.rsqrt(var[g] + GN_EPS) * hp_ref[2:3, lanes[g]] + hp_ref[3:4, lanes[g]]
                   + sums[g][CH:] * vs[g])
            zg = zg_ref[:, lanes[g]]
            y_ref[:, lanes[g]] = (out * (zg * jax.nn.sigmoid(zg))).astype(y_ref.dtype)
    if want_state:
        @pl.when(c == n_chunks - 1)
        def _():
            sel = sel_ref[...]
            for g in groups:
                for t in range(z_s.shape[1]):
                    sf_ref[g * GW + t * LANES:g * GW + (t + 1) * LANES, :] = sum(
                        _dot_tn(p, sel) for p in _split3(z_new[g][t])[:2])


def _vmap(fn, *lists):
    return [fn(*args) for args in zip(*lists)]


def _rwkv_chunk(r, k, v, lw, asg, kp, bm, tri, cat_ref, lvl_ref, z, *, reverse):
    incl = cat_ref[0]
    strict = cat_ref[1]
    eye_c = incl - strict
    bm_b = bm.astype(BF)

    def headsum(x):
        return sum(_dot(p, bm_b) for p in _split3(x)[:2])

    lane_head = lax.broadcasted_iota(jnp.int32, (1, LANES), 1) // HD
    zero_tile = jnp.zeros((CH, LANES), BF)

    def bdm(x):
        xb = x.astype(BF)
        blocks = []
        for h in range(GH):
            j = h // HEADS_PER_TILE
            tile = xb[:, j * LANES:(j + 1) * LANES]
            keep = jnp.where(lane_head == h % HEADS_PER_TILE, tile, jnp.zeros_like(tile))
            blocks.append(jnp.concatenate([keep if t == j else zero_tile for t in range(GW // LANES)],
                                          axis=1))
        return jnp.concatenate(blocks, axis=0)

    def mm(a, b_bd):
        return _vmap(lambda x, y: _dot(x.astype(BF), y), a, b_bd)

    def stack2(a, b):
        return _vmap(lambda x, y: jnp.concatenate([x, y], axis=0), a, b)

    def top(a):
        return [x[:CH] for x in a]

    def bot(a):
        return [x[CH:] for x in a]

    kk = _vmap(lambda k_, p: k_ * p[0:1], k, kp)
    ss = _vmap(lambda x: headsum(x * x), kk)
    kk = _vmap(lambda x, s: x * lax.rsqrt(jnp.maximum(s, 1e-24)), kk, ss)
    kd = _vmap(lambda k_, a, p: k_ * (1.0 + (a - 1.0) * p[1:2]), k, asg, kp)
    bvec = _vmap(lambda x, a: x * a, kk, asg)

    cum = _vmap(lambda x: sum(_dot(tri, p) for p in _split3(x)[:2]), lw)
    tot = _vmap(lambda x: x[0:1] if reverse else x[CH - 1:CH], cum)
    e_pos = _vmap(jnp.exp, cum)
    e_neg = _vmap(lambda x: jnp.exp(-x), cum)
    e_rem = _vmap(lambda t, en: jnp.exp(t) * en, tot, e_neg)
    a_t = _vmap(lambda x, ep, l: -x * (ep * jnp.exp(-l)), kk, e_pos, lw)
    r_t = _vmap(lambda x, ep: x * ep, r, e_pos)
    ar = [x.astype(BF) for x in stack2(a_t, r_t)]
    bt_bd = _vmap(lambda x, en: bdm(x * en), bvec, e_neg)
    kt_bd = _vmap(lambda x, en: bdm(x * en), kd, e_neg)
    keep2 = jnp.concatenate([strict, incl], axis=0) > 0
    ab = [jnp.where(keep2, m, 0.0) for m in _vmap(_dot_nt, ar, bt_bd)]
    ak = [jnp.where(keep2, m, 0.0).astype(BF) for m in _vmap(_dot_nt, ar, kt_bd)]
    a_ab = top(ab)
    a_rb = [m.astype(BF) for m in bot(ab)]

    p = [n * lvl_ref[0] for n in a_ab]
    x = [eye_c + n for n in p]
    p = [n.astype(BF) for n in p]
    pw = [m.astype(BF) for m in mm(p, _vmap(bdm, p))]
    e = 2
    while e < INV_BASE:
        rhs = _vmap(bdm, pw)
        if 2 * e < INV_BASE:
            both = mm(stack2([m.astype(BF) for m in x], pw), rhs)
            x = _vmap(jnp.add, x, top(both))
            pw = [m.astype(BF) for m in bot(both)]
        else:
            x = _vmap(jnp.add, x, mm(x, rhs))
        e *= 2
    for lvl in range(1, lvl_ref.shape[0]):
        off = [(n * lvl_ref[lvl]).astype(BF) for n in a_ab]
        xb = [m.astype(BF) for m in x]
        x = _vmap(jnp.add, x, mm(xb, _vmap(bdm, mm(off, _vmap(bdm, xb)))))
    xb = [m.astype(BF) for m in x]

    vb = [m.astype(BF) for m in v]
    s_v = mm(ak, _vmap(bdm, vb))
    def z_rhs(zq):
        n_t = len(zq)
        zero = jnp.zeros((LANES, LANES), BF)
        return jnp.concatenate(
            [jnp.concatenate([zq[i].astype(BF) if i == t else zero for t in range(n_t)], axis=1)
             for i in range(n_t)], axis=0)

    s_z = _vmap(lambda a, zq: _dot(a, z_rhs(zq)), ar, z)
    u = mm(xb, _vmap(lambda a, b_: bdm(a + b_), top(s_z), top(s_v)))
    ub = [m.astype(BF) for m in u]
    y =_vmap(lambda a, b_, c_: a + b_ + c_, bot(s_z), bot(s_v), mm(a_rb, _vmap(bdm, ub)))
    row_id = lax.broadcasted_iota(jnp.int32, (LANES, 1), 0)
    lane_id = lax.broadcasted_iota(jnp.int32, (1, LANES), 1)
    same_head = (row_id // HD) == (lane_id // HD)
    diag = row_id == lane_id
    lhs = _vmap(lambda b_, k_, er: jnp.concatenate([b_ * er, k_ * er], axis=0).astype(BF), bvec, kd, e_rem)
    rhs = stack2(ub, vb)
    decay = _vmap(jnp.exp, tot)
    z_new = [[] for _ in z]
    for t in range(GW // LANES):
        cols = slice(t * LANES, (t + 1) * LANES)
        for g in range(len(z)):
            upd = jnp.where(same_head, _dot_tn(lhs[g][:, cols], rhs[g][:, cols]), 0.0)
            g_col = jnp.sum(jnp.where(diag, decay[g][:, cols], 0.0), axis=1, keepdims=True)
            z_new[g].append(z[g][t] * g_col + upd)
    return y, z_new


def _rwkv_scan(r, k, v, tw, ta, lora, jd, kp, consts, d, *, batch, seq_len, z0=None, want_state=False,
               finish=None):
    rows, e = r.shape
    ngs = min(SCAN_GROUPS, e // GW)
    sw = ngs * GW
    ng = e // sw
    nc = seq_len // CH
    reverse = d == 1
    zero_init = z0 is None
    rank = lora["w2"].shape[1]

    def cidx(c):
        return (nc - 1 - c) if reverse else c

    def lora_specs(dd, jdd):
        return [pl.BlockSpec((CH, rank), lambda b, g, c: (b * nc + cidx(c), dd)),
                pl.BlockSpec((None, rank, sw), lambda b, g, c: (jdd, 0, g)),
                pl.BlockSpec((None, 1, sw), lambda b, g, c: (jdd, 0, g))]

    tok = pl.BlockSpec((CH, sw), lambda b, g, c: (b * nc + cidx(c), g))
    w_specs, a_specs = lora_specs(d, jd), lora_specs(d, jd)
    in_specs = [tok, tok, tok, w_specs[0], a_specs[0], w_specs[1], w_specs[2], a_specs[1], a_specs[2],
                pl.BlockSpec((2, sw), lambda b, g, c: (0, g)),
                pl.BlockSpec((GW, GW), lambda b, g, c: (0, 0)),
                pl.BlockSpec((None, CH, CH), lambda b, g, c: (d, 0, 0)),
                pl.BlockSpec((None, 2, CH, GW), lambda b, g, c: (d, 0, 0, 0)),
                pl.BlockSpec(consts["lvl"].shape, lambda b, g, c: (0, 0, 0))]
    args = [r, k, v, tw, ta, lora["w2"], lora["w0"], lora["a2"], lora["a0"], kp, consts["bm"],
            consts["tri"], consts["cat"], consts["lvl"]]
    if finish is not None:
        y_other, zg, hp = finish
        in_specs += lora_specs(1 - d, jd + 1 - 2 * d) + [tok, tok, pl.BlockSpec((4, sw), lambda b, g, c: (0, g))]
        args += [ta, lora["a2"], lora["a0"], y_other, zg, hp]
    if want_state:
        in_specs.append(pl.BlockSpec((LANES, HD), lambda b, g, c: (0, 0)))
        args.append(consts["sel"])
    if not zero_init:
        in_specs.append(pl.BlockSpec((None, ngs, GW // LANES, LANES, LANES), lambda b, g, c: (b, g, 0, 0, 0)))
        args.append(z0)
    out_shape = [jax.ShapeDtypeStruct((rows, e), F32 if finish is None else BF)]
    out_specs = [tok]
    if want_state:
        out_shape.append(jax.ShapeDtypeStruct((batch, e, HD), F32))
        out_specs.append(pl.BlockSpec((None, sw, HD), lambda b, g, c: (b, g, 0)))
    res = pl.pallas_call(
        functools.partial(_rwkv_scan_body, reverse=reverse, zero_init=zero_init,
                          want_state=want_state, finish=finish is not None, n_chunks=nc),
        out_shape=out_shape,
        grid=(batch, ng, nc),
        in_specs=in_specs,
        out_specs=out_specs,
        scratch_shapes=[pltpu.VMEM((ngs, GW // LANES, LANES, LANES), F32)],
        compiler_params=_cparams("parallel", "parallel", "arbitrary"),
        name="rwkv_scan",
    )(*args)
    return res if want_state else (res[0], None)


def _silu(x):
    return x * jax.nn.sigmoid(x)


def _act_tanh(x):
    return jnp.tanh(x)


def _act_sigmoid(x):
    return jax.nn.sigmoid(x)


def _act_logdecay(x):
    return -jnp.exp(-0.5) * jax.nn.sigmoid(x)


def _act_vmix(acc, v_first, mix):
    return acc + (v_first - acc) * mix


def _trunk(x, mods, init_lru, init_rwkv, W, *, batch, seq_len, grid_mode, want_state):
    rows, d = x.shape
    e = W["lru_w_out"].shape[1]
    depth = W["norm_pre"].shape[0]
    tm_row = min(256, seq_len)
    tm = min(1024, seq_len)
    nb = seq_len // tm
    m_tiles = rows // tm
    rb = batch
    lru_tr = max(2048 // rb, 1) * rb if rb % SUBLANES == 0 else 2048
    lru_tr = min(lru_tr, rows)
    cond_rows = rows // mods[0].shape[0]
    tn_in = min(1024, 2 * e)
    fin_lru, fin_rwkv = [], []
    v_first = None

    for i in range(depth):
        mod = mods[i]
        j = i // 2
        if i % 2 == 0:
            h = _prenorm(x, W["norm_pre"], i, mod, tm=tm_row, rows_per_cond=cond_rows)
            if rb % SUBLANES == 0:
                xz = _mm(h, W["lru_w_in"], j, None, tm=tm, tn=tn_in, m_tiles=m_tiles,
                         o_shape=(seq_len, batch * 2 * e),
                         o_imap=lambda n, m: (m % nb, (m // nb) * (2 * e // tn_in) + n), out_dtype=BF,
                         name="lru_in")
                xz = xz.reshape(seq_len * batch, 2 * e)
                y0, hf0 = _lru_core(xz, W["lru"], j, 0, init_lru[j][0], rb, tr=lru_tr)
                y, hf1 = _lru_core(xz, W["lru"], j, 1, init_lru[j][1], rb, tr=lru_tr, y_prev=y0)
                y = y.reshape(seq_len, batch * e)
                m = _mm(y, W["lru_w_out"], j, None, tm=tm, tn=1024, m_tiles=m_tiles,
                        a_imap=lambda n, m: (m % nb, m // nb), name="lru_out")
            else:
                tm_plain = min(1024, rows)
                mk = dict(tm=tm_plain, m_tiles=rows // tm_plain)
                xz = _mm(h, W["lru_w_in"], j, None, tn=tn_in, out_dtype=BF, name="lru_in", **mk)
                xz = xz.reshape(batch, seq_len, 2 * e)
                tt = min(lru_tr // batch, seq_len)
                y0, hf0 = _lru_core_rows(xz, W["lru"], j, 0, init_lru[j][0], tt=tt)
                y, hf1 = _lru_core_rows(xz, W["lru"], j, 1, init_lru[j][1], tt=tt, y_prev=y0)
                m = _mm(y.reshape(rows, e), W["lru_w_out"], j, None, tn=1024, name="lru_out", **mk)
            fin_lru.append((hf0, hf1))
        else:
            h, xr, xw, xk, xv, xa = _shiftmix(
                x, W["norm_pre"], i, mod, W["rwkv_mu"][j], seq_len=seq_len, cond_rows=cond_rows, grid_mode=grid_mode,
                tm=seq_len if not grid_mode else min(512, seq_len))
            tm_plain = min(1024, rows)
            mk = dict(tm=tm_plain, m_tiles=rows // tm_plain)
            r = _mm(xr, W["rwkv_w_r"], j, None, tn=1024, name="rwkv_r", **mk)
            k = _mm(xk, W["rwkv_w_k"], j, None, tn=1024, name="rwkv_k", **mk)
            z = _mm(h, W["rwkv_w_g"], j, None, tn=1024, name="rwkv_g", **mk)
            if j == 0:
                v = _mm(xv, W["rwkv_w_v"], j, None, tn=1024, name="rwkv_v", **mk)
                v_first = v
            else:
                tv = _mm(xv, W["rwkv_v1"], j - 1, None, tn=128, out_dtype=BF, name="rwkv_v1", **mk)
                mix = _mm(tv, W["rwkv_v2"], j - 1, W["rwkv_v0"][j - 1], tn=1024, act=_act_sigmoid,
                          name="rwkv_v2", **mk)
                v = _mm(xv, W["rwkv_w_v"], j, None, tn=1024, extras=(v_first, mix), act=_act_vmix,
                        name="rwkv_v", **mk)
            tw = _mm(xw, W["rwkv_w1"], j, None, tn=256, act=_act_tanh, out_dtype=BF, name="rwkv_w1", **mk)
            ta = _mm(xa, W["rwkv_a1"], j, None, tn=256, out_dtype=BF, name="rwkv_a1", **mk)
            fins = []
            y_d = None
            for dd in range(2):
                z0 = None if init_rwkv is None else init_rwkv[j][dd]
                finish = None if dd == 0 else (y_d, z, W["rwkv_hp"][j])
                y_d, s_d = _rwkv_scan(r, k, v, tw, ta, W["rwkv_lora"], j * 2 + dd, W["rwkv_kp"][j],
                                      W["consts"], dd, batch=batch, seq_len=seq_len, z0=z0,
                                      want_state=want_state, finish=finish)
                fins.append(s_d)
            fin_rwkv.append(fins)
            m = _mm(y_d, W["rwkv_w_o"], j, None, tn=1024, name="rwkv_o", **mk)
        x = _postnorm(x, m, W["norm_post"], i, mod, tm=tm_row, rows_per_cond=cond_rows)
    return x, fin_lru, fin_rwkv


def _scan_consts():
    idx = jnp.arange(GW)
    bm = ((idx[:, None] // HD) == (idx[None, :] // HD)).astype(F32)
    t_r, t_c = jnp.arange(CH)[:, None], idx[None, :] % CH
    cats, tris = [], []
    ic = jnp.arange(CH)
    for d in range(2):
        ge = (t_r >= t_c) if d == 0 else (t_r <= t_c)
        gt = (t_r > t_c) if d == 0 else (t_r < t_c)
        cats.append(jnp.stack([ge.astype(F32), gt.astype(F32)]))
        tris.append(((ic[:, None] >= ic[None, :]) if d == 0 else (ic[:, None] <= ic[None, :])).astype(BF))
    sel = (idx[:, None] % HD == jnp.arange(HD)[None, :]).astype(BF)
    lvl = [t_r // INV_BASE == t_c // INV_BASE]
    size = INV_BASE
    while size < CH:
        lvl.append((t_r // (2 * size) == t_c // (2 * size)) & (t_r // size != t_c // size))
        size *= 2
    return dict(bm=bm, cat=jnp.stack(cats), tri=jnp.stack(tris), sel=sel,
                lvl=jnp.stack(lvl).astype(F32))


def kernel(x_prompt, x_sample, state_lru, state_rwkv, c, c_ctx, ada_w, ada_b, norm_pre, norm_post, lru_w_in, lru_conv_w, lru_conv_b, lru_gate_w, lru_gate_b, lru_lambda, lru_w_out, rwkv_mu, rwkv_w_r, rwkv_w_k, rwkv_w_v, rwkv_w_g, rwkv_w_o, rwkv_w0, rwkv_w1, rwkv_w2, rwkv_a0, rwkv_a1, rwkv_a2, rwkv_k_k, rwkv_k_a, rwkv_r_k, rwkv_ln_w, rwkv_ln_b, rwkv_v0, rwkv_v1, rwkv_v2):
    bp, tp, d = x_prompt.shape
    bs, ts, _ = x_sample.shape
    depth = ada_w.shape[0]
    n_lru, _, e2 = lru_w_in.shape
    e = e2 // 2
    n_rwkv = rwkv_w_r.shape[0]
    nblk = e // LB
    heads = e // HD
    ng = e // GW

    def merge_lead(w, k):
        return w.reshape((-1,) + w.shape[k:])

    lora_v = rwkv_v1.shape[-1]
    pad_v = (-lora_v) % 128
    W = dict(
        norm_pre=norm_pre.reshape(depth, 1, d), norm_post=norm_post.reshape(depth, 1, d),
        lru_w_in=lru_w_in, lru_w_out=lru_w_out,
        lru=dict(conv_w=lru_conv_w, conv_b=lru_conv_b.reshape(n_lru, 1, e),
                 gate_w=merge_lead(lru_gate_w, 2),
                 gate_b=merge_lead(lru_gate_b, 2),
                 lam=lru_lambda.reshape(n_lru * 2, 1, e)),
        rwkv_mu=rwkv_mu,
        rwkv_w_r=rwkv_w_r, rwkv_w_k=rwkv_w_k, rwkv_w_v=rwkv_w_v, rwkv_w_g=rwkv_w_g, rwkv_w_o=rwkv_w_o,
        rwkv_w1=jnp.concatenate([rwkv_w1[:, 0], rwkv_w1[:, 1]], axis=-1).astype(BF),
        rwkv_a1=jnp.concatenate([rwkv_a1[:, 0], rwkv_a1[:, 1]], axis=-1).astype(BF),
        rwkv_lora=dict(w2=merge_lead(rwkv_w2, 2).astype(BF), a2=merge_lead(rwkv_a2, 2).astype(BF),
                       w0=rwkv_w0.reshape(n_rwkv * 2, 1, e), a0=rwkv_a0.reshape(n_rwkv * 2, 1, e)),
        rwkv_v0=rwkv_v0.reshape(-1, 1, e),
        rwkv_v1=jnp.pad(rwkv_v1, ((0, 0), (0, 0), (0, pad_v))).astype(BF),
        rwkv_v2=jnp.pad(rwkv_v2, ((0, 0), (0, pad_v), (0, 0))).astype(BF),
        rwkv_kp=jnp.stack([rwkv_k_k, rwkv_k_a], axis=1),
        rwkv_hp=jnp.stack([rwkv_k_a, rwkv_r_k, rwkv_ln_w, rwkv_ln_b], axis=1),
        consts=_scan_consts(),
    )

    n_cond = 1 + bs
    cond = jnp.zeros((SUBLANES * pl.cdiv(n_cond, SUBLANES), d), F32)
    cond = cond.at[0].set(c_ctx).at[1:n_cond].set(c)
    mods_p, mods_s = [], []
    for i in range(depth):
        mod = _mm(cond, ada_w, i, ada_b[i][None], tm=cond.shape[0], tn=256, m_tiles=1,
                  a_pre=_silu, name="adaln")
        mod = mod.reshape(cond.shape[0], 3, d)
        mods_p.append(mod[0:1])
        mods_s.append(mod[1:n_cond])

    zero_lru = [(jnp.zeros((bp, e), F32), jnp.zeros((bp, e), F32))] * n_lru
    y_p, fin_lru, fin_rwkv = _trunk(x_prompt.reshape(bp * tp, d), mods_p, zero_lru, None, W,
                                    batch=bp, seq_len=tp, grid_mode=False, want_state=True)
    new_state_lru = jnp.stack([jnp.stack(f, axis=1) for f in fin_lru], axis=1)
    new_state_rwkv = jnp.stack(
        [jnp.stack([s.reshape(bp, heads, HD, HD) for s in f], axis=1) for f in fin_rwkv], axis=1)

    rep = max(SUBLANES // bs, 1)
    init_lru = [tuple(jnp.tile(state_lru[:, jj, dd], (rep, 1)) for dd in range(2)) for jj in range(n_lru)]
    eye_t = jnp.eye(HEADS_PER_TILE, dtype=F32)
    n_tiles = e // LANES
    st = state_rwkv.reshape(bs, n_rwkv, 2, n_tiles, HEADS_PER_TILE, HD, HD)
    z0 = jnp.einsum("bldthvk,hi->bldthkiv", st, eye_t).reshape(
        bs, n_rwkv, 2, ng, GW // LANES, LANES, LANES)
    init_rwkv = [[z0[:, jj, dd] for dd in range(2)] for jj in range(n_rwkv)]
    y_s, _, _ = _trunk(x_sample.reshape(bs * ts, d), mods_s, init_lru, init_rwkv, W,
                       batch=bs, seq_len=ts, grid_mode=True, want_state=False)
    return (y_p.reshape(bp, tp, d), y_s.reshape(bs, ts, d), new_state_lru, new_state_rwkv)
```

```python
import functools

import jax
import jax.numpy as jnp
from jax import lax
from jax.experimental import pallas as pl
from jax.experimental.pallas import tpu as pltpu

F32 = jnp.float32
BF = jnp.bfloat16

NORM_EPS = 1e-6
GN_EPS = 64e-5
LRU_C = 8.0
GRID_W = 64
CONV_W = 4
CONV_LEFT = 2

HD = 64
GH = 4
GW = GH * HD
CH = 64
INV_BASE = 16
SCAN_GROUPS = 16
LB = 256
SUBLANES = 8
LANES = 128
HEADS_PER_TILE = LANES // HD
VMEM_LIMIT = 56 * 1024 * 1024
MM_VMEM_BUDGET = 44 * 1024 * 1024


def _cparams(*sem):
    return pltpu.CompilerParams(dimension_semantics=sem, vmem_limit_bytes=VMEM_LIMIT)


def _dot(a, b):
    return jnp.dot(a, b, preferred_element_type=F32)


def _dot_nt(a, b):
    return lax.dot_general(a, b, (((1,), (1,)), ((), ())), preferred_element_type=F32)


def _dot_tn(a, b):
    return lax.dot_general(a, b, (((0,), (0,)), ((), ())), preferred_element_type=F32)


def _split3(x):
    h1 = x.astype(BF)
    r1 = x - h1.astype(F32)
    h2 = r1.astype(BF)
    h3 = (r1 - h2.astype(F32)).astype(BF)
    return h1, h2, h3


def _modnorm(x, g, mod):
    y = x * lax.rsqrt(jnp.mean(x * x, axis=-1, keepdims=True) + NORM_EPS) * g
    return y * (1.0 + mod[1:2]) + mod[0:1]


def _mm_body(*refs, act, a_pre, n_extra):
    a_ref, w_ref, b_ref = refs[:3]
    extras = refs[3:3 + n_extra]
    o_ref = refs[3 + n_extra]
    a = a_ref[...]
    if a_pre is not None:
        a = a_pre(a.astype(F32))
    acc = _dot(a.astype(BF), w_ref[...]) + b_ref[...]
    if act is not None:
        acc = act(acc, *[e[...] for e in extras])
    o_ref[...] = acc.astype(o_ref.dtype)


def _mm_tn(k, n, tm, tn_max, n_extra, out_itemsize):
    tn = min(tn_max, n)
    while tn > LANES:
        w_bytes = 2 * k * tn * 2
        io_bytes = 2 * tm * (k * 2 + tn * (out_itemsize + 4 * n_extra))
        if n % tn == 0 and w_bytes + io_bytes <= MM_VMEM_BUDGET:
            break
        tn //= 2
    return tn


def _mm(a, w3, li, bias, *, tm, tn, m_tiles, a_imap=None, o_shape=None, o_imap=None,
        extras=(), act=None, a_pre=None, out_dtype=F32, name="mm"):
    _, K, N = w3.shape
    if o_imap is None:
        tn = _mm_tn(K, N, tm, tn, len(extras), jnp.dtype(out_dtype).itemsize)
    tn = min(tn, N)
    assert N % tn == 0 and K % 128 == 0
    n_tiles = N // tn
    if a_imap is None:
        a_imap = lambda n, m: (m, 0)
    if o_imap is None:
        o_imap = lambda n, m: (m, n)
    if o_shape is None:
        o_shape = (m_tiles * tm, N)
    if bias is None:
        bias = jnp.zeros((1, N), F32)
    in_specs = [
        pl.BlockSpec((tm, K), a_imap),
        pl.BlockSpec((None, K, tn), lambda n, m: (li, 0, n)),
        pl.BlockSpec((1, tn), lambda n, m: (0, n)),
    ] + [pl.BlockSpec((tm, tn), lambda n, m: (m, n)) for _ in extras]
    return pl.pallas_call(
        functools.partial(_mm_body, act=act, a_pre=a_pre, n_extra=len(extras)),
        out_shape=jax.ShapeDtypeStruct(o_shape, out_dtype),
        grid=(n_tiles, m_tiles),
        in_specs=in_specs,
        out_specs=pl.BlockSpec((tm, tn), o_imap),
        compiler_params=_cparams("parallel", "parallel"),
        name=name,
    )(a, w3, bias, *extras)


def _prenorm_body(x_ref, g_ref, mod_ref, o_ref):
    o_ref[...] = _modnorm(x_ref[...], g_ref[...], mod_ref[0]).astype(o_ref.dtype)


def _prenorm(x, g3, layer, mod, *, tm, rows_per_cond):
    rows, d = x.shape
    return pl.pallas_call(
        _prenorm_body,
        out_shape=jax.ShapeDtypeStruct((rows, d), BF),
        grid=(rows // tm,),
        in_specs=[
            pl.BlockSpec((tm, d), lambda i: (i, 0)),
            pl.BlockSpec((None, 1, d), lambda i: (layer, 0, 0)),
            pl.BlockSpec((1, 3, d), lambda i: ((i * tm) // rows_per_cond, 0, 0)),
        ],
        out_specs=pl.BlockSpec((tm, d), lambda i: (i, 0)),
        compiler_params=_cparams("parallel"),
        name="prenorm",
    )(x, g3, mod)


def _postnorm_body(x_ref, m_ref, g_ref, mod_ref, o_ref):
    m = m_ref[...]
    y = m * lax.rsqrt(jnp.mean(m * m, axis=-1, keepdims=True) + NORM_EPS) * g_ref[...]
    o_ref[...] = x_ref[...] + mod_ref[0][2:3] * y


def _postnorm(x, m, g3, layer, mod, *, tm, rows_per_cond):
    rows, d = x.shape
    return pl.pallas_call(
        _postnorm_body,
        out_shape=jax.ShapeDtypeStruct((rows, d), F32),
        grid=(rows // tm,),
        in_specs=[
            pl.BlockSpec((tm, d), lambda i: (i, 0)),
            pl.BlockSpec((tm, d), lambda i: (i, 0)),
            pl.BlockSpec((None, 1, d), lambda i: (layer, 0, 0)),
            pl.BlockSpec((1, 3, d), lambda i: ((i * tm) // rows_per_cond, 0, 0)),
        ],
        out_specs=pl.BlockSpec((tm, d), lambda i: (i, 0)),
        compiler_params=_cparams("parallel"),
        name="postnorm",
    )(x, m, g3, mod)


def _write_mixes(h, shifted, mu_ref, out_refs):
    xx = shifted - h
    out_refs[0][...] = h.astype(BF)
    for m in range(5):
        out_refs[m + 1][...] = (h + xx * mu_ref[m:m + 1]).astype(BF)


def _shiftmix_seq_body(x_ref, g_ref, mod_ref, mu_ref, *out_refs):
    h = _modnorm(x_ref[...], g_ref[...], mod_ref[0])
    t_len, d = h.shape
    half = d // 2
    t = lax.broadcasted_iota(jnp.int32, (t_len, 1), 0)
    prev = jnp.where(t == 0, 0.0, pltpu.roll(h[:, :half], 1, 0))
    nxt = jnp.where(t == t_len - 1, 0.0, pltpu.roll(h[:, half:], t_len - 1, 0))
    _write_mixes(h, jnp.concatenate([prev, nxt], axis=1), mu_ref, out_refs)


def _shiftmix_grid_body(x_ref, up_ref, dn_ref, g_ref, mod_ref, mu_ref, *out_refs, tiles_per_batch):
    ti = pl.program_id(0) % tiles_per_batch
    g = g_ref[...]
    mod = mod_ref[0]
    h = _modnorm(x_ref[...], g, mod)
    tr, d = h.shape
    q = d // 4
    h_up = jnp.where(ti > 0, _modnorm(up_ref[...], g, mod)[:, 2 * q:3 * q], 0.0)
    h_dn = jnp.where(ti < tiles_per_batch - 1, _modnorm(dn_ref[...], g, mod)[:, 3 * q:], 0.0)
    col = lax.broadcasted_iota(jnp.int32, (tr, 1), 0) % GRID_W
    left = jnp.where(col == 0, 0.0, pltpu.roll(h[:, :q], 1, 0))
    right = jnp.where(col == GRID_W - 1, 0.0, pltpu.roll(h[:, q:2 * q], tr - 1, 0))
    up = jnp.concatenate([h_up, h[:tr - GRID_W, 2 * q:3 * q]], axis=0)
    down = jnp.concatenate([h[GRID_W:, 3 * q:], h_dn], axis=0)
    _write_mixes(h, jnp.concatenate([left, right, up, down], axis=1), mu_ref, out_refs)


def _shiftmix(x, g3, layer, mod, mu, *, seq_len, cond_rows, grid_mode, tm):
    rows, d = x.shape
    outs = [jax.ShapeDtypeStruct((rows, d), BF)] * 6
    common = [
        pl.BlockSpec((None, 1, d), lambda i: (layer, 0, 0)),
        pl.BlockSpec((1, 3, d), lambda i: ((i * tm) // cond_rows, 0, 0)),
        pl.BlockSpec((5, d), lambda i: (0, 0)),
    ]
    out_specs = [pl.BlockSpec((tm, d), lambda i: (i, 0))] * 6
    if not grid_mode:
        assert tm == seq_len
        return pl.pallas_call(
            _shiftmix_seq_body, out_shape=outs, grid=(rows // tm,),
            in_specs=[pl.BlockSpec((tm, d), lambda i: (i, 0))] + common,
            out_specs=out_specs, compiler_params=_cparams("parallel"), name="shiftmix_seq",
        )(x, g3, mod, mu)
    per = tm // GRID_W
    n_halo = rows // GRID_W
    return pl.pallas_call(
        functools.partial(_shiftmix_grid_body, tiles_per_batch=seq_len // tm),
        out_shape=outs, grid=(rows // tm,),
        in_specs=[
            pl.BlockSpec((tm, d), lambda i: (i, 0)),
            pl.BlockSpec((GRID_W, d), lambda i: (jnp.maximum(i * per - 1, 0), 0)),
            pl.BlockSpec((GRID_W, d), lambda i: (jnp.minimum((i + 1) * per, n_halo - 1), 0)),
        ] + common,
        out_specs=out_specs, compiler_params=_cparams("parallel"), name="shiftmix_grid",
    )(x, x, x, g3, mod, mu)


def _lru_body(*refs, rb, reverse, combine, n_tiles):
    if combine:
        (x_ref, xp_ref, xn_ref, cw_ref, cb_ref, gw_ref, gb_ref, lam_ref, h0_ref, yp_ref, z_ref,
         y_ref, hf_ref, xs, a_s, u_s, hc) = refs
    else:
        (x_ref, xp_ref, xn_ref, cw_ref, cb_ref, gw_ref, gb_ref, lam_ref, h0_ref,
         y_ref, hf_ref, xs, a_s, u_s, hc) = refs
    tr = x_ref.shape[0]
    hb = xp_ref.shape[0]
    step = pl.program_id(1)
    t_idx = (n_tiles - 1 - step) if reverse else step

    @pl.when(step == 0)
    def _():
        hc[...] = h0_ref[...]

    xs[hb:hb + tr, :] = x_ref[...].astype(F32)
    xs[0:hb, :] = jnp.where(t_idx > 0, xp_ref[...].astype(F32), 0.0)
    xs[hb + tr:hb + tr + hb, :] = jnp.where(t_idx < n_tiles - 1, xn_ref[...].astype(F32), 0.0)
    xc = cb_ref[...]
    for tap in range(CONV_W):
        off = hb + (tap - CONV_LEFT) * rb
        xc = xc + xs[off:off + tr, :] * cw_ref[tap:tap + 1]
    h = _lru_gates_scan(xc, gw_ref, gb_ref, lam_ref, a_s, u_s, hc[...], rb=rb, reverse=reverse)
    hc[...] = h
    hf_ref[...] = h
    if combine:
        z = z_ref[...].astype(F32)
        y_ref[...] = ((yp_ref[...] + u_s[...]) * (z * jax.nn.sigmoid(z))).astype(y_ref.dtype)
    else:
        y_ref[...] = u_s[...]


def _lru_gates_scan(xc, gw_ref, gb_ref, lam_ref, a_s, u_s, h, *, rb, reverse):
    tr = xc.shape[0]
    xb = xc.astype(BF)

    def gate(n):
        return 0.5 * jnp.tanh(0.5 * (_dot(xb, gw_ref[n].astype(BF)) + gb_ref[n:n + 1])) + 0.5

    r = gate(0)
    i = gate(1)
    log_a = r * (-LRU_C * jax.nn.softplus(-lam_ref[...]))
    a_s[...] = jnp.exp(log_a)
    th = jnp.tanh(log_a)
    u_s[...] = jnp.sqrt(-2.0 * th / (1.0 - th)) * (i * xc)

    if rb % SUBLANES == 0:
        n_steps = tr // rb

        def body(s, h):
            row = pl.multiple_of(((n_steps - 1 - s) if reverse else s) * rb, rb)
            h = a_s[pl.ds(row, rb), :] * h + u_s[pl.ds(row, rb), :]
            u_s[pl.ds(row, rb), :] = h
            return h

        h = lax.fori_loop(0, n_steps, body, h, unroll=4)
    else:
        assert 2 * rb == SUBLANES
        n_steps = tr // SUBLANES
        first = lax.broadcasted_iota(jnp.int32, (SUBLANES, 1), 0) < rb

        def body(s, h):
            row = pl.multiple_of(((n_steps - 1 - s) if reverse else s) * SUBLANES, SUBLANES)
            a = a_s[pl.ds(row, SUBLANES), :]
            u = u_s[pl.ds(row, SUBLANES), :]
            h1 = a * h + u
            h2 = a * pltpu.roll(h1, rb, 0) + u
            u_s[pl.ds(row, SUBLANES), :] = jnp.where(first != reverse, h1, h2)
            return pltpu.roll(h2, rb, 0)

        h = lax.fori_loop(0, n_steps, body, h, unroll=4)
    return h


def _lru_rows_body(*refs, nb, reverse, combine, n_tiles):
    if combine:
        (x_ref, xp_ref, xn_ref, cw_ref, cb_ref, gw_ref, gb_ref, lam_ref, h0_ref, yp_ref, z_ref,
         y_ref, hf_ref, xs, a_s, u_s, hc, zs) = refs
    else:
        (x_ref, xp_ref, xn_ref, cw_ref, cb_ref, gw_ref, gb_ref, lam_ref, h0_ref,
         y_ref, hf_ref, xs, a_s, u_s, hc) = refs
    tt = x_ref.shape[1]
    ht = xp_ref.shape[1]
    tr, hb = tt * nb, ht * nb
    tiles = [slice(t * LANES, (t + 1) * LANES) for t in range(LB // LANES)]
    step = pl.program_id(1)
    t_idx = (n_tiles - 1 - step) if reverse else step

    @pl.when(step == 0)
    def _():
        hc[...] = h0_ref[...]

    for b in range(nb):
        for t, ln in enumerate(tiles):
            xs[t, pl.ds(hb + b, tt, stride=nb), :] = x_ref[b, :, ln].astype(F32)
            xs[t, pl.ds(b, ht, stride=nb), :] = jnp.where(t_idx > 0, xp_ref[b, :, ln].astype(F32), 0.0)
            xs[t, pl.ds(hb + tr + b, ht, stride=nb), :] = jnp.where(
                t_idx < n_tiles - 1, xn_ref[b, :, ln].astype(F32), 0.0)
    parts = []
    for t, ln in enumerate(tiles):
        acc = cb_ref[:, ln]
        for tap in range(CONV_W):
            off = hb + (tap - CONV_LEFT) * nb
            acc = acc + xs[t, off:off + tr, :] * cw_ref[tap:tap + 1, ln]
        parts.append(acc)
    xc = jnp.concatenate(parts, axis=1)
    h = _lru_gates_scan(xc, gw_ref, gb_ref, lam_ref, a_s, u_s, hc[...], rb=nb, reverse=reverse)
    hc[...] = h
    hf_ref[...] = h
    if not combine:
        y_ref[...] = u_s[...]
        return
    for b in range(nb):
        for t, ln in enumerate(tiles):
            zs[t, pl.ds(b, tt, stride=nb), :] = z_ref[b, :, ln].astype(F32)
    z = jnp.concatenate([zs[t] for t in range(len(tiles))], axis=1)
    res = (yp_ref[...] + u_s[...]) * (z * jax.nn.sigmoid(z))
    for t, ln in enumerate(tiles):
        zs[t] = res[:, ln]
    for b in range(nb):
        for t, ln in enumerate(tiles):
            y_ref[b, :, ln] = zs[t, pl.ds(b, tt, stride=nb), :].astype(y_ref.dtype)


def _lru_core_rows(xz3, lru, j, d, h0, *, tt, y_prev=None):
    nb, seq_len, e2 = xz3.shape
    e = e2 // 2
    nblk = e // LB
    n_tiles = seq_len // tt
    ht = 2 * SUBLANES
    per = tt // ht
    n_halo = seq_len // ht
    tr, hb = tt * nb, ht * nb
    hr = h0.shape[0]
    reverse = d == 1
    combine = y_prev is not None

    def tidx(s):
        return (n_tiles - 1 - s) if reverse else s

    in_specs = [
        pl.BlockSpec((nb, tt, LB), lambda c, s: (0, tidx(s), c)),
        pl.BlockSpec((nb, ht, LB), lambda c, s: (0, jnp.maximum(tidx(s) * per - 1, 0), c)),
        pl.BlockSpec((nb, ht, LB), lambda c, s: (0, jnp.minimum((tidx(s) + 1) * per, n_halo - 1), c)),
        pl.BlockSpec((None, CONV_W, LB), lambda c, s: (j, 0, c)),
        pl.BlockSpec((None, 1, LB), lambda c, s: (j, 0, c)),
        pl.BlockSpec((None, 2, None, LB, LB), lambda c, s: (j * 2 + d, 0, c, 0, 0)),
        pl.BlockSpec((None, 2, LB), lambda c, s: (j * 2 + d, 0, c)),
        pl.BlockSpec((None, 1, LB), lambda c, s: (j * 2 + d, 0, c)),
        pl.BlockSpec((hr, LB), lambda c, s: (0, c)),
    ]
    args = [xz3, xz3, xz3, lru["conv_w"], lru["conv_b"], lru["gate_w"], lru["gate_b"], lru["lam"], h0]
    lt = LB // LANES
    scratch = [pltpu.VMEM((lt, tr + 2 * hb, LANES), F32), pltpu.VMEM((tr, LB), F32),
               pltpu.VMEM((tr, LB), F32), pltpu.VMEM((hr, LB), F32)]
    if combine:
        in_specs += [
            pl.BlockSpec((tr, LB), lambda c, s: (tidx(s), c)),
            pl.BlockSpec((nb, tt, LB), lambda c, s: (0, tidx(s), nblk + c)),
        ]
        args += [y_prev, xz3]
        scratch.append(pltpu.VMEM((lt, tr, LANES), F32))
        y_shape = jax.ShapeDtypeStruct((nb, seq_len, e), BF)
        y_spec = pl.BlockSpec((nb, tt, LB), lambda c, s: (0, tidx(s), c))
    else:
        y_shape = jax.ShapeDtypeStruct((seq_len * nb, e), F32)
        y_spec = pl.BlockSpec((tr, LB), lambda c, s: (tidx(s), c))
    return pl.pallas_call(
        functools.partial(_lru_rows_body, nb=nb, reverse=reverse, combine=combine, n_tiles=n_tiles),
        out_shape=[y_shape, jax.ShapeDtypeStruct((hr, e), F32)],
        grid=(nblk, n_tiles),
        in_specs=in_specs,
        out_specs=[y_spec, pl.BlockSpec((hr, LB), lambda c, s: (0, c))],
        scratch_shapes=scratch,
        compiler_params=_cparams("parallel", "arbitrary"),
        name="lru_core_rows",
    )(*args)


def _lru_core(xz, lru, j, d, h0, rb, *, tr, y_prev=None):
    rows, e2 = xz.shape
    e = e2 // 2
    nblk = e // LB
    n_tiles = rows // tr
    hb = max(2 * rb, 2 * SUBLANES)
    per = tr // hb
    n_halo = rows // hb
    hr = h0.shape[0]
    reverse = d == 1
    combine = y_prev is not None

    def tidx(s):
        return (n_tiles - 1 - s) if reverse else s

    in_specs = [
        pl.BlockSpec((tr, LB), lambda c, s: (tidx(s), c)),
        pl.BlockSpec((hb, LB), lambda c, s: (jnp.maximum(tidx(s) * per - 1, 0), c)),
        pl.BlockSpec((hb, LB), lambda c, s: (jnp.minimum((tidx(s) + 1) * per, n_halo - 1), c)),
        pl.BlockSpec((None, CONV_W, LB), lambda c, s: (j, 0, c)),
        pl.BlockSpec((None, 1, LB), lambda c, s: (j, 0, c)),
        pl.BlockSpec((None, 2, None, LB, LB), lambda c, s: (j * 2 + d, 0, c, 0, 0)),
        pl.BlockSpec((None, 2, LB), lambda c, s: (j * 2 + d, 0, c)),
        pl.BlockSpec((None, 1, LB), lambda c, s: (j * 2 + d, 0, c)),
        pl.BlockSpec((hr, LB), lambda c, s: (0, c)),
    ]
    args = [xz, xz, xz, lru["conv_w"], lru["conv_b"], lru["gate_w"], lru["gate_b"], lru["lam"], h0]
    if combine:
        in_specs += [
            pl.BlockSpec((tr, LB), lambda c, s: (tidx(s), c)),
            pl.BlockSpec((tr, LB), lambda c, s: (tidx(s), nblk + c)),
        ]
        args += [y_prev, xz]
    return pl.pallas_call(
        functools.partial(_lru_body, rb=rb, reverse=reverse, combine=combine, n_tiles=n_tiles),
        out_shape=[jax.ShapeDtypeStruct((rows, e), BF if combine else F32),
                   jax.ShapeDtypeStruct((hr, e), F32)],
        grid=(nblk, n_tiles),
        in_specs=in_specs,
        out_specs=[pl.BlockSpec((tr, LB), lambda c, s: (tidx(s), c)),
                   pl.BlockSpec((hr, LB), lambda c, s: (0, c))],
        scratch_shapes=[pltpu.VMEM((tr + 2 * hb, LB), F32), pltpu.VMEM((tr, LB), F32),
                        pltpu.VMEM((tr, LB), F32), pltpu.VMEM((hr, LB), F32)],
        compiler_params=_cparams("parallel", "arbitrary"),
        name="lru_core",
    )(*args)


def _rwkv_scan_body(*refs, reverse, zero_init, want_state, finish, n_chunks):
    it = iter(refs)
    (r_ref, k_ref, v_ref, tw_ref, ta_ref, w2_ref, w0_ref, a2_ref, a0_ref, kp_ref, bm_ref,
     tri_ref, cat_ref, lvl_ref) = (next(it) for _ in range(14))
    if finish:
        tao_ref, a2o_ref, a0o_ref, yo_ref, zg_ref, hp_ref = (next(it) for _ in range(6))
    sel_ref = next(it) if want_state else None
    z0_ref = None if zero_init else next(it)
    y_ref = next(it)
    sf_ref = next(it) if want_state else None
    z_s = next(it)
    c = pl.program_id(2)

    @pl.when(c == 0)
    def _():
        z_s[...] = jnp.zeros_like(z_s) if zero_init else z0_ref[...]

    lw_all = _act_logdecay(_dot(tw_ref[...], w2_ref[...]) + w0_ref[...])
    as_all = jax.nn.sigmoid(_dot(ta_ref[...], a2_ref[...]) + a0_ref[...])

    groups = range(z_s.shape[0])
    lanes = [slice(g * GW, (g + 1) * GW) for g in groups]
    bm = bm_ref[...]
    rs = [r_ref[:, l] for l in lanes]
    ks = [k_ref[:, l] for l in lanes]
    vs = [v_ref[:, l] for l in lanes]
    ys, z_new = _rwkv_chunk(
        rs, ks, vs, [lw_all[:, l] for l in lanes], [as_all[:, l] for l in lanes],
        [kp_ref[:, l] for l in lanes], bm, tri_ref[...], cat_ref, lvl_ref,
        [[z_s[g, t] for t in range(z_s.shape[1])] for g in groups], reverse=reverse)
    for g in groups:
        for t in range(z_s.shape[1]):
            z_s[g, t] = z_new[g][t]
    if not finish:
        for g in groups:
            y_ref[:, lanes[g]] = ys[g]
    else:
        bm_b = bm.astype(BF)
        aso_all = jax.nn.sigmoid(_dot(tao_ref[...], a2o_ref[...]) + a0o_ref[...])

        def headsum(x):
            return sum(_dot(p, bm_b) for p in _split3(x)[:2])

        ysum = [ys[g] + yo_ref[:, lanes[g]] for g in groups]
        arg = [rs[g] * ks[g] * hp_ref[1:2, lanes[g]]
               * (2.0 + hp_ref[0:1, lanes[g]] * (as_all[:, lanes[g]] + aso_all[:, lanes[g]] - 2.0))
               for g in groups]
        sums = [headsum(jnp.concatenate([ysum[g], arg[g]], axis=0)) for g in groups]
        yc = [ysum[g] - sums[g][:CH] * (1.0 / HD) for g in groups]
        var = [headsum(yc[g] * yc[g]) * (1.0 / HD) for g in groups]
        for g in groups:
            out = (yc[g] * lax.rsqrt(var[g] + GN_EPS) * hp_ref[2:3, lanes[g]] + hp_ref[3:4, lanes[g]]
                   + sums[g][CH:] * vs[g])
            zg = zg_ref[:, lanes[g]]
            y_ref[:, lanes[g]] = (out * (zg * jax.nn.sigmoid(zg))).astype(y_ref.dtype)
    if want_state:
        @pl.when(c == n_chunks - 1)
        def _():
            sel = sel_ref[...]
            for g in groups:
                for t in range(z_s.shape[1]):
                    sf_ref[g * GW + t * LANES:g * GW + (t + 1) * LANES, :] = sum(
                        _dot_tn(p, sel) for p in _split3(z_new[g][t])[:2])


def _vmap(fn, *lists):
    return [fn(*args) for args in zip(*lists)]


def _rwkv_chunk(r, k, v, lw, asg, kp, bm, tri, cat_ref, lvl_ref, z, *, reverse):
    incl = cat_ref[0]
    strict = cat_ref[1]
    eye_c = incl - strict
    bm_b = bm.astype(BF)

    def headsum(x):
        return sum(_dot(p, bm_b) for p in _split3(x)[:2])

    lane_head = lax.broadcasted_iota(jnp.int32, (1, LANES), 1) // HD
    zero_tile = jnp.zeros((CH, LANES), BF)

    def bdm(x):
        xb = x.astype(BF)
        blocks = []
        for h in range(GH):
            j = h // HEADS_PER_TILE
            tile = xb[:, j * LANES:(j + 1) * LANES]
            keep = jnp.where(lane_head == h % HEADS_PER_TILE, tile, jnp.zeros_like(tile))
            blocks.append(jnp.concatenate([keep if t == j else zero_tile for t in range(GW // LANES)],
                                          axis=1))
        return jnp.concatenate(blocks, axis=0)

    def mm(a, b_bd):
        return _vmap(lambda x, y: _dot(x.astype(BF), y), a, b_bd)

    def stack2(a, b):
        return _vmap(lambda x, y: jnp.concatenate([x, y], axis=0), a, b)

    def top(a):
        return [x[:CH] for x in a]

    def bot(a):
        return [x[CH:] for x in a]

    kk = _vmap(lambda k_, p: k_ * p[0:1], k, kp)
    ss = _vmap(lambda x: headsum(x * x), kk)
    kk = _vmap(lambda x, s: x * lax.rsqrt(jnp.maximum(s, 1e-24)), kk, ss)
    kd = _vmap(lambda k_, a, p: k_ * (1.0 + (a - 1.0) * p[1:2]), k, asg, kp)
    bvec = _vmap(lambda x, a: x * a, kk, asg)

    cum = _vmap(lambda x: sum(_dot(tri, p) for p in _split3(x)[:2]), lw)
    tot = _vmap(lambda x: x[0:1] if reverse else x[CH - 1:CH], cum)
    e_pos = _vmap(jnp.exp, cum)
    e_neg = _vmap(lambda x: jnp.exp(-x), cum)
    e_rem = _vmap(lambda t, en: jnp.exp(t) * en, tot, e_neg)
    a_t = _vmap(lambda x, ep, l: -x * (ep * jnp.exp(-l)), kk, e_pos, lw)
    r_t = _vmap(lambda x, ep: x * ep, r, e_pos)
    ar = [x.astype(BF) for x in stack2(a_t, r_t)]
    bt_bd = _vmap(lambda x, en: bdm(x * en), bvec, e_neg)
    kt_bd = _vmap(lambda x, en: bdm(x * en), kd, e_neg)
    keep2 = jnp.concatenate([strict, incl], axis=0) > 0
    ab = [jnp.where(keep2, m, 0.0) for m in _vmap(_dot_nt, ar, bt_bd)]
    ak = [jnp.where(keep2, m, 0.0).astype(BF) for m in _vmap(_dot_nt, ar, kt_bd)]
    a_ab = top(ab)
    a_rb = [m.astype(BF) for m in bot(ab)]

    p = [n * lvl_ref[0] for n in a_ab]
    x = [eye_c + n for n in p]
    p = [n.astype(BF) for n in p]
    pw = [m.astype(BF) for m in mm(p, _vmap(bdm, p))]
    e = 2
    while e < INV_BASE:
        rhs = _vmap(bdm, pw)
        if 2 * e < INV_BASE:
            both = mm(stack2([m.astype(BF) for m in x], pw), rhs)
            x = _vmap(jnp.add, x, top(both))
            pw = [m.astype(BF) for m in bot(both)]
        else:
            x = _vmap(jnp.add, x, mm(x, rhs))
        e *= 2
    for lvl in range(1, lvl_ref.shape[0]):
        off = [(n * lvl_ref[lvl]).astype(BF) for n in a_ab]
        xb = [m.astype(BF) for m in x]
        x = _vmap(jnp.add, x, mm(xb, _vmap(bdm, mm(off, _vmap(bdm, xb)))))
    xb = [m.astype(BF) for m in x]

    vb = [m.astype(BF) for m in v]
    s_v = mm(ak, _vmap(bdm, vb))
    def z_rhs(zq):
        n_t = len(zq)
        zero = jnp.zeros((LANES, LANES), BF)
        return jnp.concatenate(
            [jnp.concatenate([zq[i].astype(BF) if i == t else zero for t in range(n_t)], axis=1)
             for i in range(n_t)], axis=0)

    s_z = _vmap(lambda a, zq: _dot(a, z_rhs(zq)), ar, z)
    u = mm(xb, _vmap(lambda a, b_: bdm(a + b_), top(s_z), top(s_v)))
    ub = [m.astype(BF) for m in u]
    y =_vmap(lambda a, b_, c_: a + b_ + c_, bot(s_z), bot(s_v), mm(a_rb, _vmap(bdm, ub)))
    row_id = lax.broadcasted_iota(jnp.int32, (LANES, 1), 0)
    lane_id = lax.broadcasted_iota(jnp.int32, (1, LANES), 1)
    same_head = (row_id // HD) == (lane_id // HD)
    diag = row_id == lane_id
    lhs = _vmap(lambda b_, k_, er: jnp.concatenate([b_ * er, k_ * er], axis=0).astype(BF), bvec, kd, e_rem)
    rhs = stack2(ub, vb)
    decay = _vmap(jnp.exp, tot)
    z_new = [[] for _ in z]
    for t in range(GW // LANES):
        cols = slice(t * LANES, (t + 1) * LANES)
        for g in range(len(z)):
            upd = jnp.where(same_head, _dot_tn(lhs[g][:, cols], rhs[g][:, cols]), 0.0)
            g_col = jnp.sum(jnp.where(diag, decay[g][:, cols], 0.0), axis=1, keepdims=True)
            z_new[g].append(z[g][t] * g_col + upd)
    return y, z_new


def _rwkv_scan(r, k, v, tw, ta, lora, jd, kp, consts, d, *, batch, seq_len, z0=None, want_state=False,
               finish=None):
    rows, e = r.shape
    ngs = min(SCAN_GROUPS, e // GW)
    sw = ngs * GW
    ng = e // sw
    nc = seq_len // CH
    reverse = d == 1
    zero_init = z0 is None
    rank = lora["w2"].shape[1]

    def cidx(c):
        return (nc - 1 - c) if reverse else c

    def lora_specs(dd, jdd):
        return [pl.BlockSpec((CH, rank), lambda b, g, c: (b * nc + cidx(c), dd)),
                pl.BlockSpec((None, rank, sw), lambda b, g, c: (jdd, 0, g)),
                pl.BlockSpec((None, 1, sw), lambda b, g, c: (jdd, 0, g))]

    tok = pl.BlockSpec((CH, sw), lambda b, g, c: (b * nc + cidx(c), g))
    w_specs, a_specs = lora_specs(d, jd), lora_specs(d, jd)
    in_specs = [tok, tok, tok, w_specs[0], a_specs[0], w_specs[1], w_specs[2], a_specs[1], a_specs[2],
                pl.BlockSpec((2, sw), lambda b, g, c: (0, g)),
                pl.BlockSpec((GW, GW), lambda b, g, c: (0, 0)),
                pl.BlockSpec((None, CH, CH), lambda b, g, c: (d, 0, 0)),
                pl.BlockSpec((None, 2, CH, GW), lambda b, g, c: (d, 0, 0, 0)),
                pl.BlockSpec(consts["lvl"].shape, lambda b, g, c: (0, 0, 0))]
    args = [r, k, v, tw, ta, lora["w2"], lora["w0"], lora["a2"], lora["a0"], kp, consts["bm"],
            consts["tri"], consts["cat"], consts["lvl"]]
    if finish is not None:
        y_other, zg, hp = finish
        in_specs += lora_specs(1 - d, jd + 1 - 2 * d) + [tok, tok, pl.BlockSpec((4, sw), lambda b, g, c: (0, g))]
        args += [ta, lora["a2"], lora["a0"], y_other, zg, hp]
    if want_state:
        in_specs.append(pl.BlockSpec((LANES, HD), lambda b, g, c: (0, 0)))
        args.append(consts["sel"])
    if not zero_init:
        in_specs.append(pl.BlockSpec((None, ngs, GW // LANES, LANES, LANES), lambda b, g, c: (b, g, 0, 0, 0)))
        args.append(z0)
    out_shape = [jax.ShapeDtypeStruct((rows, e), F32 if finish is None else BF)]
    out_specs = [tok]
    if want_state:
        out_shape.append(jax.ShapeDtypeStruct((batch, e, HD), F32))
        out_specs.append(pl.BlockSpec((None, sw, HD), lambda b, g, c: (b, g, 0)))
    res = pl.pallas_call(
        functools.partial(_rwkv_scan_body, reverse=reverse, zero_init=zero_init,
                          want_state=want_state, finish=finish is not None, n_chunks=nc),
        out_shape=out_shape,
        grid=(batch, ng, nc),
        in_specs=in_specs,
        out_specs=out_specs,
        scratch_shapes=[pltpu.VMEM((ngs, GW // LANES, LANES, LANES), F32)],
        compiler_params=_cparams("parallel", "parallel", "arbitrary"),
        name="rwkv_scan",
    )(*args)
    return res if want_state else (res[0], None)


def _silu(x):
    return x * jax.nn.sigmoid(x)


def _act_tanh(x):
    return jnp.tanh(x)


def _act_sigmoid(x):
    return jax.nn.sigmoid(x)


def _act_logdecay(x):
    return -jnp.exp(-0.5) * jax.nn.sigmoid(x)


def _act_vmix(acc, v_first, mix):
    return acc + (v_first - acc) * mix


def _trunk(x, mods, init_lru, init_rwkv, W, *, batch, seq_len, grid_mode, want_state):
    rows, d = x.shape
    e = W["lru_w_out"].shape[1]
    depth = W["norm_pre"].shape[0]
    tm_row = min(256, seq_len)
    tm = min(1024, seq_len)
    nb = seq_len // tm
    m_tiles = rows // tm
    rb = batch
    lru_tr = max(2048 // rb, 1) * rb if rb % SUBLANES == 0 else 2048
    lru_tr = min(lru_tr, rows)
    cond_rows = rows // mods[0].shape[0]
    tn_in = min(1024, 2 * e)
    fin_lru, fin_rwkv = [], []
    v_first = None

    for i in range(depth):
        mod = mods[i]
        j = i // 2
        if i % 2 == 0:
            h = _prenorm(x, W["norm_pre"], i, mod, tm=tm_row, rows_per_cond=cond_rows)
            if rb % SUBLANES == 0 and rb > LB:
                xz = _mm(h, W["lru_w_in"], j, None, tm=tm, tn=tn_in, m_tiles=m_tiles,
                         o_shape=(seq_len, batch * 2 * e),
                         o_imap=lambda n, m: (m % nb, (m // nb) * (2 * e // tn_in) + n), out_dtype=BF,
                         name="lru_in")
                xz = xz.reshape(seq_len * batch, 2 * e)
                y0, hf0 = _lru_core(xz, W["lru"], j, 0, init_lru[j][0], rb, tr=lru_tr)
                y, hf1 = _lru_core(xz, W["lru"], j, 1, init_lru[j][1], rb, tr=lru_tr, y_prev=y0)
                y = y.reshape(seq_len, batch * e)
                m = _mm(y, W["lru_w_out"], j, None, tm=tm, tn=1024, m_tiles=m_tiles,
                        a_imap=lambda n, m: (m % nb, m // nb), name="lru_out")
            else:
                tm_plain = min(1024, rows)
                mk = dict(tm=tm_plain, m_tiles=rows // tm_plain)
                xz = _mm(h, W["lru_w_in"], j, None, tn=tn_in, out_dtype=BF, name="lru_in", **mk)
                xz = xz.reshape(batch, seq_len, 2 * e)
                tt = min(lru_tr // batch, seq_len)
                y0, hf0 = _lru_core_rows(xz, W["lru"], j, 0, init_lru[j][0], tt=tt)
                y, hf1 = _lru_core_rows(xz, W["lru"], j, 1, init_lru[j][1], tt=tt, y_prev=y0)
                m = _mm(y.reshape(rows, e), W["lru_w_out"], j, None, tn=1024, name="lru_out", **mk)
            fin_lru.append((hf0, hf1))
        else:
            h, xr, xw, xk, xv, xa = _shiftmix(
                x, W["norm_pre"], i, mod, W["rwkv_mu"][j], seq_len=seq_len, cond_rows=cond_rows, grid_mode=grid_mode,
                tm=seq_len if not grid_mode else min(512, seq_len))
            tm_plain = min(1024, rows)
            mk = dict(tm=tm_plain, m_tiles=rows // tm_plain)
            r = _mm(xr, W["rwkv_w_r"], j, None, tn=1024, name="rwkv_r", **mk)
            k = _mm(xk, W["rwkv_w_k"], j, None, tn=1024, name="rwkv_k", **mk)
            z = _mm(h, W["rwkv_w_g"], j, None, tn=1024, name="rwkv_g", **mk)
            if j == 0:
                v = _mm(xv, W["rwkv_w_v"], j, None, tn=1024, name="rwkv_v", **mk)
                v_first = v
            else:
                tv = _mm(xv, W["rwkv_v1"], j - 1, None, tn=128, out_dtype=BF, name="rwkv_v1", **mk)
                mix = _mm(tv, W["rwkv_v2"], j - 1, W["rwkv_v0"][j - 1], tn=1024, act=_act_sigmoid,
                          name="rwkv_v2", **mk)
                v = _mm(xv, W["rwkv_w_v"], j, None, tn=1024, extras=(v_first, mix), act=_act_vmix,
                        name="rwkv_v", **mk)
            tw = _mm(xw, W["rwkv_w1"], j, None, tn=256, act=_act_tanh, out_dtype=BF, name="rwkv_w1", **mk)
            ta = _mm(xa, W["rwkv_a1"], j, None, tn=256, out_dtype=BF, name="rwkv_a1", **mk)
            fins = []
            y_d = None
            for dd in range(2):
                z0 = None if init_rwkv is None else init_rwkv[j][dd]
                finish = None if dd == 0 else (y_d, z, W["rwkv_hp"][j])
                y_d, s_d = _rwkv_scan(r, k, v, tw, ta, W["rwkv_lora"], j * 2 + dd, W["rwkv_kp"][j],
                                      W["consts"], dd, batch=batch, seq_len=seq_len, z0=z0,
                                      want_state=want_state, finish=finish)
                fins.append(s_d)
            fin_rwkv.append(fins)
            m = _mm(y_d, W["rwkv_w_o"], j, None, tn=1024, name="rwkv_o", **mk)
        x = _postnorm(x, m, W["norm_post"], i, mod, tm=tm_row, rows_per_cond=cond_rows)
    return x, fin_lru, fin_rwkv


def _scan_consts():
    idx = jnp.arange(GW)
    bm = ((idx[:, None] // HD) == (idx[None, :] // HD)).astype(F32)
    t_r, t_c = jnp.arange(CH)[:, None], idx[None, :] % CH
    cats, tris = [], []
    ic = jnp.arange(CH)
    for d in range(2):
        ge = (t_r >= t_c) if d == 0 else (t_r <= t_c)
        gt = (t_r > t_c) if d == 0 else (t_r < t_c)
        cats.append(jnp.stack([ge.astype(F32), gt.astype(F32)]))
        tris.append(((ic[:, None] >= ic[None, :]) if d == 0 else (ic[:, None] <= ic[None, :])).astype(BF))
    sel = (idx[:, None] % HD == jnp.arange(HD)[None, :]).astype(BF)
    lvl = [t_r // INV_BASE == t_c // INV_BASE]
    size = INV_BASE
    while size < CH:
        lvl.append((t_r // (2 * size) == t_c // (2 * size)) & (t_r // size != t_c // size))
        size *= 2
    return dict(bm=bm, cat=jnp.stack(cats), tri=jnp.stack(tris), sel=sel,
                lvl=jnp.stack(lvl).astype(F32))


def kernel(x_prompt, x_sample, state_lru, state_rwkv, c, c_ctx, ada_w, ada_b, norm_pre, norm_post, lru_w_in, lru_conv_w, lru_conv_b, lru_gate_w, lru_gate_b, lru_lambda, lru_w_out, rwkv_mu, rwkv_w_r, rwkv_w_k, rwkv_w_v, rwkv_w_g, rwkv_w_o, rwkv_w0, rwkv_w1, rwkv_w2, rwkv_a0, rwkv_a1, rwkv_a2, rwkv_k_k, rwkv_k_a, rwkv_r_k, rwkv_ln_w, rwkv_ln_b, rwkv_v0, rwkv_v1, rwkv_v2):
    bp, tp, d = x_prompt.shape
    bs, ts, _ = x_sample.shape
    depth = ada_w.shape[0]
    n_lru, _, e2 = lru_w_in.shape
    e = e2 // 2
    n_rwkv = rwkv_w_r.shape[0]
    nblk = e // LB
    heads = e // HD
    ng = e // GW

    def merge_lead(w, k):
        return w.reshape((-1,) + w.shape[k:])

    lora_v = rwkv_v1.shape[-1]
    pad_v = (-lora_v) % 128
    W = dict(
        norm_pre=norm_pre.reshape(depth, 1, d), norm_post=norm_post.reshape(depth, 1, d),
        lru_w_in=lru_w_in.astype(BF), lru_w_out=lru_w_out.astype(BF),
        lru=dict(conv_w=lru_conv_w, conv_b=lru_conv_b.reshape(n_lru, 1, e),
                 gate_w=merge_lead(lru_gate_w, 2),
                 gate_b=merge_lead(lru_gate_b, 2),
                 lam=lru_lambda.reshape(n_lru * 2, 1, e)),
        rwkv_mu=rwkv_mu,
        rwkv_w_r=rwkv_w_r.astype(BF), rwkv_w_k=rwkv_w_k.astype(BF), rwkv_w_v=rwkv_w_v.astype(BF),
        rwkv_w_g=rwkv_w_g.astype(BF), rwkv_w_o=rwkv_w_o.astype(BF),
        rwkv_w1=jnp.concatenate([rwkv_w1[:, 0], rwkv_w1[:, 1]], axis=-1).astype(BF),
        rwkv_a1=jnp.concatenate([rwkv_a1[:, 0], rwkv_a1[:, 1]], axis=-1).astype(BF),
        rwkv_lora=dict(w2=merge_lead(rwkv_w2, 2).astype(BF), a2=merge_lead(rwkv_a2, 2).astype(BF),
                       w0=rwkv_w0.reshape(n_rwkv * 2, 1, e), a0=rwkv_a0.reshape(n_rwkv * 2, 1, e)),
        rwkv_v0=rwkv_v0.reshape(-1, 1, e),
        rwkv_v1=jnp.pad(rwkv_v1, ((0, 0), (0, 0), (0, pad_v))).astype(BF),
        rwkv_v2=jnp.pad(rwkv_v2, ((0, 0), (0, pad_v), (0, 0))).astype(BF),
        rwkv_kp=jnp.stack([rwkv_k_k, rwkv_k_a], axis=1),
        rwkv_hp=jnp.stack([rwkv_k_a, rwkv_r_k, rwkv_ln_w, rwkv_ln_b], axis=1),
        consts=_scan_consts(),
    )

    n_cond = 1 + bs
    cond = jnp.zeros((SUBLANES * pl.cdiv(n_cond, SUBLANES), d), F32)
    cond = cond.at[0].set(c_ctx).at[1:n_cond].set(c)
    ada_wb = ada_w.astype(BF)
    mods_p, mods_s = [], []
    for i in range(depth):
        mod = _mm(cond, ada_wb, i, ada_b[i][None], tm=cond.shape[0], tn=256, m_tiles=1,
                  a_pre=_silu, name="adaln")
        mod = mod.reshape(cond.shape[0], 3, d)
        mods_p.append(mod[0:1])
        mods_s.append(mod[1:n_cond])

    zero_lru = [(jnp.zeros((bp, e), F32), jnp.zeros((bp, e), F32))] * n_lru
    y_p, fin_lru, fin_rwkv = _trunk(x_prompt.reshape(bp * tp, d), mods_p, zero_lru, None, W,
                                    batch=bp, seq_len=tp, grid_mode=False, want_state=True)
    new_state_lru = jnp.stack([jnp.stack(f, axis=1) for f in fin_lru], axis=1)
    new_state_rwkv = jnp.stack(
        [jnp.stack([s.reshape(bp, heads, HD, HD) for s in f], axis=1) for f in fin_rwkv], axis=1)

    rep = max(SUBLANES // bs, 1)
    init_lru = [tuple(jnp.tile(state_lru[:, jj, dd], (rep, 1)) for dd in range(2)) for jj in range(n_lru)]
    eye_t = jnp.eye(HEADS_PER_TILE, dtype=F32)
    n_tiles = e // LANES
    st = state_rwkv.reshape(bs, n_rwkv, 2, n_tiles, HEADS_PER_TILE, HD, HD)
    z0 = jnp.einsum("bldthvk,hi->bldthkiv", st, eye_t).reshape(
        bs, n_rwkv, 2, ng, GW // LANES, LANES, LANES)
    init_rwkv = [[z0[:, jj, dd] for dd in range(2)] for jj in range(n_rwkv)]
    y_s, _, _ = _trunk(x_sample.reshape(bs * ts, d), mods_s, init_lru, init_rwkv, W,
                       batch=bs, seq_len=ts, grid_mode=True, want_state=False)
    return (y_p.reshape(bp, tp, d), y_s.reshape(bs, ts, d), new_state_lru, new_state_rwkv)
```

```python
import functools

import jax
import jax.numpy as jnp
from jax import lax
from jax.experimental import pallas as pl
from jax.experimental.pallas import tpu as pltpu

F32 = jnp.float32
BF = jnp.bfloat16

NORM_EPS = 1e-6
GN_EPS = 64e-5
LRU_C = 8.0
GRID_W = 64
CONV_W = 4
CONV_LEFT = 2

HD = 64
GH = 4
GW = GH * HD
CH = 64
INV_BASE = 16
SCAN_GROUPS = 16
LB = 256
SUBLANES = 8
LANES = 128
HEADS_PER_TILE = LANES // HD
VMEM_LIMIT = 56 * 1024 * 1024
MM_VMEM_BUDGET = 44 * 1024 * 1024
MM_TILE_ROWS = 1024
MM_TN = 1024
NORM_TILE_ROWS = 256
SHIFT_TILE_ROWS = 512
LRU_TILE_ROWS = 2048


def _cparams(*sem):
    return pltpu.CompilerParams(dimension_semantics=sem, vmem_limit_bytes=VMEM_LIMIT)


def _dot(a, b):
    return jnp.dot(a, b, preferred_element_type=F32)


def _dot_nt(a, b):
    return lax.dot_general(a, b, (((1,), (1,)), ((), ())), preferred_element_type=F32)


def _dot_tn(a, b):
    return lax.dot_general(a, b, (((0,), (0,)), ((), ())), preferred_element_type=F32)


def _split3(x):
    h1 = x.astype(BF)
    r1 = x - h1.astype(F32)
    h2 = r1.astype(BF)
    h3 = (r1 - h2.astype(F32)).astype(BF)
    return h1, h2, h3


def _modnorm(x, g, mod):
    y = x * lax.rsqrt(jnp.mean(x * x, axis=-1, keepdims=True) + NORM_EPS) * g
    return y * (1.0 + mod[1:2]) + mod[0:1]


def _mm_body(*refs, act, a_pre, n_extra):
    a_ref, w_ref, b_ref = refs[:3]
    extras = refs[3:3 + n_extra]
    o_ref = refs[3 + n_extra]
    a = a_ref[...]
    if a_pre is not None:
        a = a_pre(a.astype(F32))
    acc = _dot(a.astype(BF), w_ref[...]) + b_ref[...]
    if act is not None:
        acc = act(acc, *[e[...] for e in extras])
    o_ref[...] = acc.astype(o_ref.dtype)


def _mm_tn(k, n, tm, tn_max, n_extra, out_itemsize):
    tn = min(tn_max, n)
    while tn > LANES:
        w_bytes = 2 * k * tn * 2
        io_bytes = 2 * tm * (k * 2 + tn * (out_itemsize + 4 * n_extra))
        if n % tn == 0 and w_bytes + io_bytes <= MM_VMEM_BUDGET:
            break
        tn //= 2
    return tn


def _mm(a, w3, li, bias, *, tm, tn, m_tiles, extras=(), act=None, a_pre=None, out_dtype=F32, name="mm"):
    _, K, N = w3.shape
    tn = _mm_tn(K, N, tm, tn, len(extras), jnp.dtype(out_dtype).itemsize)
    assert N % tn == 0 and K % LANES == 0
    if bias is None:
        bias = jnp.zeros((1, N), F32)
    in_specs = [
        pl.BlockSpec((tm, K), lambda n, m: (m, 0)),
        pl.BlockSpec((None, K, tn), lambda n, m: (li, 0, n)),
        pl.BlockSpec((1, tn), lambda n, m: (0, n)),
    ] + [pl.BlockSpec((tm, tn), lambda n, m: (m, n)) for _ in extras]
    return pl.pallas_call(
        functools.partial(_mm_body, act=act, a_pre=a_pre, n_extra=len(extras)),
        out_shape=jax.ShapeDtypeStruct((m_tiles * tm, N), out_dtype),
        grid=(N // tn, m_tiles),
        in_specs=in_specs,
        out_specs=pl.BlockSpec((tm, tn), lambda n, m: (m, n)),
        compiler_params=_cparams("parallel", "parallel"),
        name=name,
    )(a, w3, bias, *extras)


def _prenorm_body(x_ref, g_ref, mod_ref, o_ref):
    o_ref[...] = _modnorm(x_ref[...], g_ref[...], mod_ref[0]).astype(o_ref.dtype)


def _prenorm(x, g3, layer, mod, *, tm, rows_per_cond):
    rows, d = x.shape
    return pl.pallas_call(
        _prenorm_body,
        out_shape=jax.ShapeDtypeStruct((rows, d), BF),
        grid=(rows // tm,),
        in_specs=[
            pl.BlockSpec((tm, d), lambda i: (i, 0)),
            pl.BlockSpec((None, 1, d), lambda i: (layer, 0, 0)),
            pl.BlockSpec((1, 3, d), lambda i: ((i * tm) // rows_per_cond, 0, 0)),
        ],
        out_specs=pl.BlockSpec((tm, d), lambda i: (i, 0)),
        compiler_params=_cparams("parallel"),
        name="prenorm",
    )(x, g3, mod)


def _postnorm_body(x_ref, m_ref, g_ref, mod_ref, o_ref):
    m = m_ref[...]
    y = m * lax.rsqrt(jnp.mean(m * m, axis=-1, keepdims=True) + NORM_EPS) * g_ref[...]
    o_ref[...] = x_ref[...] + mod_ref[0][2:3] * y


def _postnorm(x, m, g3, layer, mod, *, tm, rows_per_cond):
    rows, d = x.shape
    return pl.pallas_call(
        _postnorm_body,
        out_shape=jax.ShapeDtypeStruct((rows, d), F32),
        grid=(rows // tm,),
        in_specs=[
            pl.BlockSpec((tm, d), lambda i: (i, 0)),
            pl.BlockSpec((tm, d), lambda i: (i, 0)),
            pl.BlockSpec((None, 1, d), lambda i: (layer, 0, 0)),
            pl.BlockSpec((1, 3, d), lambda i: ((i * tm) // rows_per_cond, 0, 0)),
        ],
        out_specs=pl.BlockSpec((tm, d), lambda i: (i, 0)),
        compiler_params=_cparams("parallel"),
        name="postnorm",
    )(x, m, g3, mod)


def _write_mixes(h, shifted, mu_ref, out_refs):
    xx = shifted - h
    out_refs[0][...] = h.astype(BF)
    for m in range(5):
        out_refs[m + 1][...] = (h + xx * mu_ref[m:m + 1]).astype(BF)


def _shiftmix_seq_body(x_ref, g_ref, mod_ref, mu_ref, *out_refs):
    h = _modnorm(x_ref[...], g_ref[...], mod_ref[0])
    t_len, d = h.shape
    half = d // 2
    t = lax.broadcasted_iota(jnp.int32, (t_len, 1), 0)
    prev = jnp.where(t == 0, 0.0, pltpu.roll(h[:, :half], 1, 0))
    nxt = jnp.where(t == t_len - 1, 0.0, pltpu.roll(h[:, half:], t_len - 1, 0))
    _write_mixes(h, jnp.concatenate([prev, nxt], axis=1), mu_ref, out_refs)


def _shiftmix_grid_body(x_ref, up_ref, dn_ref, g_ref, mod_ref, mu_ref, *out_refs, tiles_per_batch):
    ti = pl.program_id(0) % tiles_per_batch
    g = g_ref[...]
    mod = mod_ref[0]
    h = _modnorm(x_ref[...], g, mod)
    tr, d = h.shape
    q = d // 4
    h_up = jnp.where(ti > 0, _modnorm(up_ref[...], g, mod)[:, 2 * q:3 * q], 0.0)
    h_dn = jnp.where(ti < tiles_per_batch - 1, _modnorm(dn_ref[...], g, mod)[:, 3 * q:], 0.0)
    col = lax.broadcasted_iota(jnp.int32, (tr, 1), 0) % GRID_W
    left = jnp.where(col == 0, 0.0, pltpu.roll(h[:, :q], 1, 0))
    right = jnp.where(col == GRID_W - 1, 0.0, pltpu.roll(h[:, q:2 * q], tr - 1, 0))
    up = jnp.concatenate([h_up, h[:tr - GRID_W, 2 * q:3 * q]], axis=0)
    down = jnp.concatenate([h[GRID_W:, 3 * q:], h_dn], axis=0)
    _write_mixes(h, jnp.concatenate([left, right, up, down], axis=1), mu_ref, out_refs)


def _shiftmix(x, g3, layer, mod, mu, *, seq_len, cond_rows, grid_mode, tm):
    rows, d = x.shape
    outs = [jax.ShapeDtypeStruct((rows, d), BF)] * 6
    common = [
        pl.BlockSpec((None, 1, d), lambda i: (layer, 0, 0)),
        pl.BlockSpec((1, 3, d), lambda i: ((i * tm) // cond_rows, 0, 0)),
        pl.BlockSpec((5, d), lambda i: (0, 0)),
    ]
    out_specs = [pl.BlockSpec((tm, d), lambda i: (i, 0))] * 6
    if not grid_mode:
        assert tm == seq_len
        return pl.pallas_call(
            _shiftmix_seq_body, out_shape=outs, grid=(rows // tm,),
            in_specs=[pl.BlockSpec((tm, d), lambda i: (i, 0))] + common,
            out_specs=out_specs, compiler_params=_cparams("parallel"), name="shiftmix_seq",
        )(x, g3, mod, mu)
    per = tm // GRID_W
    n_halo = rows // GRID_W
    return pl.pallas_call(
        functools.partial(_shiftmix_grid_body, tiles_per_batch=seq_len // tm),
        out_shape=outs, grid=(rows // tm,),
        in_specs=[
            pl.BlockSpec((tm, d), lambda i: (i, 0)),
            pl.BlockSpec((GRID_W, d), lambda i: (jnp.maximum(i * per - 1, 0), 0)),
            pl.BlockSpec((GRID_W, d), lambda i: (jnp.minimum((i + 1) * per, n_halo - 1), 0)),
        ] + common,
        out_specs=out_specs, compiler_params=_cparams("parallel"), name="shiftmix_grid",
    )(x, x, x, g3, mod, mu)


def _lru_gates_scan(xc, gw_ref, gb_ref, lam_ref, a_s, u_s, h, *, rb, reverse):
    tr = xc.shape[0]
    xb = xc.astype(BF)

    def gate(n):
        return 0.5 * jnp.tanh(0.5 * (_dot(xb, gw_ref[n].astype(BF)) + gb_ref[n:n + 1])) + 0.5

    r = gate(0)
    i = gate(1)
    log_a = r * (-LRU_C * jax.nn.softplus(-lam_ref[...]))
    a_s[...] = jnp.exp(log_a)
    th = jnp.tanh(log_a)
    u_s[...] = jnp.sqrt(-2.0 * th / (1.0 - th)) * (i * xc)

    if rb % SUBLANES == 0:
        n_steps = tr // rb

        def body(s, h):
            row = pl.multiple_of(((n_steps - 1 - s) if reverse else s) * rb, rb)
            h = a_s[pl.ds(row, rb), :] * h + u_s[pl.ds(row, rb), :]
            u_s[pl.ds(row, rb), :] = h
            return h

        h = lax.fori_loop(0, n_steps, body, h, unroll=4)
    else:
        assert 2 * rb == SUBLANES
        n_steps = tr // SUBLANES
        first = lax.broadcasted_iota(jnp.int32, (SUBLANES, 1), 0) < rb

        def body(s, h):
            row = pl.multiple_of(((n_steps - 1 - s) if reverse else s) * SUBLANES, SUBLANES)
            a = a_s[pl.ds(row, SUBLANES), :]
            u = u_s[pl.ds(row, SUBLANES), :]
            h1 = a * h + u
            h2 = a * pltpu.roll(h1, rb, 0) + u
            u_s[pl.ds(row, SUBLANES), :] = jnp.where(first != reverse, h1, h2)
            return pltpu.roll(h2, rb, 0)

        h = lax.fori_loop(0, n_steps, body, h, unroll=4)
    return h


def _lru_rows_body(*refs, nb, reverse, combine, n_tiles):
    if combine:
        (x_ref, xp_ref, xn_ref, cw_ref, cb_ref, gw_ref, gb_ref, lam_ref, h0_ref, yp_ref, z_ref,
         y_ref, hf_ref, xs, a_s, u_s, hc, zs) = refs
    else:
        (x_ref, xp_ref, xn_ref, cw_ref, cb_ref, gw_ref, gb_ref, lam_ref, h0_ref,
         y_ref, hf_ref, xs, a_s, u_s, hc) = refs
    tt = x_ref.shape[1]
    ht = xp_ref.shape[1]
    tr, hb = tt * nb, ht * nb
    tiles = [slice(t * LANES, (t + 1) * LANES) for t in range(LB // LANES)]
    step = pl.program_id(2)
    t_idx = (n_tiles - 1 - step) if reverse else step

    @pl.when(step == 0)
    def _():
        hc[...] = h0_ref[...]

    for b in range(nb):
        for t, ln in enumerate(tiles):
            xs[t, pl.ds(hb + b, tt, stride=nb), :] = x_ref[b, :, ln].astype(F32)
            xs[t, pl.ds(b, ht, stride=nb), :] = jnp.where(t_idx > 0, xp_ref[b, :, ln].astype(F32), 0.0)
            xs[t, pl.ds(hb + tr + b, ht, stride=nb), :] = jnp.where(
                t_idx < n_tiles - 1, xn_ref[b, :, ln].astype(F32), 0.0)
    parts = []
    for t, ln in enumerate(tiles):
        acc = cb_ref[:, ln]
        for tap in range(CONV_W):
            off = hb + (tap - CONV_LEFT) * nb
            acc = acc + xs[t, off:off + tr, :] * cw_ref[tap:tap + 1, ln]
        parts.append(acc)
    xc = jnp.concatenate(parts, axis=1)
    h = _lru_gates_scan(xc, gw_ref, gb_ref, lam_ref, a_s, u_s, hc[...], rb=nb, reverse=reverse)
    hc[...] = h
    hf_ref[...] = h
    if not combine:
        y_ref[...] = u_s[...]
        return
    for t, ln in enumerate(tiles):
        zs[t] = yp_ref[:, ln] + u_s[:, ln]
    for b in range(nb):
        for t, ln in enumerate(tiles):
            z = z_ref[b, :, ln].astype(F32)
            y_b = zs[t, pl.ds(b, tt, stride=nb), :]
            y_ref[b, :, ln] = (y_b * (z * jax.nn.sigmoid(z))).astype(y_ref.dtype)


def _lru_core_rows(xz3, lru, j, d, h0, *, y_prev=None):
    batch, seq_len, e2 = xz3.shape
    e = e2 // 2
    nblk = e // LB
    nb = min(batch, SUBLANES)
    n_groups = batch // nb
    tt = min(LRU_TILE_ROWS // nb, seq_len)
    n_tiles = seq_len // tt
    ht = 2 * SUBLANES
    per = tt // ht
    n_halo = seq_len // ht
    tr, hb = tt * nb, ht * nb
    hr = h0.shape[0] // n_groups
    reverse = d == 1
    combine = y_prev is not None

    def tidx(s):
        return (n_tiles - 1 - s) if reverse else s

    in_specs = [
        pl.BlockSpec((nb, tt, LB), lambda c, g, s: (g, tidx(s), c)),
        pl.BlockSpec((nb, ht, LB), lambda c, g, s: (g, jnp.maximum(tidx(s) * per - 1, 0), c)),
        pl.BlockSpec((nb, ht, LB), lambda c, g, s: (g, jnp.minimum((tidx(s) + 1) * per, n_halo - 1), c)),
        pl.BlockSpec((None, CONV_W, LB), lambda c, g, s: (j, 0, c)),
        pl.BlockSpec((None, 1, LB), lambda c, g, s: (j, 0, c)),
        pl.BlockSpec((None, 2, None, LB, LB), lambda c, g, s: (j * 2 + d, 0, c, 0, 0)),
        pl.BlockSpec((None, 2, LB), lambda c, g, s: (j * 2 + d, 0, c)),
        pl.BlockSpec((None, 1, LB), lambda c, g, s: (j * 2 + d, 0, c)),
        pl.BlockSpec((hr, LB), lambda c, g, s: (g, c)),
    ]
    args = [xz3, xz3, xz3, lru["conv_w"], lru["conv_b"], lru["gate_w"], lru["gate_b"], lru["lam"], h0]
    lt = LB // LANES
    scratch = [pltpu.VMEM((lt, tr + 2 * hb, LANES), F32), pltpu.VMEM((tr, LB), F32),
               pltpu.VMEM((tr, LB), F32), pltpu.VMEM((hr, LB), F32)]
    if combine:
        in_specs += [
            pl.BlockSpec((None, tr, LB), lambda c, g, s: (g, tidx(s), c)),
            pl.BlockSpec((nb, tt, LB), lambda c, g, s: (g, tidx(s), nblk + c)),
        ]
        args += [y_prev, xz3]
        scratch.append(pltpu.VMEM((lt, tr, LANES), F32))
        y_shape = jax.ShapeDtypeStruct((batch, seq_len, e), BF)
        y_spec = pl.BlockSpec((nb, tt, LB), lambda c, g, s: (g, tidx(s), c))
    else:
        y_shape = jax.ShapeDtypeStruct((n_groups, seq_len * nb, e), F32)
        y_spec = pl.BlockSpec((None, tr, LB), lambda c, g, s: (g, tidx(s), c))
    return pl.pallas_call(
        functools.partial(_lru_rows_body, nb=nb, reverse=reverse, combine=combine, n_tiles=n_tiles),
        out_shape=[y_shape, jax.ShapeDtypeStruct((hr * n_groups, e), F32)],
        grid=(nblk, n_groups, n_tiles),
        in_specs=in_specs,
        out_specs=[y_spec, pl.BlockSpec((hr, LB), lambda c, g, s: (g, c))],
        scratch_shapes=scratch,
        compiler_params=_cparams("parallel", "parallel", "arbitrary"),
        name="lru_core_rows",
    )(*args)


def _rwkv_scan_body(*refs, reverse, zero_init, want_state, finish, n_chunks):
    it = iter(refs)
    (r_ref, k_ref, v_ref, tw_ref, ta_ref, w2_ref, w0_ref, a2_ref, a0_ref, kp_ref, bm_ref,
     tri_ref, cat_ref, lvl_ref) = (next(it) for _ in range(14))
    if finish:
        tao_ref, a2o_ref, a0o_ref, yo_ref, zg_ref, hp_ref = (next(it) for _ in range(6))
    sel_ref = next(it) if want_state else None
    z0_ref = None if zero_init else next(it)
    y_ref = next(it)
    sf_ref = next(it) if want_state else None
    z_s = next(it)
    c = pl.program_id(2)

    @pl.when(c == 0)
    def _():
        z_s[...] = jnp.zeros_like(z_s) if zero_init else z0_ref[...]

    lw_all = _act_logdecay(_dot(tw_ref[...], w2_ref[...]) + w0_ref[...])
    as_all = jax.nn.sigmoid(_dot(ta_ref[...], a2_ref[...]) + a0_ref[...])

    groups = range(z_s.shape[0])
    lanes = [slice(g * GW, (g + 1) * GW) for g in groups]
    bm = bm_ref[...]
    rs = [r_ref[:, l] for l in lanes]
    ks = [k_ref[:, l] for l in lanes]
    vs = [v_ref[:, l] for l in lanes]
    ys, z_new = _rwkv_chunk(
        rs, ks, vs, [lw_all[:, l] for l in lanes], [as_all[:, l] for l in lanes],
        [kp_ref[:, l] for l in lanes], bm, tri_ref[...], cat_ref, lvl_ref,
        [[z_s[g, t] for t in range(z_s.shape[1])] for g in groups], reverse=reverse)
    for g in groups:
        for t in range(z_s.shape[1]):
            z_s[g, t] = z_new[g][t]
    if not finish:
        for g in groups:
            y_ref[:, lanes[g]] = ys[g]
    else:
        bm_b = bm.astype(BF)
        aso_all = jax.nn.sigmoid(_dot(tao_ref[...], a2o_ref[...]) + a0o_ref[...])

        def headsum(x):
            return sum(_dot(p, bm_b) for p in _split3(x)[:2])

        ysum = [ys[g] + yo_ref[:, lanes[g]] for g in groups]
        arg = [rs[g] * ks[g] * hp_ref[1:2, lanes[g]]
               * (2.0 + hp_ref[0:1, lanes[g]] * (as_all[:, lanes[g]] + aso_all[:, lanes[g]] - 2.0))
               for g in groups]
        sums = [headsum(jnp.concatenate([ysum[g], arg[g]], axis=0)) for g in groups]
        yc = [ysum[g] - sums[g][:CH] * (1.0 / HD) for g in groups]
        var = [headsum(yc[g] * yc[g]) * (1.0 / HD) for g in groups]
        for g in groups:
            out = (yc[g] * lax.rsqrt(var[g] + GN_EPS) * hp_ref[2:3, lanes[g]] + hp_ref[3:4, lanes[g]]
                   + sums[g][CH:] * vs[g])
            zg = zg_ref[:, lanes[g]]
            y_ref[:, lanes[g]] = (out * (zg * jax.nn.sigmoid(zg))).astype(y_ref.dtype)
    if want_state:
        @pl.when(c == n_chunks - 1)
        def _():
            sel = sel_ref[...]
            for g in groups:
                for t in range(z_s.shape[1]):
                    sf_ref[g * GW + t * LANES:g * GW + (t + 1) * LANES, :] = sum(
                        _dot_tn(p, sel) for p in _split3(z_new[g][t])[:2])


def _vmap(fn, *lists):
    return [fn(*args) for args in zip(*lists)]


def _rwkv_chunk(r, k, v, lw, asg, kp, bm, tri, cat_ref, lvl_ref, z, *, reverse):
    incl = cat_ref[0]
    strict = cat_ref[1]
    eye_c = incl - strict
    bm_b = bm.astype(BF)

    def headsum(x):
        return sum(_dot(p, bm_b) for p in _split3(x)[:2])

    lane_head = lax.broadcasted_iota(jnp.int32, (1, LANES), 1) // HD
    zero_tile = jnp.zeros((CH, LANES), BF)

    def bdm(x):
        xb = x.astype(BF)
        blocks = []
        for h in range(GH):
            j = h // HEADS_PER_TILE
            tile = xb[:, j * LANES:(j + 1) * LANES]
            keep = jnp.where(lane_head == h % HEADS_PER_TILE, tile, jnp.zeros_like(tile))
            blocks.append(jnp.concatenate([keep if t == j else zero_tile for t in range(GW // LANES)],
                                          axis=1))
        return jnp.concatenate(blocks, axis=0)

    def mm(a, b_bd):
        return _vmap(lambda x, y: _dot(x.astype(BF), y), a, b_bd)

    def stack2(a, b):
        return _vmap(lambda x, y: jnp.concatenate([x, y], axis=0), a, b)

    def top(a):
        return [x[:CH] for x in a]

    def bot(a):
        return [x[CH:] for x in a]

    kk = _vmap(lambda k_, p: k_ * p[0:1], k, kp)
    ss = _vmap(lambda x: headsum(x * x), kk)
    kk = _vmap(lambda x, s: x * lax.rsqrt(jnp.maximum(s, 1e-24)), kk, ss)
    kd = _vmap(lambda k_, a, p: k_ * (1.0 + (a - 1.0) * p[1:2]), k, asg, kp)
    bvec = _vmap(lambda x, a: x * a, kk, asg)

    cum = _vmap(lambda x: sum(_dot(tri, p) for p in _split3(x)[:2]), lw)
    tot = _vmap(lambda x: x[0:1] if reverse else x[CH - 1:CH], cum)
    e_pos = _vmap(jnp.exp, cum)
    e_neg = _vmap(lambda x: jnp.exp(-x), cum)
    e_rem = _vmap(lambda t, en: jnp.exp(t) * en, tot, e_neg)
    a_t = _vmap(lambda x, ep, l: -x * (ep * jnp.exp(-l)), kk, e_pos, lw)
    r_t = _vmap(lambda x, ep: x * ep, r, e_pos)
    ar = [x.astype(BF) for x in stack2(a_t, r_t)]
    bt_bd = _vmap(lambda x, en: bdm(x * en), bvec, e_neg)
    kt_bd = _vmap(lambda x, en: bdm(x * en), kd, e_neg)
    keep2 = jnp.concatenate([strict, incl], axis=0) > 0
    ab = [jnp.where(keep2, m, 0.0) for m in _vmap(_dot_nt, ar, bt_bd)]
    ak = [jnp.where(keep2, m, 0.0).astype(BF) for m in _vmap(_dot_nt, ar, kt_bd)]
    a_ab = top(ab)
    a_rb = [m.astype(BF) for m in bot(ab)]

    p = [n * lvl_ref[0] for n in a_ab]
    x = [eye_c + n for n in p]
    p = [n.astype(BF) for n in p]
    pw = [m.astype(BF) for m in mm(p, _vmap(bdm, p))]
    e = 2
    while e < INV_BASE:
        rhs = _vmap(bdm, pw)
        if 2 * e < INV_BASE:
            both = mm(stack2([m.astype(BF) for m in x], pw), rhs)
            x = _vmap(jnp.add, x, top(both))
            pw = [m.astype(BF) for m in bot(both)]
        else:
            x = _vmap(jnp.add, x, mm(x, rhs))
        e *= 2
    for lvl in range(1, lvl_ref.shape[0]):
        off = [(n * lvl_ref[lvl]).astype(BF) for n in a_ab]
        xb = [m.astype(BF) for m in x]
        x = _vmap(jnp.add, x, mm(xb, _vmap(bdm, mm(off, _vmap(bdm, xb)))))
    xb = [m.astype(BF) for m in x]

    vb = [m.astype(BF) for m in v]
    s_v = mm(ak, _vmap(bdm, vb))
    def z_rhs(zq):
        n_t = len(zq)
        zero = jnp.zeros((LANES, LANES), BF)
        return jnp.concatenate(
            [jnp.concatenate([zq[i].astype(BF) if i == t else zero for t in range(n_t)], axis=1)
             for i in range(n_t)], axis=0)

    s_z = _vmap(lambda a, zq: _dot(a, z_rhs(zq)), ar, z)
    u = mm(xb, _vmap(lambda a, b_: bdm(a + b_), top(s_z), top(s_v)))
    ub = [m.astype(BF) for m in u]
    y =_vmap(lambda a, b_, c_: a + b_ + c_, bot(s_z), bot(s_v), mm(a_rb, _vmap(bdm, ub)))
    row_id = lax.broadcasted_iota(jnp.int32, (LANES, 1), 0)
    lane_id = lax.broadcasted_iota(jnp.int32, (1, LANES), 1)
    same_head = (row_id // HD) == (lane_id // HD)
    diag = row_id == lane_id
    lhs = _vmap(lambda b_, k_, er: jnp.concatenate([b_ * er, k_ * er], axis=0).astype(BF), bvec, kd, e_rem)
    rhs = stack2(ub, vb)
    decay = _vmap(jnp.exp, tot)
    z_new = [[] for _ in z]
    for t in range(GW // LANES):
        cols = slice(t * LANES, (t + 1) * LANES)
        for g in range(len(z)):
            upd = jnp.where(same_head, _dot_tn(lhs[g][:, cols], rhs[g][:, cols]), 0.0)
            g_col = jnp.sum(jnp.where(diag, decay[g][:, cols], 0.0), axis=1, keepdims=True)
            z_new[g].append(z[g][t] * g_col + upd)
    return y, z_new


def _rwkv_scan(r, k, v, tw, ta, lora, jd, kp, consts, d, *, batch, seq_len, z0=None, want_state=False,
               finish=None):
    rows, e = r.shape
    ngs = min(SCAN_GROUPS, e // GW)
    sw = ngs * GW
    ng = e // sw
    nc = seq_len // CH
    reverse = d == 1
    zero_init = z0 is None
    rank = lora["w2"].shape[1]

    def cidx(c):
        return (nc - 1 - c) if reverse else c

    def lora_specs(dd, jdd):
        return [pl.BlockSpec((CH, rank), lambda b, g, c: (b * nc + cidx(c), dd)),
                pl.BlockSpec((None, rank, sw), lambda b, g, c: (jdd, 0, g)),
                pl.BlockSpec((None, 1, sw), lambda b, g, c: (jdd, 0, g))]

    tok = pl.BlockSpec((CH, sw), lambda b, g, c: (b * nc + cidx(c), g))
    w_specs, a_specs = lora_specs(d, jd), lora_specs(d, jd)
    in_specs = [tok, tok, tok, w_specs[0], a_specs[0], w_specs[1], w_specs[2], a_specs[1], a_specs[2],
                pl.BlockSpec((2, sw), lambda b, g, c: (0, g)),
                pl.BlockSpec((GW, GW), lambda b, g, c: (0, 0)),
                pl.BlockSpec((None, CH, CH), lambda b, g, c: (d, 0, 0)),
                pl.BlockSpec((None, 2, CH, GW), lambda b, g, c: (d, 0, 0, 0)),
                pl.BlockSpec(consts["lvl"].shape, lambda b, g, c: (0, 0, 0))]
    args = [r, k, v, tw, ta, lora["w2"], lora["w0"], lora["a2"], lora["a0"], kp, consts["bm"],
            consts["tri"], consts["cat"], consts["lvl"]]
    if finish is not None:
        y_other, zg, hp = finish
        in_specs += lora_specs(1 - d, jd + 1 - 2 * d) + [tok, tok, pl.BlockSpec((4, sw), lambda b, g, c: (0, g))]
        args += [ta, lora["a2"], lora["a0"], y_other, zg, hp]
    if want_state:
        in_specs.append(pl.BlockSpec((LANES, HD), lambda b, g, c: (0, 0)))
        args.append(consts["sel"])
    if not zero_init:
        in_specs.append(pl.BlockSpec((None, ngs, GW // LANES, LANES, LANES), lambda b, g, c: (b, g, 0, 0, 0)))
        args.append(z0)
    out_shape = [jax.ShapeDtypeStruct((rows, e), F32 if finish is None else BF)]
    out_specs = [tok]
    if want_state:
        out_shape.append(jax.ShapeDtypeStruct((batch, e, HD), F32))
        out_specs.append(pl.BlockSpec((None, sw, HD), lambda b, g, c: (b, g, 0)))
    res = pl.pallas_call(
        functools.partial(_rwkv_scan_body, reverse=reverse, zero_init=zero_init,
                          want_state=want_state, finish=finish is not None, n_chunks=nc),
        out_shape=out_shape,
        grid=(batch, ng, nc),
        in_specs=in_specs,
        out_specs=out_specs,
        scratch_shapes=[pltpu.VMEM((ngs, GW // LANES, LANES, LANES), F32)],
        compiler_params=_cparams("parallel", "parallel", "arbitrary"),
        name="rwkv_scan",
    )(*args)
    return res if want_state else (res[0], None)


def _silu(x):
    return x * jax.nn.sigmoid(x)


def _act_tanh(x):
    return jnp.tanh(x)


def _act_sigmoid(x):
    return jax.nn.sigmoid(x)


def _act_logdecay(x):
    return -jnp.exp(-0.5) * jax.nn.sigmoid(x)


def _act_vmix(acc, v_first, mix):
    return acc + (v_first - acc) * mix


def _trunk(x, mods, init_lru, init_rwkv, W, *, batch, seq_len, grid_mode, want_state):
    rows, d = x.shape
    e = W["lru_w_out"].shape[1]
    depth = W["norm_pre"].shape[0]
    tm_row = min(NORM_TILE_ROWS, seq_len)
    tm_mm = min(MM_TILE_ROWS, rows)
    mk = dict(tm=tm_mm, m_tiles=rows // tm_mm)
    cond_rows = rows // mods[0].shape[0]
    fin_lru, fin_rwkv = [], []
    v_first = None

    for i in range(depth):
        mod = mods[i]
        j = i // 2
        if i % 2 == 0:
            h = _prenorm(x, W["norm_pre"], i, mod, tm=tm_row, rows_per_cond=cond_rows)
            xz = _mm(h, W["lru_w_in"], j, None, tn=MM_TN, out_dtype=BF, name="lru_in", **mk)
            xz = xz.reshape(batch, seq_len, 2 * e)
            y0, hf0 = _lru_core_rows(xz, W["lru"], j, 0, init_lru[j][0])
            y, hf1 = _lru_core_rows(xz, W["lru"], j, 1, init_lru[j][1], y_prev=y0)
            m = _mm(y.reshape(rows, e), W["lru_w_out"], j, None, tn=MM_TN, name="lru_out", **mk)
            fin_lru.append((hf0, hf1))
        else:
            h, xr, xw, xk, xv, xa = _shiftmix(
                x, W["norm_pre"], i, mod, W["rwkv_mu"][j], seq_len=seq_len, cond_rows=cond_rows, grid_mode=grid_mode,
                tm=seq_len if not grid_mode else min(SHIFT_TILE_ROWS, seq_len))
            r = _mm(xr, W["rwkv_w_r"], j, None, tn=MM_TN, name="rwkv_r", **mk)
            k = _mm(xk, W["rwkv_w_k"], j, None, tn=MM_TN, name="rwkv_k", **mk)
            z = _mm(h, W["rwkv_w_g"], j, None, tn=MM_TN, name="rwkv_g", **mk)
            if j == 0:
                v = _mm(xv, W["rwkv_w_v"], j, None, tn=MM_TN, name="rwkv_v", **mk)
                v_first = v
            else:
                tv = _mm(xv, W["rwkv_v1"], j - 1, None, tn=MM_TN, out_dtype=BF, name="rwkv_v1", **mk)
                mix = _mm(tv, W["rwkv_v2"], j - 1, W["rwkv_v0"][j - 1], tn=MM_TN, act=_act_sigmoid,
                          name="rwkv_v2", **mk)
                v = _mm(xv, W["rwkv_w_v"], j, None, tn=MM_TN, extras=(v_first, mix), act=_act_vmix,
                        name="rwkv_v", **mk)
            tw = _mm(xw, W["rwkv_w1"], j, None, tn=MM_TN, act=_act_tanh, out_dtype=BF, name="rwkv_w1", **mk)
            ta = _mm(xa, W["rwkv_a1"], j, None, tn=MM_TN, out_dtype=BF, name="rwkv_a1", **mk)
            fins = []
            y_d = None
            for dd in range(2):
                z0 = None if init_rwkv is None else init_rwkv[j][dd]
                finish = None if dd == 0 else (y_d, z, W["rwkv_hp"][j])
                y_d, s_d = _rwkv_scan(r, k, v, tw, ta, W["rwkv_lora"], j * 2 + dd, W["rwkv_kp"][j],
                                      W["consts"], dd, batch=batch, seq_len=seq_len, z0=z0,
                                      want_state=want_state, finish=finish)
                fins.append(s_d)
            fin_rwkv.append(fins)
            m = _mm(y_d, W["rwkv_w_o"], j, None, tn=MM_TN, name="rwkv_o", **mk)
        x = _postnorm(x, m, W["norm_post"], i, mod, tm=tm_row, rows_per_cond=cond_rows)
    return x, fin_lru, fin_rwkv


def _scan_consts():
    idx = jnp.arange(GW)
    bm = ((idx[:, None] // HD) == (idx[None, :] // HD)).astype(F32)
    t_r, t_c = jnp.arange(CH)[:, None], idx[None, :] % CH
    cats, tris = [], []
    ic = jnp.arange(CH)
    for d in range(2):
        ge = (t_r >= t_c) if d == 0 else (t_r <= t_c)
        gt = (t_r > t_c) if d == 0 else (t_r < t_c)
        cats.append(jnp.stack([ge.astype(F32), gt.astype(F32)]))
        tris.append(((ic[:, None] >= ic[None, :]) if d == 0 else (ic[:, None] <= ic[None, :])).astype(BF))
    sel = (idx[:, None] % HD == jnp.arange(HD)[None, :]).astype(BF)
    lvl = [t_r // INV_BASE == t_c // INV_BASE]
    size = INV_BASE
    while size < CH:
        lvl.append((t_r // (2 * size) == t_c // (2 * size)) & (t_r // size != t_c // size))
        size *= 2
    return dict(bm=bm, cat=jnp.stack(cats), tri=jnp.stack(tris), sel=sel,
                lvl=jnp.stack(lvl).astype(F32))


def kernel(x_prompt, x_sample, state_lru, state_rwkv, c, c_ctx, ada_w, ada_b, norm_pre, norm_post, lru_w_in, lru_conv_w, lru_conv_b, lru_gate_w, lru_gate_b, lru_lambda, lru_w_out, rwkv_mu, rwkv_w_r, rwkv_w_k, rwkv_w_v, rwkv_w_g, rwkv_w_o, rwkv_w0, rwkv_w1, rwkv_w2, rwkv_a0, rwkv_a1, rwkv_a2, rwkv_k_k, rwkv_k_a, rwkv_r_k, rwkv_ln_w, rwkv_ln_b, rwkv_v0, rwkv_v1, rwkv_v2):
    bp, tp, d = x_prompt.shape
    bs, ts, _ = x_sample.shape
    depth = ada_w.shape[0]
    n_lru, _, e2 = lru_w_in.shape
    e = e2 // 2
    n_rwkv = rwkv_w_r.shape[0]
    nblk = e // LB
    heads = e // HD
    ng = e // GW

    def merge_lead(w, k):
        return w.reshape((-1,) + w.shape[k:])

    lora_v = rwkv_v1.shape[-1]
    pad_v = (-lora_v) % 128
    W = dict(
        norm_pre=norm_pre.reshape(depth, 1, d), norm_post=norm_post.reshape(depth, 1, d),
        lru_w_in=lru_w_in.astype(BF), lru_w_out=lru_w_out.astype(BF),
        lru=dict(conv_w=lru_conv_w, conv_b=lru_conv_b.reshape(n_lru, 1, e),
                 gate_w=merge_lead(lru_gate_w, 2),
                 gate_b=merge_lead(lru_gate_b, 2),
                 lam=lru_lambda.reshape(n_lru * 2, 1, e)),
        rwkv_mu=rwkv_mu,
        rwkv_w_r=rwkv_w_r.astype(BF), rwkv_w_k=rwkv_w_k.astype(BF), rwkv_w_v=rwkv_w_v.astype(BF),
        rwkv_w_g=rwkv_w_g.astype(BF), rwkv_w_o=rwkv_w_o.astype(BF),
        rwkv_w1=jnp.concatenate([rwkv_w1[:, 0], rwkv_w1[:, 1]], axis=-1).astype(BF),
        rwkv_a1=jnp.concatenate([rwkv_a1[:, 0], rwkv_a1[:, 1]], axis=-1).astype(BF),
        rwkv_lora=dict(w2=merge_lead(rwkv_w2, 2).astype(BF), a2=merge_lead(rwkv_a2, 2).astype(BF),
                       w0=rwkv_w0.reshape(n_rwkv * 2, 1, e), a0=rwkv_a0.reshape(n_rwkv * 2, 1, e)),
        rwkv_v0=rwkv_v0.reshape(-1, 1, e),
        rwkv_v1=jnp.pad(rwkv_v1, ((0, 0), (0, 0), (0, pad_v))).astype(BF),
        rwkv_v2=jnp.pad(rwkv_v2, ((0, 0), (0, pad_v), (0, 0))).astype(BF),
        rwkv_kp=jnp.stack([rwkv_k_k, rwkv_k_a], axis=1),
        rwkv_hp=jnp.stack([rwkv_k_a, rwkv_r_k, rwkv_ln_w, rwkv_ln_b], axis=1),
        consts=_scan_consts(),
    )

    n_cond = 1 + bs
    cond = jnp.zeros((SUBLANES * pl.cdiv(n_cond, SUBLANES), d), F32)
    cond = cond.at[0].set(c_ctx).at[1:n_cond].set(c)
    ada_wb = ada_w.astype(BF)
    mods_p, mods_s = [], []
    for i in range(depth):
        mod = _mm(cond, ada_wb, i, ada_b[i][None], tm=cond.shape[0], tn=256, m_tiles=1,
                  a_pre=_silu, name="adaln")
        mod = mod.reshape(cond.shape[0], 3, d)
        mods_p.append(mod[0:1])
        mods_s.append(mod[1:n_cond])

    zero_lru = [(jnp.zeros((bp, e), F32), jnp.zeros((bp, e), F32))] * n_lru
    y_p, fin_lru, fin_rwkv = _trunk(x_prompt.reshape(bp * tp, d), mods_p, zero_lru, None, W,
                                    batch=bp, seq_len=tp, grid_mode=False, want_state=True)
    new_state_lru = jnp.stack([jnp.stack(f, axis=1) for f in fin_lru], axis=1)
    new_state_rwkv = jnp.stack(
        [jnp.stack([s.reshape(bp, heads, HD, HD) for s in f], axis=1) for f in fin_rwkv], axis=1)

    rep = max(SUBLANES // bs, 1)
    init_lru = [tuple(jnp.tile(state_lru[:, jj, dd], (rep, 1)) for dd in range(2)) for jj in range(n_lru)]
    eye_t = jnp.eye(HEADS_PER_TILE, dtype=F32)
    n_tiles = e // LANES
    st = state_rwkv.reshape(bs, n_rwkv, 2, n_tiles, HEADS_PER_TILE, HD, HD)
    z0 = jnp.einsum("bldthvk,hi->bldthkiv", st, eye_t).reshape(
        bs, n_rwkv, 2, ng, GW // LANES, LANES, LANES)
    init_rwkv = [[z0[:, jj, dd] for dd in range(2)] for jj in range(n_rwkv)]
    y_s, _, _ = _trunk(x_sample.reshape(bs * ts, d), mods_s, init_lru, init_rwkv, W,
                       batch=bs, seq_len=ts, grid_mode=True, want_state=False)
    return (y_p.reshape(bp, tp, d), y_s.reshape(bs, ts, d), new_state_lru, new_state_rwkv)
```

```python
import functools

import jax
import jax.numpy as jnp
from jax import lax
from jax.experimental import pallas as pl
from jax.experimental.pallas import tpu as pltpu

F32 = jnp.float32
BF = jnp.bfloat16

NORM_EPS = 1e-6
GN_EPS = 64e-5
LRU_C = 8.0
GRID_W = 64
CONV_W = 4
CONV_LEFT = 2

HD = 64
GH = 4
GW = GH * HD
CH = 64
INV_BASE = 16
SCAN_GROUPS = 16
LB = 256
SUBLANES = 8
LANES = 128
HEADS_PER_TILE = LANES // HD
VMEM_LIMIT = 56 * 1024 * 1024
MM_VMEM_BUDGET = 44 * 1024 * 1024
MM_TILE_ROWS = 1024
MM_TN = 1024
NORM_TILE_ROWS = 256
SHIFT_TILE_ROWS = 512
LRU_TILE_ROWS = 1024
LRU_STEP_LANES = 1024


def _cparams(*sem):
    return pltpu.CompilerParams(dimension_semantics=sem, vmem_limit_bytes=VMEM_LIMIT)


def _dot(a, b):
    return jnp.dot(a, b, preferred_element_type=F32)


def _dot_nt(a, b):
    return lax.dot_general(a, b, (((1,), (1,)), ((), ())), preferred_element_type=F32)


def _dot_tn(a, b):
    return lax.dot_general(a, b, (((0,), (0,)), ((), ())), preferred_element_type=F32)


def _split3(x):
    h1 = x.astype(BF)
    r1 = x - h1.astype(F32)
    h2 = r1.astype(BF)
    h3 = (r1 - h2.astype(F32)).astype(BF)
    return h1, h2, h3


def _modnorm(x, g, mod):
    y = x * lax.rsqrt(jnp.mean(x * x, axis=-1, keepdims=True) + NORM_EPS) * g
    return y * (1.0 + mod[1:2]) + mod[0:1]


def _mm_body(*refs, act, a_pre, n_extra):
    a_ref, w_ref, b_ref = refs[:3]
    extras = refs[3:3 + n_extra]
    o_ref = refs[3 + n_extra]
    a = a_ref[...]
    if a_pre is not None:
        a = a_pre(a.astype(F32))
    acc = _dot(a.astype(BF), w_ref[...]) + b_ref[...]
    if act is not None:
        acc = act(acc, *[e[...] for e in extras])
    o_ref[...] = acc.astype(o_ref.dtype)


def _mm_tn(k, n, tm, tn_max, n_extra, out_itemsize):
    tn = min(tn_max, n)
    while tn > LANES:
        w_bytes = 2 * k * tn * 2
        io_bytes = 2 * tm * (k * 2 + tn * (out_itemsize + 4 * n_extra))
        if n % tn == 0 and w_bytes + io_bytes <= MM_VMEM_BUDGET:
            break
        tn //= 2
    return tn


def _mm(a, w3, li, bias, *, tm, tn, m_tiles, extras=(), act=None, a_pre=None, out_dtype=F32, name="mm"):
    _, K, N = w3.shape
    tn = _mm_tn(K, N, tm, tn, len(extras), jnp.dtype(out_dtype).itemsize)
    assert N % tn == 0 and K % LANES == 0
    if bias is None:
        bias = jnp.zeros((1, N), F32)
    in_specs = [
        pl.BlockSpec((tm, K), lambda n, m: (m, 0)),
        pl.BlockSpec((None, K, tn), lambda n, m: (li, 0, n)),
        pl.BlockSpec((1, tn), lambda n, m: (0, n)),
    ] + [pl.BlockSpec((tm, tn), lambda n, m: (m, n)) for _ in extras]
    return pl.pallas_call(
        functools.partial(_mm_body, act=act, a_pre=a_pre, n_extra=len(extras)),
        out_shape=jax.ShapeDtypeStruct((m_tiles * tm, N), out_dtype),
        grid=(N // tn, m_tiles),
        in_specs=in_specs,
        out_specs=pl.BlockSpec((tm, tn), lambda n, m: (m, n)),
        compiler_params=_cparams("parallel", "parallel"),
        name=name,
    )(a, w3, bias, *extras)


def _prenorm_body(x_ref, g_ref, mod_ref, o_ref):
    o_ref[...] = _modnorm(x_ref[...], g_ref[...], mod_ref[0]).astype(o_ref.dtype)


def _prenorm(x, g3, layer, mod, *, tm, rows_per_cond):
    rows, d = x.shape
    return pl.pallas_call(
        _prenorm_body,
        out_shape=jax.ShapeDtypeStruct((rows, d), BF),
        grid=(rows // tm,),
        in_specs=[
            pl.BlockSpec((tm, d), lambda i: (i, 0)),
            pl.BlockSpec((None, 1, d), lambda i: (layer, 0, 0)),
            pl.BlockSpec((1, 3, d), lambda i: ((i * tm) // rows_per_cond, 0, 0)),
        ],
        out_specs=pl.BlockSpec((tm, d), lambda i: (i, 0)),
        compiler_params=_cparams("parallel"),
        name="prenorm",
    )(x, g3, mod)


def _postnorm_body(x_ref, m_ref, g_ref, mod_ref, o_ref):
    m = m_ref[...]
    y = m * lax.rsqrt(jnp.mean(m * m, axis=-1, keepdims=True) + NORM_EPS) * g_ref[...]
    o_ref[...] = x_ref[...] + mod_ref[0][2:3] * y


def _postnorm(x, m, g3, layer, mod, *, tm, rows_per_cond):
    rows, d = x.shape
    return pl.pallas_call(
        _postnorm_body,
        out_shape=jax.ShapeDtypeStruct((rows, d), F32),
        grid=(rows // tm,),
        in_specs=[
            pl.BlockSpec((tm, d), lambda i: (i, 0)),
            pl.BlockSpec((tm, d), lambda i: (i, 0)),
            pl.BlockSpec((None, 1, d), lambda i: (layer, 0, 0)),
            pl.BlockSpec((1, 3, d), lambda i: ((i * tm) // rows_per_cond, 0, 0)),
        ],
        out_specs=pl.BlockSpec((tm, d), lambda i: (i, 0)),
        compiler_params=_cparams("parallel"),
        name="postnorm",
    )(x, m, g3, mod)


def _write_mixes(h, shifted, mu_ref, out_refs):
    xx = shifted - h
    out_refs[0][...] = h.astype(BF)
    for m in range(5):
        out_refs[m + 1][...] = (h + xx * mu_ref[m:m + 1]).astype(BF)


def _shiftmix_seq_body(x_ref, g_ref, mod_ref, mu_ref, *out_refs):
    h = _modnorm(x_ref[...], g_ref[...], mod_ref[0])
    t_len, d = h.shape
    half = d // 2
    t = lax.broadcasted_iota(jnp.int32, (t_len, 1), 0)
    prev = jnp.where(t == 0, 0.0, pltpu.roll(h[:, :half], 1, 0))
    nxt = jnp.where(t == t_len - 1, 0.0, pltpu.roll(h[:, half:], t_len - 1, 0))
    _write_mixes(h, jnp.concatenate([prev, nxt], axis=1), mu_ref, out_refs)


def _shiftmix_grid_body(x_ref, up_ref, dn_ref, g_ref, mod_ref, mu_ref, *out_refs, tiles_per_batch):
    ti = pl.program_id(0) % tiles_per_batch
    g = g_ref[...]
    mod = mod_ref[0]
    h = _modnorm(x_ref[...], g, mod)
    tr, d = h.shape
    q = d // 4
    h_up = jnp.where(ti > 0, _modnorm(up_ref[...], g, mod)[:, 2 * q:3 * q], 0.0)
    h_dn = jnp.where(ti < tiles_per_batch - 1, _modnorm(dn_ref[...], g, mod)[:, 3 * q:], 0.0)
    col = lax.broadcasted_iota(jnp.int32, (tr, 1), 0) % GRID_W
    left = jnp.where(col == 0, 0.0, pltpu.roll(h[:, :q], 1, 0))
    right = jnp.where(col == GRID_W - 1, 0.0, pltpu.roll(h[:, q:2 * q], tr - 1, 0))
    up = jnp.concatenate([h_up, h[:tr - GRID_W, 2 * q:3 * q]], axis=0)
    down = jnp.concatenate([h[GRID_W:, 3 * q:], h_dn], axis=0)
    _write_mixes(h, jnp.concatenate([left, right, up, down], axis=1), mu_ref, out_refs)


def _shiftmix(x, g3, layer, mod, mu, *, seq_len, cond_rows, grid_mode, tm):
    rows, d = x.shape
    outs = [jax.ShapeDtypeStruct((rows, d), BF)] * 6
    common = [
        pl.BlockSpec((None, 1, d), lambda i: (layer, 0, 0)),
        pl.BlockSpec((1, 3, d), lambda i: ((i * tm) // cond_rows, 0, 0)),
        pl.BlockSpec((5, d), lambda i: (0, 0)),
    ]
    out_specs = [pl.BlockSpec((tm, d), lambda i: (i, 0))] * 6
    if not grid_mode:
        assert tm == seq_len
        return pl.pallas_call(
            _shiftmix_seq_body, out_shape=outs, grid=(rows // tm,),
            in_specs=[pl.BlockSpec((tm, d), lambda i: (i, 0))] + common,
            out_specs=out_specs, compiler_params=_cparams("parallel"), name="shiftmix_seq",
        )(x, g3, mod, mu)
    per = tm // GRID_W
    n_halo = rows // GRID_W
    return pl.pallas_call(
        functools.partial(_shiftmix_grid_body, tiles_per_batch=seq_len // tm),
        out_shape=outs, grid=(rows // tm,),
        in_specs=[
            pl.BlockSpec((tm, d), lambda i: (i, 0)),
            pl.BlockSpec((GRID_W, d), lambda i: (jnp.maximum(i * per - 1, 0), 0)),
            pl.BlockSpec((GRID_W, d), lambda i: (jnp.minimum((i + 1) * per, n_halo - 1), 0)),
        ] + common,
        out_specs=out_specs, compiler_params=_cparams("parallel"), name="shiftmix_grid",
    )(x, x, x, g3, mod, mu)


def _lru_gates_scan(xc, gw_ref, gb_ref, lam_ref, a_s, u_s, h, *, rb, reverse):
    tr = xc.shape[0]
    for q in range(xc.shape[1] // LB):
        ln = slice(q * LB, (q + 1) * LB)
        xq = xc[:, ln]
        xb = xq.astype(BF)

        def gate(n):
            return 0.5 * jnp.tanh(0.5 * (_dot(xb, gw_ref[n, q].astype(BF)) + gb_ref[n:n + 1, ln])) + 0.5

        r = gate(0)
        i = gate(1)
        log_a = r * (-LRU_C * jax.nn.softplus(-lam_ref[:, ln]))
        a_s[:, ln] = jnp.exp(log_a)
        th = jnp.tanh(log_a)
        u_s[:, ln] = jnp.sqrt(-2.0 * th / (1.0 - th)) * (i * xq)

    if rb % SUBLANES == 0:
        n_steps = tr // rb

        def body(s, h):
            row = pl.multiple_of(((n_steps - 1 - s) if reverse else s) * rb, rb)
            h = a_s[pl.ds(row, rb), :] * h + u_s[pl.ds(row, rb), :]
            u_s[pl.ds(row, rb), :] = h
            return h

        h = lax.fori_loop(0, n_steps, body, h, unroll=4)
    else:
        assert 2 * rb == SUBLANES
        n_steps = tr // SUBLANES
        first = lax.broadcasted_iota(jnp.int32, (SUBLANES, 1), 0) < rb

        def body(s, h):
            row = pl.multiple_of(((n_steps - 1 - s) if reverse else s) * SUBLANES, SUBLANES)
            a = a_s[pl.ds(row, SUBLANES), :]
            u = u_s[pl.ds(row, SUBLANES), :]
            h1 = a * h + u
            h2 = a * pltpu.roll(h1, rb, 0) + u
            u_s[pl.ds(row, SUBLANES), :] = jnp.where(first != reverse, h1, h2)
            return pltpu.roll(h2, rb, 0)

        h = lax.fori_loop(0, n_steps, body, h, unroll=4)
    return h


def _lru_rows_body(*refs, nb, reverse, combine, n_tiles):
    if combine:
        (x_ref, xp_ref, xn_ref, cw_ref, cb_ref, gw_ref, gb_ref, lam_ref, h0_ref, yp_ref, z_ref,
         y_ref, hf_ref, xs, a_s, u_s, hc, zs) = refs
    else:
        (x_ref, xp_ref, xn_ref, cw_ref, cb_ref, gw_ref, gb_ref, lam_ref, h0_ref,
         y_ref, hf_ref, xs, a_s, u_s, hc) = refs
    tt = x_ref.shape[1]
    ht = xp_ref.shape[1]
    tr, hb = tt * nb, ht * nb
    tiles = [slice(t * LANES, (t + 1) * LANES) for t in range(x_ref.shape[2] // LANES)]
    step = pl.program_id(2)
    t_idx = (n_tiles - 1 - step) if reverse else step

    @pl.when(step == 0)
    def _():
        hc[...] = h0_ref[...]

    for b in range(nb):
        for t, ln in enumerate(tiles):
            xs[t, pl.ds(hb + b, tt, stride=nb), :] = x_ref[b, :, ln].astype(F32)
            xs[t, pl.ds(b, ht, stride=nb), :] = jnp.where(t_idx > 0, xp_ref[b, :, ln].astype(F32), 0.0)
            xs[t, pl.ds(hb + tr + b, ht, stride=nb), :] = jnp.where(
                t_idx < n_tiles - 1, xn_ref[b, :, ln].astype(F32), 0.0)
    parts = []
    for t, ln in enumerate(tiles):
        acc = cb_ref[:, ln]
        for tap in range(CONV_W):
            off = hb + (tap - CONV_LEFT) * nb
            acc = acc + xs[t, off:off + tr, :] * cw_ref[tap:tap + 1, ln]
        parts.append(acc)
    xc = jnp.concatenate(parts, axis=1)
    h = _lru_gates_scan(xc, gw_ref, gb_ref, lam_ref, a_s, u_s, hc[...], rb=nb, reverse=reverse)
    hc[...] = h
    hf_ref[...] = h
    if not combine:
        y_ref[...] = u_s[...]
        return
    for t, ln in enumerate(tiles):
        zs[t] = yp_ref[:, ln] + u_s[:, ln]
    for b in range(nb):
        for t, ln in enumerate(tiles):
            z = z_ref[b, :, ln].astype(F32)
            y_b = zs[t, pl.ds(b, tt, stride=nb), :]
            y_ref[b, :, ln] = (y_b * (z * jax.nn.sigmoid(z))).astype(y_ref.dtype)


def _lru_core_rows(xz3, lru, j, d, h0, *, y_prev=None):
    batch, seq_len, e2 = xz3.shape
    e = e2 // 2
    nblk = e // LB
    nb = min(batch, SUBLANES)
    n_groups = batch // nb
    tt = min(LRU_TILE_ROWS // nb, seq_len)
    n_tiles = seq_len // tt
    ht = 2 * SUBLANES
    per = tt // ht
    n_halo = seq_len // ht
    tr, hb = tt * nb, ht * nb
    hr = h0.shape[0] // n_groups
    reverse = d == 1
    combine = y_prev is not None

    def tidx(s):
        return (n_tiles - 1 - s) if reverse else s

    sw = min(LRU_STEP_LANES, e)
    nsub = sw // LB
    in_specs = [
        pl.BlockSpec((nb, tt, sw), lambda c, g, s: (g, tidx(s), c)),
        pl.BlockSpec((nb, ht, sw), lambda c, g, s: (g, jnp.maximum(tidx(s) * per - 1, 0), c)),
        pl.BlockSpec((nb, ht, sw), lambda c, g, s: (g, jnp.minimum((tidx(s) + 1) * per, n_halo - 1), c)),
        pl.BlockSpec((None, CONV_W, sw), lambda c, g, s: (j, 0, c)),
        pl.BlockSpec((None, 1, sw), lambda c, g, s: (j, 0, c)),
        pl.BlockSpec((None, 2, nsub, LB, LB), lambda c, g, s: (j * 2 + d, 0, c, 0, 0)),
        pl.BlockSpec((None, 2, sw), lambda c, g, s: (j * 2 + d, 0, c)),
        pl.BlockSpec((None, 1, sw), lambda c, g, s: (j * 2 + d, 0, c)),
        pl.BlockSpec((hr, sw), lambda c, g, s: (g, c)),
    ]
    args = [xz3, xz3, xz3, lru["conv_w"], lru["conv_b"], lru["gate_w"], lru["gate_b"], lru["lam"], h0]
    lt = sw // LANES
    scratch = [pltpu.VMEM((lt, tr + 2 * hb, LANES), F32), pltpu.VMEM((tr, sw), F32),
               pltpu.VMEM((tr, sw), F32), pltpu.VMEM((hr, sw), F32)]
    if combine:
        in_specs += [
            pl.BlockSpec((None, tr, sw), lambda c, g, s: (g, tidx(s), c)),
            pl.BlockSpec((nb, tt, sw), lambda c, g, s: (g, tidx(s), e // sw + c)),
        ]
        args += [y_prev, xz3]
        scratch.append(pltpu.VMEM((lt, tr, LANES), F32))
        y_shape = jax.ShapeDtypeStruct((batch, seq_len, e), BF)
        y_spec = pl.BlockSpec((nb, tt, sw), lambda c, g, s: (g, tidx(s), c))
    else:
        y_shape = jax.ShapeDtypeStruct((n_groups, seq_len * nb, e), F32)
        y_spec = pl.BlockSpec((None, tr, sw), lambda c, g, s: (g, tidx(s), c))
    return pl.pallas_call(
        functools.partial(_lru_rows_body, nb=nb, reverse=reverse, combine=combine, n_tiles=n_tiles),
        out_shape=[y_shape, jax.ShapeDtypeStruct((hr * n_groups, e), F32)],
        grid=(e // sw, n_groups, n_tiles),
        in_specs=in_specs,
        out_specs=[y_spec, pl.BlockSpec((hr, sw), lambda c, g, s: (g, c))],
        scratch_shapes=scratch,
        compiler_params=_cparams("parallel", "parallel", "arbitrary"),
        name="lru_core_rows",
    )(*args)


def _rwkv_scan_body(*refs, reverse, zero_init, want_state, finish, n_chunks):
    it = iter(refs)
    (r_ref, k_ref, v_ref, tw_ref, ta_ref, w2_ref, w0_ref, a2_ref, a0_ref, kp_ref, bm_ref,
     tri_ref, cat_ref, lvl_ref) = (next(it) for _ in range(14))
    if finish:
        tao_ref, a2o_ref, a0o_ref, yo_ref, zg_ref, hp_ref = (next(it) for _ in range(6))
    sel_ref = next(it) if want_state else None
    z0_ref = None if zero_init else next(it)
    y_ref = next(it)
    sf_ref = next(it) if want_state else None
    z_s = next(it)
    c = pl.program_id(2)

    @pl.when(c == 0)
    def _():
        z_s[...] = jnp.zeros_like(z_s) if zero_init else z0_ref[...]

    lw_all = _act_logdecay(_dot(tw_ref[...], w2_ref[...]) + w0_ref[...])
    as_all = jax.nn.sigmoid(_dot(ta_ref[...], a2_ref[...]) + a0_ref[...])

    groups = range(z_s.shape[0])
    lanes = [slice(g * GW, (g + 1) * GW) for g in groups]
    bm = bm_ref[...]
    rs = [r_ref[:, l] for l in lanes]
    ks = [k_ref[:, l] for l in lanes]
    vs = [v_ref[:, l] for l in lanes]
    ys, z_new = _rwkv_chunk(
        rs, ks, vs, [lw_all[:, l] for l in lanes], [as_all[:, l] for l in lanes],
        [kp_ref[:, l] for l in lanes], bm, tri_ref[...], cat_ref, lvl_ref,
        [[z_s[g, t] for t in range(z_s.shape[1])] for g in groups], reverse=reverse)
    for g in groups:
        for t in range(z_s.shape[1]):
            z_s[g, t] = z_new[g][t]
    if not finish:
        for g in groups:
            y_ref[:, lanes[g]] = ys[g]
    else:
        bm_b = bm.astype(BF)
        aso_all = jax.nn.sigmoid(_dot(tao_ref[...], a2o_ref[...]) + a0o_ref[...])

        def headsum(x):
            return sum(_dot(p, bm_b) for p in _split3(x)[:2])

        ysum = [ys[g] + yo_ref[:, lanes[g]] for g in groups]
        arg = [rs[g] * ks[g] * hp_ref[1:2, lanes[g]]
               * (2.0 + hp_ref[0:1, lanes[g]] * (as_all[:, lanes[g]] + aso_all[:, lanes[g]] - 2.0))
               for g in groups]
        sums = [headsum(jnp.concatenate([ysum[g], arg[g]], axis=0)) for g in groups]
        yc = [ysum[g] - sums[g][:CH] * (1.0 / HD) for g in groups]
        var = [headsum(yc[g] * yc[g]) * (1.0 / HD) for g in groups]
        for g in groups:
            out = (yc[g] * lax.rsqrt(var[g] + GN_EPS) * hp_ref[2:3, lanes[g]] + hp_ref[3:4, lanes[g]]
                   + sums[g][CH:] * vs[g])
            zg = zg_ref[:, lanes[g]]
            y_ref[:, lanes[g]] = (out * (zg * jax.nn.sigmoid(zg))).astype(y_ref.dtype)
    if want_state:
        @pl.when(c == n_chunks - 1)
        def _():
            sel = sel_ref[...]
            for g in groups:
                for t in range(z_s.shape[1]):
                    sf_ref[g * GW + t * LANES:g * GW + (t + 1) * LANES, :] = sum(
                        _dot_tn(p, sel) for p in _split3(z_new[g][t])[:2])


def _vmap(fn, *lists):
    return [fn(*args) for args in zip(*lists)]


def _rwkv_chunk(r, k, v, lw, asg, kp, bm, tri, cat_ref, lvl_ref, z, *, reverse):
    incl = cat_ref[0]
    strict = cat_ref[1]
    eye_c = incl - strict
    bm_b = bm.astype(BF)

    def headsum(x):
        return sum(_dot(p, bm_b) for p in _split3(x)[:2])

    lane_head = lax.broadcasted_iota(jnp.int32, (1, LANES), 1) // HD
    zero_tile = jnp.zeros((CH, LANES), BF)

    def bdm(x):
        xb = x.astype(BF)
        blocks = []
        for h in range(GH):
            j = h // HEADS_PER_TILE
            tile = xb[:, j * LANES:(j + 1) * LANES]
            keep = jnp.where(lane_head == h % HEADS_PER_TILE, tile, jnp.zeros_like(tile))
            blocks.append(jnp.concatenate([keep if t == j else zero_tile for t in range(GW // LANES)],
                                          axis=1))
        return jnp.concatenate(blocks, axis=0)

    def mm(a, b_bd):
        return _vmap(lambda x, y: _dot(x.astype(BF), y), a, b_bd)

    def stack2(a, b):
        return _vmap(lambda x, y: jnp.concatenate([x, y], axis=0), a, b)

    def top(a):
        return [x[:CH] for x in a]

    def bot(a):
        return [x[CH:] for x in a]

    kk = _vmap(lambda k_, p: k_ * p[0:1], k, kp)
    ss = _vmap(lambda x: headsum(x * x), kk)
    kk = _vmap(lambda x, s: x * lax.rsqrt(jnp.maximum(s, 1e-24)), kk, ss)
    kd = _vmap(lambda k_, a, p: k_ * (1.0 + (a - 1.0) * p[1:2]), k, asg, kp)
    bvec = _vmap(lambda x, a: x * a, kk, asg)

    cum = _vmap(lambda x: sum(_dot(tri, p) for p in _split3(x)[:2]), lw)
    tot = _vmap(lambda x: x[0:1] if reverse else x[CH - 1:CH], cum)
    e_pos = _vmap(jnp.exp, cum)
    e_neg = _vmap(lambda x: jnp.exp(-x), cum)
    e_rem = _vmap(lambda t, en: jnp.exp(t) * en, tot, e_neg)
    a_t = _vmap(lambda x, ep, l: -x * (ep * jnp.exp(-l)), kk, e_pos, lw)
    r_t = _vmap(lambda x, ep: x * ep, r, e_pos)
    ar = [x.astype(BF) for x in stack2(a_t, r_t)]
    bt_bd = _vmap(lambda x, en: bdm(x * en), bvec, e_neg)
    kt_bd = _vmap(lambda x, en: bdm(x * en), kd, e_neg)
    keep2 = jnp.concatenate([strict, incl], axis=0) > 0
    ab = [jnp.where(keep2, m, 0.0) for m in _vmap(_dot_nt, ar, bt_bd)]
    ak = [jnp.where(keep2, m, 0.0).astype(BF) for m in _vmap(_dot_nt, ar, kt_bd)]
    a_ab = top(ab)
    a_rb = [m.astype(BF) for m in bot(ab)]

    p = [n * lvl_ref[0] for n in a_ab]
    x = [eye_c + n for n in p]
    p = [n.astype(BF) for n in p]
    pw = [m.astype(BF) for m in mm(p, _vmap(bdm, p))]
    e = 2
    while e < INV_BASE:
        rhs = _vmap(bdm, pw)
        if 2 * e < INV_BASE:
            both = mm(stack2([m.astype(BF) for m in x], pw), rhs)
            x = _vmap(jnp.add, x, top(both))
            pw = [m.astype(BF) for m in bot(both)]
        else:
            x = _vmap(jnp.add, x, mm(x, rhs))
        e *= 2
    for lvl in range(1, lvl_ref.shape[0]):
        off = [(n * lvl_ref[lvl]).astype(BF) for n in a_ab]
        xb = [m.astype(BF) for m in x]
        x = _vmap(jnp.add, x, mm(xb, _vmap(bdm, mm(off, _vmap(bdm, xb)))))
    xb = [m.astype(BF) for m in x]

    vb = [m.astype(BF) for m in v]
    s_v = mm(ak, _vmap(bdm, vb))
    def z_rhs(zq):
        n_t = len(zq)
        zero = jnp.zeros((LANES, LANES), BF)
        return jnp.concatenate(
            [jnp.concatenate([zq[i].astype(BF) if i == t else zero for t in range(n_t)], axis=1)
             for i in range(n_t)], axis=0)

    s_z = _vmap(lambda a, zq: _dot(a, z_rhs(zq)), ar, z)
    u = mm(xb, _vmap(lambda a, b_: bdm(a + b_), top(s_z), top(s_v)))
    ub = [m.astype(BF) for m in u]
    y =_vmap(lambda a, b_, c_: a + b_ + c_, bot(s_z), bot(s_v), mm(a_rb, _vmap(bdm, ub)))
    row_id = lax.broadcasted_iota(jnp.int32, (LANES, 1), 0)
    lane_id = lax.broadcasted_iota(jnp.int32, (1, LANES), 1)
    same_head = (row_id // HD) == (lane_id // HD)
    diag = row_id == lane_id
    lhs = _vmap(lambda b_, k_, er: jnp.concatenate([b_ * er, k_ * er], axis=0).astype(BF), bvec, kd, e_rem)
    rhs = stack2(ub, vb)
    decay = _vmap(jnp.exp, tot)
    z_new = [[] for _ in z]
    for t in range(GW // LANES):
        cols = slice(t * LANES, (t + 1) * LANES)
        for g in range(len(z)):
            upd = jnp.where(same_head, _dot_tn(lhs[g][:, cols], rhs[g][:, cols]), 0.0)
            g_col = jnp.sum(jnp.where(diag, decay[g][:, cols], 0.0), axis=1, keepdims=True)
            z_new[g].append(z[g][t] * g_col + upd)
    return y, z_new


def _rwkv_scan(r, k, v, tw, ta, lora, jd, kp, consts, d, *, batch, seq_len, z0=None, want_state=False,
               finish=None):
    rows, e = r.shape
    ngs = min(SCAN_GROUPS, e // GW)
    sw = ngs * GW
    ng = e // sw
    nc = seq_len // CH
    reverse = d == 1
    zero_init = z0 is None
    rank = lora["w2"].shape[1]

    def cidx(c):
        return (nc - 1 - c) if reverse else c

    def lora_specs(dd, jdd):
        return [pl.BlockSpec((CH, rank), lambda b, g, c: (b * nc + cidx(c), dd)),
                pl.BlockSpec((None, rank, sw), lambda b, g, c: (jdd, 0, g)),
                pl.BlockSpec((None, 1, sw), lambda b, g, c: (jdd, 0, g))]

    tok = pl.BlockSpec((CH, sw), lambda b, g, c: (b * nc + cidx(c), g))
    w_specs, a_specs = lora_specs(d, jd), lora_specs(d, jd)
    in_specs = [tok, tok, tok, w_specs[0], a_specs[0], w_specs[1], w_specs[2], a_specs[1], a_specs[2],
                pl.BlockSpec((2, sw), lambda b, g, c: (0, g)),
                pl.BlockSpec((GW, GW), lambda b, g, c: (0, 0)),
                pl.BlockSpec((None, CH, CH), lambda b, g, c: (d, 0, 0)),
                pl.BlockSpec((None, 2, CH, GW), lambda b, g, c: (d, 0, 0, 0)),
                pl.BlockSpec(consts["lvl"].shape, lambda b, g, c: (0, 0, 0))]
    args = [r, k, v, tw, ta, lora["w2"], lora["w0"], lora["a2"], lora["a0"], kp, consts["bm"],
            consts["tri"], consts["cat"], consts["lvl"]]
    if finish is not None:
        y_other, zg, hp = finish
        in_specs += lora_specs(1 - d, jd + 1 - 2 * d) + [tok, tok, pl.BlockSpec((4, sw), lambda b, g, c: (0, g))]
        args += [ta, lora["a2"], lora["a0"], y_other, zg, hp]
    if want_state:
        in_specs.append(pl.BlockSpec((LANES, HD), lambda b, g, c: (0, 0)))
        args.append(consts["sel"])
    if not zero_init:
        in_specs.append(pl.BlockSpec((None, ngs, GW // LANES, LANES, LANES), lambda b, g, c: (b, g, 0, 0, 0)))
        args.append(z0)
    out_shape = [jax.ShapeDtypeStruct((rows, e), F32 if finish is None else BF)]
    out_specs = [tok]
    if want_state:
        out_shape.append(jax.ShapeDtypeStruct((batch, e, HD), F32))
        out_specs.append(pl.BlockSpec((None, sw, HD), lambda b, g, c: (b, g, 0)))
    res = pl.pallas_call(
        functools.partial(_rwkv_scan_body, reverse=reverse, zero_init=zero_init,
                          want_state=want_state, finish=finish is not None, n_chunks=nc),
        out_shape=out_shape,
        grid=(batch, ng, nc),
        in_specs=in_specs,
        out_specs=out_specs,
        scratch_shapes=[pltpu.VMEM((ngs, GW // LANES, LANES, LANES), F32)],
        compiler_params=_cparams("parallel", "parallel", "arbitrary"),
        name="rwkv_scan",
    )(*args)
    return res if want_state else (res[0], None)


def _silu(x):
    return x * jax.nn.sigmoid(x)


def _act_tanh(x):
    return jnp.tanh(x)


def _act_sigmoid(x):
    return jax.nn.sigmoid(x)


def _act_logdecay(x):
    return -jnp.exp(-0.5) * jax.nn.sigmoid(x)


def _act_vmix(acc, v_first, mix):
    return acc + (v_first - acc) * mix


def _trunk(x, mods, init_lru, init_rwkv, W, *, batch, seq_len, grid_mode, want_state):
    rows, d = x.shape
    e = W["lru_w_out"].shape[1]
    depth = W["norm_pre"].shape[0]
    tm_row = min(NORM_TILE_ROWS, seq_len)
    tm_mm = min(MM_TILE_ROWS, rows)
    mk = dict(tm=tm_mm, m_tiles=rows // tm_mm)
    cond_rows = rows // mods[0].shape[0]
    fin_lru, fin_rwkv = [], []
    v_first = None

    for i in range(depth):
        mod = mods[i]
        j = i // 2
        if i % 2 == 0:
            h = _prenorm(x, W["norm_pre"], i, mod, tm=tm_row, rows_per_cond=cond_rows)
            xz = _mm(h, W["lru_w_in"], j, None, tn=MM_TN, out_dtype=BF, name="lru_in", **mk)
            xz = xz.reshape(batch, seq_len, 2 * e)
            y0, hf0 = _lru_core_rows(xz, W["lru"], j, 0, init_lru[j][0])
            y, hf1 = _lru_core_rows(xz, W["lru"], j, 1, init_lru[j][1], y_prev=y0)
            m = _mm(y.reshape(rows, e), W["lru_w_out"], j, None, tn=MM_TN, name="lru_out", **mk)
            fin_lru.append((hf0, hf1))
        else:
            h, xr, xw, xk, xv, xa = _shiftmix(
                x, W["norm_pre"], i, mod, W["rwkv_mu"][j], seq_len=seq_len, cond_rows=cond_rows, grid_mode=grid_mode,
                tm=seq_len if not grid_mode else min(SHIFT_TILE_ROWS, seq_len))
            r = _mm(xr, W["rwkv_w_r"], j, None, tn=MM_TN, name="rwkv_r", **mk)
            k = _mm(xk, W["rwkv_w_k"], j, None, tn=MM_TN, name="rwkv_k", **mk)
            z = _mm(h, W["rwkv_w_g"], j, None, tn=MM_TN, name="rwkv_g", **mk)
            if j == 0:
                v = _mm(xv, W["rwkv_w_v"], j, None, tn=MM_TN, name="rwkv_v", **mk)
                v_first = v
            else:
                tv = _mm(xv, W["rwkv_v1"], j - 1, None, tn=MM_TN, out_dtype=BF, name="rwkv_v1", **mk)
                mix = _mm(tv, W["rwkv_v2"], j - 1, W["rwkv_v0"][j - 1], tn=MM_TN, act=_act_sigmoid,
                          name="rwkv_v2", **mk)
                v = _mm(xv, W["rwkv_w_v"], j, None, tn=MM_TN, extras=(v_first, mix), act=_act_vmix,
                        name="rwkv_v", **mk)
            tw = _mm(xw, W["rwkv_w1"], j, None, tn=MM_TN, act=_act_tanh, out_dtype=BF, name="rwkv_w1", **mk)
            ta = _mm(xa, W["rwkv_a1"], j, None, tn=MM_TN, out_dtype=BF, name="rwkv_a1", **mk)
            fins = []
            y_d = None
            for dd in range(2):
                z0 = None if init_rwkv is None else init_rwkv[j][dd]
                finish = None if dd == 0 else (y_d, z, W["rwkv_hp"][j])
                y_d, s_d = _rwkv_scan(r, k, v, tw, ta, W["rwkv_lora"], j * 2 + dd, W["rwkv_kp"][j],
                                      W["consts"], dd, batch=batch, seq_len=seq_len, z0=z0,
                                      want_state=want_state, finish=finish)
                fins.append(s_d)
            fin_rwkv.append(fins)
            m = _mm(y_d, W["rwkv_w_o"], j, None, tn=MM_TN, name="rwkv_o", **mk)
        x = _postnorm(x, m, W["norm_post"], i, mod, tm=tm_row, rows_per_cond=cond_rows)
    return x, fin_lru, fin_rwkv


def _scan_consts():
    idx = jnp.arange(GW)
    bm = ((idx[:, None] // HD) == (idx[None, :] // HD)).astype(F32)
    t_r, t_c = jnp.arange(CH)[:, None], idx[None, :] % CH
    cats, tris = [], []
    ic = jnp.arange(CH)
    for d in range(2):
        ge = (t_r >= t_c) if d == 0 else (t_r <= t_c)
        gt = (t_r > t_c) if d == 0 else (t_r < t_c)
        cats.append(jnp.stack([ge.astype(F32), gt.astype(F32)]))
        tris.append(((ic[:, None] >= ic[None, :]) if d == 0 else (ic[:, None] <= ic[None, :])).astype(BF))
    sel = (idx[:, None] % HD == jnp.arange(HD)[None, :]).astype(BF)
    lvl = [t_r // INV_BASE == t_c // INV_BASE]
    size = INV_BASE
    while size < CH:
        lvl.append((t_r // (2 * size) == t_c // (2 * size)) & (t_r // size != t_c // size))
        size *= 2
    return dict(bm=bm, cat=jnp.stack(cats), tri=jnp.stack(tris), sel=sel,
                lvl=jnp.stack(lvl).astype(F32))


def kernel(x_prompt, x_sample, state_lru, state_rwkv, c, c_ctx, ada_w, ada_b, norm_pre, norm_post, lru_w_in, lru_conv_w, lru_conv_b, lru_gate_w, lru_gate_b, lru_lambda, lru_w_out, rwkv_mu, rwkv_w_r, rwkv_w_k, rwkv_w_v, rwkv_w_g, rwkv_w_o, rwkv_w0, rwkv_w1, rwkv_w2, rwkv_a0, rwkv_a1, rwkv_a2, rwkv_k_k, rwkv_k_a, rwkv_r_k, rwkv_ln_w, rwkv_ln_b, rwkv_v0, rwkv_v1, rwkv_v2):
    bp, tp, d = x_prompt.shape
    bs, ts, _ = x_sample.shape
    depth = ada_w.shape[0]
    n_lru, _, e2 = lru_w_in.shape
    e = e2 // 2
    n_rwkv = rwkv_w_r.shape[0]
    nblk = e // LB
    heads = e // HD
    ng = e // GW

    def merge_lead(w, k):
        return w.reshape((-1,) + w.shape[k:])

    lora_v = rwkv_v1.shape[-1]
    pad_v = (-lora_v) % 128
    W = dict(
        norm_pre=norm_pre.reshape(depth, 1, d), norm_post=norm_post.reshape(depth, 1, d),
        lru_w_in=lru_w_in.astype(BF), lru_w_out=lru_w_out.astype(BF),
        lru=dict(conv_w=lru_conv_w, conv_b=lru_conv_b.reshape(n_lru, 1, e),
                 gate_w=merge_lead(lru_gate_w, 2),
                 gate_b=merge_lead(lru_gate_b, 2),
                 lam=lru_lambda.reshape(n_lru * 2, 1, e)),
        rwkv_mu=rwkv_mu,
        rwkv_w_r=rwkv_w_r.astype(BF), rwkv_w_k=rwkv_w_k.astype(BF), rwkv_w_v=rwkv_w_v.astype(BF),
        rwkv_w_g=rwkv_w_g.astype(BF), rwkv_w_o=rwkv_w_o.astype(BF),
        rwkv_w1=jnp.concatenate([rwkv_w1[:, 0], rwkv_w1[:, 1]], axis=-1).astype(BF),
        rwkv_a1=jnp.concatenate([rwkv_a1[:, 0], rwkv_a1[:, 1]], axis=-1).astype(BF),
        rwkv_lora=dict(w2=merge_lead(rwkv_w2, 2).astype(BF), a2=merge_lead(rwkv_a2, 2).astype(BF),
                       w0=rwkv_w0.reshape(n_rwkv * 2, 1, e), a0=rwkv_a0.reshape(n_rwkv * 2, 1, e)),
        rwkv_v0=rwkv_v0.reshape(-1, 1, e),
        rwkv_v1=jnp.pad(rwkv_v1, ((0, 0), (0, 0), (0, pad_v))).astype(BF),
        rwkv_v2=jnp.pad(rwkv_v2, ((0, 0), (0, pad_v), (0, 0))).astype(BF),
        rwkv_kp=jnp.stack([rwkv_k_k, rwkv_k_a], axis=1),
        rwkv_hp=jnp.stack([rwkv_k_a, rwkv_r_k, rwkv_ln_w, rwkv_ln_b], axis=1),
        consts=_scan_consts(),
    )

    n_cond = 1 + bs
    cond = jnp.zeros((SUBLANES * pl.cdiv(n_cond, SUBLANES), d), F32)
    cond = cond.at[0].set(c_ctx).at[1:n_cond].set(c)
    ada_wb = ada_w.astype(BF)
    mods_p, mods_s = [], []
    for i in range(depth):
        mod = _mm(cond, ada_wb, i, ada_b[i][None], tm=cond.shape[0], tn=256, m_tiles=1,
                  a_pre=_silu, name="adaln")
        mod = mod.reshape(cond.shape[0], 3, d)
        mods_p.append(mod[0:1])
        mods_s.append(mod[1:n_cond])

    zero_lru = [(jnp.zeros((bp, e), F32), jnp.zeros((bp, e), F32))] * n_lru
    y_p, fin_lru, fin_rwkv = _trunk(x_prompt.reshape(bp * tp, d), mods_p, zero_lru, None, W,
                                    batch=bp, seq_len=tp, grid_mode=False, want_state=True)
    new_state_lru = jnp.stack([jnp.stack(f, axis=1) for f in fin_lru], axis=1)
    new_state_rwkv = jnp.stack(
        [jnp.stack([s.reshape(bp, heads, HD, HD) for s in f], axis=1) for f in fin_rwkv], axis=1)

    rep = max(SUBLANES // bs, 1)
    init_lru = [tuple(jnp.tile(state_lru[:, jj, dd], (rep, 1)) for dd in range(2)) for jj in range(n_lru)]
    eye_t = jnp.eye(HEADS_PER_TILE, dtype=F32)
    n_tiles = e // LANES
    st = state_rwkv.reshape(bs, n_rwkv, 2, n_tiles, HEADS_PER_TILE, HD, HD)
    z0 = jnp.einsum("bldthvk,hi->bldthkiv", st, eye_t).reshape(
        bs, n_rwkv, 2, ng, GW // LANES, LANES, LANES)
    init_rwkv = [[z0[:, jj, dd] for dd in range(2)] for jj in range(n_rwkv)]
    y_s, _, _ = _trunk(x_sample.reshape(bs * ts, d), mods_s, init_lru, init_rwkv, W,
                       batch=bs, seq_len=ts, grid_mode=True, want_state=False)
    return (y_p.reshape(bp, tp, d), y_s.reshape(bs, ts, d), new_state_lru, new_state_rwkv)
```

```python
import functools

import jax
import jax.numpy as jnp
from jax import lax
from jax.experimental import pallas as pl
from jax.experimental.pallas import tpu as pltpu

F32 = jnp.float32
BF = jnp.bfloat16

NORM_EPS = 1e-6
GN_EPS = 64e-5
LRU_C = 8.0
GRID_W = 64
CONV_W = 4
CONV_LEFT = 2

HD = 64
GH = 4
GW = GH * HD
CH = 64
INV_BASE = 16
SCAN_GROUPS = 16
LB = 256
SUBLANES = 8
LANES = 128
HEADS_PER_TILE = LANES // HD
VMEM_LIMIT = 56 * 1024 * 1024
MM_VMEM_BUDGET = 44 * 1024 * 1024
MM_TILE_ROWS = 1024
MM_TN = 1024
NORM_TILE_ROWS = 256
SHIFT_TILE_ROWS = 512
LRU_TILE_ROWS = 512
LRU_STEP_LANES = 2048


def _cparams(*sem):
    return pltpu.CompilerParams(dimension_semantics=sem, vmem_limit_bytes=VMEM_LIMIT)


def _dot(a, b):
    return jnp.dot(a, b, preferred_element_type=F32)


def _dot_nt(a, b):
    return lax.dot_general(a, b, (((1,), (1,)), ((), ())), preferred_element_type=F32)


def _dot_tn(a, b):
    return lax.dot_general(a, b, (((0,), (0,)), ((), ())), preferred_element_type=F32)


def _split3(x):
    h1 = x.astype(BF)
    r1 = x - h1.astype(F32)
    h2 = r1.astype(BF)
    h3 = (r1 - h2.astype(F32)).astype(BF)
    return h1, h2, h3


def _modnorm(x, g, mod):
    y = x * lax.rsqrt(jnp.mean(x * x, axis=-1, keepdims=True) + NORM_EPS) * g
    return y * (1.0 + mod[1:2]) + mod[0:1]


def _mm_body(*refs, act, a_pre, n_extra):
    a_ref, w_ref, b_ref = refs[:3]
    extras = refs[3:3 + n_extra]
    o_ref = refs[3 + n_extra]
    a = a_ref[...]
    if a_pre is not None:
        a = a_pre(a.astype(F32))
    acc = _dot(a.astype(BF), w_ref[...]) + b_ref[...]
    if act is not None:
        acc = act(acc, *[e[...] for e in extras])
    o_ref[...] = acc.astype(o_ref.dtype)


def _mm_tn(k, n, tm, tn_max, n_extra, out_itemsize):
    tn = min(tn_max, n)
    while tn > LANES:
        w_bytes = 2 * k * tn * 2
        io_bytes = 2 * tm * (k * 2 + tn * (out_itemsize + 4 * n_extra))
        if n % tn == 0 and w_bytes + io_bytes <= MM_VMEM_BUDGET:
            break
        tn //= 2
    return tn


def _mm(a, w3, li, bias, *, tm, tn, m_tiles, extras=(), act=None, a_pre=None, out_dtype=F32, name="mm"):
    _, K, N = w3.shape
    tn = _mm_tn(K, N, tm, tn, len(extras), jnp.dtype(out_dtype).itemsize)
    assert N % tn == 0 and K % LANES == 0
    if bias is None:
        bias = jnp.zeros((1, N), F32)
    in_specs = [
        pl.BlockSpec((tm, K), lambda n, m: (m, 0)),
        pl.BlockSpec((None, K, tn), lambda n, m: (li, 0, n)),
        pl.BlockSpec((1, tn), lambda n, m: (0, n)),
    ] + [pl.BlockSpec((tm, tn), lambda n, m: (m, n)) for _ in extras]
    return pl.pallas_call(
        functools.partial(_mm_body, act=act, a_pre=a_pre, n_extra=len(extras)),
        out_shape=jax.ShapeDtypeStruct((m_tiles * tm, N), out_dtype),
        grid=(N // tn, m_tiles),
        in_specs=in_specs,
        out_specs=pl.BlockSpec((tm, tn), lambda n, m: (m, n)),
        compiler_params=_cparams("parallel", "parallel"),
        name=name,
    )(a, w3, bias, *extras)


def _prenorm_body(x_ref, g_ref, mod_ref, o_ref):
    o_ref[...] = _modnorm(x_ref[...], g_ref[...], mod_ref[0]).astype(o_ref.dtype)


def _prenorm(x, g3, layer, mod, *, tm, rows_per_cond):
    rows, d = x.shape
    return pl.pallas_call(
        _prenorm_body,
        out_shape=jax.ShapeDtypeStruct((rows, d), BF),
        grid=(rows // tm,),
        in_specs=[
            pl.BlockSpec((tm, d), lambda i: (i, 0)),
            pl.BlockSpec((None, 1, d), lambda i: (layer, 0, 0)),
            pl.BlockSpec((1, 3, d), lambda i: ((i * tm) // rows_per_cond, 0, 0)),
        ],
        out_specs=pl.BlockSpec((tm, d), lambda i: (i, 0)),
        compiler_params=_cparams("parallel"),
        name="prenorm",
    )(x, g3, mod)


def _postnorm_body(x_ref, m_ref, g_ref, mod_ref, o_ref):
    m = m_ref[...]
    y = m * lax.rsqrt(jnp.mean(m * m, axis=-1, keepdims=True) + NORM_EPS) * g_ref[...]
    o_ref[...] = x_ref[...] + mod_ref[0][2:3] * y


def _postnorm(x, m, g3, layer, mod, *, tm, rows_per_cond):
    rows, d = x.shape
    return pl.pallas_call(
        _postnorm_body,
        out_shape=jax.ShapeDtypeStruct((rows, d), F32),
        grid=(rows // tm,),
        in_specs=[
            pl.BlockSpec((tm, d), lambda i: (i, 0)),
            pl.BlockSpec((tm, d), lambda i: (i, 0)),
            pl.BlockSpec((None, 1, d), lambda i: (layer, 0, 0)),
            pl.BlockSpec((1, 3, d), lambda i: ((i * tm) // rows_per_cond, 0, 0)),
        ],
        out_specs=pl.BlockSpec((tm, d), lambda i: (i, 0)),
        compiler_params=_cparams("parallel"),
        name="postnorm",
    )(x, m, g3, mod)


def _write_mixes(h, shifted, mu_ref, out_refs):
    xx = shifted - h
    out_refs[0][...] = h.astype(BF)
    for m in range(5):
        out_refs[m + 1][...] = (h + xx * mu_ref[m:m + 1]).astype(BF)


def _shiftmix_seq_body(x_ref, g_ref, mod_ref, mu_ref, *out_refs):
    h = _modnorm(x_ref[...], g_ref[...], mod_ref[0])
    t_len, d = h.shape
    half = d // 2
    t = lax.broadcasted_iota(jnp.int32, (t_len, 1), 0)
    prev = jnp.where(t == 0, 0.0, pltpu.roll(h[:, :half], 1, 0))
    nxt = jnp.where(t == t_len - 1, 0.0, pltpu.roll(h[:, half:], t_len - 1, 0))
    _write_mixes(h, jnp.concatenate([prev, nxt], axis=1), mu_ref, out_refs)


def _shiftmix_grid_body(x_ref, up_ref, dn_ref, g_ref, mod_ref, mu_ref, *out_refs, tiles_per_batch):
    ti = pl.program_id(0) % tiles_per_batch
    g = g_ref[...]
    mod = mod_ref[0]
    h = _modnorm(x_ref[...], g, mod)
    tr, d = h.shape
    q = d // 4
    h_up = jnp.where(ti > 0, _modnorm(up_ref[...], g, mod)[:, 2 * q:3 * q], 0.0)
    h_dn = jnp.where(ti < tiles_per_batch - 1, _modnorm(dn_ref[...], g, mod)[:, 3 * q:], 0.0)
    col = lax.broadcasted_iota(jnp.int32, (tr, 1), 0) % GRID_W
    left = jnp.where(col == 0, 0.0, pltpu.roll(h[:, :q], 1, 0))
    right = jnp.where(col == GRID_W - 1, 0.0, pltpu.roll(h[:, q:2 * q], tr - 1, 0))
    up = jnp.concatenate([h_up, h[:tr - GRID_W, 2 * q:3 * q]], axis=0)
    down = jnp.concatenate([h[GRID_W:, 3 * q:], h_dn], axis=0)
    _write_mixes(h, jnp.concatenate([left, right, up, down], axis=1), mu_ref, out_refs)


def _shiftmix(x, g3, layer, mod, mu, *, seq_len, cond_rows, grid_mode, tm):
    rows, d = x.shape
    outs = [jax.ShapeDtypeStruct((rows, d), BF)] * 6
    common = [
        pl.BlockSpec((None, 1, d), lambda i: (layer, 0, 0)),
        pl.BlockSpec((1, 3, d), lambda i: ((i * tm) // cond_rows, 0, 0)),
        pl.BlockSpec((5, d), lambda i: (0, 0)),
    ]
    out_specs = [pl.BlockSpec((tm, d), lambda i: (i, 0))] * 6
    if not grid_mode:
        assert tm == seq_len
        return pl.pallas_call(
            _shiftmix_seq_body, out_shape=outs, grid=(rows // tm,),
            in_specs=[pl.BlockSpec((tm, d), lambda i: (i, 0))] + common,
            out_specs=out_specs, compiler_params=_cparams("parallel"), name="shiftmix_seq",
        )(x, g3, mod, mu)
    per = tm // GRID_W
    n_halo = rows // GRID_W
    return pl.pallas_call(
        functools.partial(_shiftmix_grid_body, tiles_per_batch=seq_len // tm),
        out_shape=outs, grid=(rows // tm,),
        in_specs=[
            pl.BlockSpec((tm, d), lambda i: (i, 0)),
            pl.BlockSpec((GRID_W, d), lambda i: (jnp.maximum(i * per - 1, 0), 0)),
            pl.BlockSpec((GRID_W, d), lambda i: (jnp.minimum((i + 1) * per, n_halo - 1), 0)),
        ] + common,
        out_specs=out_specs, compiler_params=_cparams("parallel"), name="shiftmix_grid",
    )(x, x, x, g3, mod, mu)


def _lru_gates_scan(xc, gw_ref, gb_ref, lam_ref, a_s, u_s, h, *, rb, reverse):
    tr = xc.shape[0]
    for q in range(xc.shape[1] // LB):
        ln = slice(q * LB, (q + 1) * LB)
        xq = xc[:, ln]
        xb = xq.astype(BF)

        def gate(n):
            return 0.5 * jnp.tanh(0.5 * (_dot(xb, gw_ref[n, q].astype(BF)) + gb_ref[n:n + 1, ln])) + 0.5

        r = gate(0)
        i = gate(1)
        log_a = r * (-LRU_C * jax.nn.softplus(-lam_ref[:, ln]))
        a_s[:, ln] = jnp.exp(log_a)
        th = jnp.tanh(log_a)
        u_s[:, ln] = jnp.sqrt(-2.0 * th / (1.0 - th)) * (i * xq)

    if rb % SUBLANES == 0:
        n_steps = tr // rb

        def body(s, h):
            row = pl.multiple_of(((n_steps - 1 - s) if reverse else s) * rb, rb)
            h = a_s[pl.ds(row, rb), :] * h + u_s[pl.ds(row, rb), :]
            u_s[pl.ds(row, rb), :] = h
            return h

        h = lax.fori_loop(0, n_steps, body, h, unroll=4)
    else:
        assert 2 * rb == SUBLANES
        n_steps = tr // SUBLANES
        first = lax.broadcasted_iota(jnp.int32, (SUBLANES, 1), 0) < rb

        def body(s, h):
            row = pl.multiple_of(((n_steps - 1 - s) if reverse else s) * SUBLANES, SUBLANES)
            a = a_s[pl.ds(row, SUBLANES), :]
            u = u_s[pl.ds(row, SUBLANES), :]
            h1 = a * h + u
            h2 = a * pltpu.roll(h1, rb, 0) + u
            u_s[pl.ds(row, SUBLANES), :] = jnp.where(first != reverse, h1, h2)
            return pltpu.roll(h2, rb, 0)

        h = lax.fori_loop(0, n_steps, body, h, unroll=4)
    return h


def _lru_rows_body(*refs, nb, reverse, combine, n_tiles):
    if combine:
        (x_ref, xp_ref, xn_ref, cw_ref, cb_ref, gw_ref, gb_ref, lam_ref, h0_ref, yp_ref, z_ref,
         y_ref, hf_ref, xs, a_s, u_s, hc, zs) = refs
    else:
        (x_ref, xp_ref, xn_ref, cw_ref, cb_ref, gw_ref, gb_ref, lam_ref, h0_ref,
         y_ref, hf_ref, xs, a_s, u_s, hc) = refs
    tt = x_ref.shape[1]
    ht = xp_ref.shape[1]
    tr, hb = tt * nb, ht * nb
    tiles = [slice(t * LANES, (t + 1) * LANES) for t in range(x_ref.shape[2] // LANES)]
    step = pl.program_id(2)
    t_idx = (n_tiles - 1 - step) if reverse else step

    @pl.when(step == 0)
    def _():
        hc[...] = h0_ref[...]

    for b in range(nb):
        for t, ln in enumerate(tiles):
            xs[t, pl.ds(hb + b, tt, stride=nb), :] = x_ref[b, :, ln].astype(F32)
            xs[t, pl.ds(b, ht, stride=nb), :] = jnp.where(t_idx > 0, xp_ref[b, :, ln].astype(F32), 0.0)
            xs[t, pl.ds(hb + tr + b, ht, stride=nb), :] = jnp.where(
                t_idx < n_tiles - 1, xn_ref[b, :, ln].astype(F32), 0.0)
    parts = []
    for t, ln in enumerate(tiles):
        acc = cb_ref[:, ln]
        for tap in range(CONV_W):
            off = hb + (tap - CONV_LEFT) * nb
            acc = acc + xs[t, off:off + tr, :] * cw_ref[tap:tap + 1, ln]
        parts.append(acc)
    xc = jnp.concatenate(parts, axis=1)
    h = _lru_gates_scan(xc, gw_ref, gb_ref, lam_ref, a_s, u_s, hc[...], rb=nb, reverse=reverse)
    hc[...] = h
    hf_ref[...] = h
    if not combine:
        y_ref[...] = u_s[...]
        return
    for t, ln in enumerate(tiles):
        zs[t] = yp_ref[:, ln] + u_s[:, ln]
    for b in range(nb):
        for t, ln in enumerate(tiles):
            z = z_ref[b, :, ln].astype(F32)
            y_b = zs[t, pl.ds(b, tt, stride=nb), :]
            y_ref[b, :, ln] = (y_b * (z * jax.nn.sigmoid(z))).astype(y_ref.dtype)


def _lru_core_rows(xz3, lru, j, d, h0, *, y_prev=None):
    batch, seq_len, e2 = xz3.shape
    e = e2 // 2
    nblk = e // LB
    nb = min(batch, SUBLANES)
    n_groups = batch // nb
    tt = min(LRU_TILE_ROWS // nb, seq_len)
    n_tiles = seq_len // tt
    ht = 2 * SUBLANES
    per = tt // ht
    n_halo = seq_len // ht
    tr, hb = tt * nb, ht * nb
    hr = h0.shape[0] // n_groups
    reverse = d == 1
    combine = y_prev is not None

    def tidx(s):
        return (n_tiles - 1 - s) if reverse else s

    sw = min(LRU_STEP_LANES, e)
    nsub = sw // LB
    in_specs = [
        pl.BlockSpec((nb, tt, sw), lambda c, g, s: (g, tidx(s), c)),
        pl.BlockSpec((nb, ht, sw), lambda c, g, s: (g, jnp.maximum(tidx(s) * per - 1, 0), c)),
        pl.BlockSpec((nb, ht, sw), lambda c, g, s: (g, jnp.minimum((tidx(s) + 1) * per, n_halo - 1), c)),
        pl.BlockSpec((None, CONV_W, sw), lambda c, g, s: (j, 0, c)),
        pl.BlockSpec((None, 1, sw), lambda c, g, s: (j, 0, c)),
        pl.BlockSpec((None, 2, nsub, LB, LB), lambda c, g, s: (j * 2 + d, 0, c, 0, 0)),
        pl.BlockSpec((None, 2, sw), lambda c, g, s: (j * 2 + d, 0, c)),
        pl.BlockSpec((None, 1, sw), lambda c, g, s: (j * 2 + d, 0, c)),
        pl.BlockSpec((hr, sw), lambda c, g, s: (g, c)),
    ]
    args = [xz3, xz3, xz3, lru["conv_w"], lru["conv_b"], lru["gate_w"], lru["gate_b"], lru["lam"], h0]
    lt = sw // LANES
    scratch = [pltpu.VMEM((lt, tr + 2 * hb, LANES), F32), pltpu.VMEM((tr, sw), F32),
               pltpu.VMEM((tr, sw), F32), pltpu.VMEM((hr, sw), F32)]
    if combine:
        in_specs += [
            pl.BlockSpec((None, tr, sw), lambda c, g, s: (g, tidx(s), c)),
            pl.BlockSpec((nb, tt, sw), lambda c, g, s: (g, tidx(s), e // sw + c)),
        ]
        args += [y_prev, xz3]
        scratch.append(pltpu.VMEM((lt, tr, LANES), F32))
        y_shape = jax.ShapeDtypeStruct((batch, seq_len, e), BF)
        y_spec = pl.BlockSpec((nb, tt, sw), lambda c, g, s: (g, tidx(s), c))
    else:
        y_shape = jax.ShapeDtypeStruct((n_groups, seq_len * nb, e), F32)
        y_spec = pl.BlockSpec((None, tr, sw), lambda c, g, s: (g, tidx(s), c))
    return pl.pallas_call(
        functools.partial(_lru_rows_body, nb=nb, reverse=reverse, combine=combine, n_tiles=n_tiles),
        out_shape=[y_shape, jax.ShapeDtypeStruct((hr * n_groups, e), F32)],
        grid=(e // sw, n_groups, n_tiles),
        in_specs=in_specs,
        out_specs=[y_spec, pl.BlockSpec((hr, sw), lambda c, g, s: (g, c))],
        scratch_shapes=scratch,
        compiler_params=_cparams("parallel", "parallel", "arbitrary"),
        name="lru_core_rows",
    )(*args)


def _rwkv_scan_body(*refs, reverse, zero_init, want_state, finish, n_chunks):
    it = iter(refs)
    (r_ref, k_ref, v_ref, tw_ref, ta_ref, w2_ref, w0_ref, a2_ref, a0_ref, kp_ref, bm_ref,
     tri_ref, cat_ref, lvl_ref) = (next(it) for _ in range(14))
    if finish:
        tao_ref, a2o_ref, a0o_ref, yo_ref, zg_ref, hp_ref = (next(it) for _ in range(6))
    sel_ref = next(it) if want_state else None
    z0_ref = None if zero_init else next(it)
    y_ref = next(it)
    sf_ref = next(it) if want_state else None
    z_s = next(it)
    c = pl.program_id(2)

    @pl.when(c == 0)
    def _():
        z_s[...] = jnp.zeros_like(z_s) if zero_init else z0_ref[...]

    lw_all = _act_logdecay(_dot(tw_ref[...], w2_ref[...]) + w0_ref[...])
    as_all = jax.nn.sigmoid(_dot(ta_ref[...], a2_ref[...]) + a0_ref[...])

    groups = range(z_s.shape[0])
    lanes = [slice(g * GW, (g + 1) * GW) for g in groups]
    bm = bm_ref[...]
    rs = [r_ref[:, l] for l in lanes]
    ks = [k_ref[:, l] for l in lanes]
    vs = [v_ref[:, l] for l in lanes]
    ys, z_new = _rwkv_chunk(
        rs, ks, vs, [lw_all[:, l] for l in lanes], [as_all[:, l] for l in lanes],
        [kp_ref[:, l] for l in lanes], bm, tri_ref[...], cat_ref, lvl_ref,
        [[z_s[g, t] for t in range(z_s.shape[1])] for g in groups], reverse=reverse)
    for g in groups:
        for t in range(z_s.shape[1]):
            z_s[g, t] = z_new[g][t]
    if not finish:
        for g in groups:
            y_ref[:, lanes[g]] = ys[g]
    else:
        bm_b = bm.astype(BF)
        aso_all = jax.nn.sigmoid(_dot(tao_ref[...], a2o_ref[...]) + a0o_ref[...])

        def headsum(x):
            return sum(_dot(p, bm_b) for p in _split3(x)[:2])

        ysum = [ys[g] + yo_ref[:, lanes[g]] for g in groups]
        arg = [rs[g] * ks[g] * hp_ref[1:2, lanes[g]]
               * (2.0 + hp_ref[0:1, lanes[g]] * (as_all[:, lanes[g]] + aso_all[:, lanes[g]] - 2.0))
               for g in groups]
        sums = [headsum(jnp.concatenate([ysum[g], arg[g]], axis=0)) for g in groups]
        yc = [ysum[g] - sums[g][:CH] * (1.0 / HD) for g in groups]
        var = [headsum(yc[g] * yc[g]) * (1.0 / HD) for g in groups]
        for g in groups:
            out = (yc[g] * lax.rsqrt(var[g] + GN_EPS) * hp_ref[2:3, lanes[g]] + hp_ref[3:4, lanes[g]]
                   + sums[g][CH:] * vs[g])
            zg = zg_ref[:, lanes[g]]
            y_ref[:, lanes[g]] = (out * (zg * jax.nn.sigmoid(zg))).astype(y_ref.dtype)
    if want_state:
        @pl.when(c == n_chunks - 1)
        def _():
            sel = sel_ref[...]
            for g in groups:
                for t in range(z_s.shape[1]):
                    sf_ref[g * GW + t * LANES:g * GW + (t + 1) * LANES, :] = sum(
                        _dot_tn(p, sel) for p in _split3(z_new[g][t])[:2])


def _vmap(fn, *lists):
    return [fn(*args) for args in zip(*lists)]


def _rwkv_chunk(r, k, v, lw, asg, kp, bm, tri, cat_ref, lvl_ref, z, *, reverse):
    incl = cat_ref[0]
    strict = cat_ref[1]
    eye_c = incl - strict
    bm_b = bm.astype(BF)

    def headsum(x):
        return sum(_dot(p, bm_b) for p in _split3(x)[:2])

    lane_head = lax.broadcasted_iota(jnp.int32, (1, LANES), 1) // HD
    zero_tile = jnp.zeros((CH, LANES), BF)

    def bdm(x):
        xb = x.astype(BF)
        blocks = []
        for h in range(GH):
            j = h // HEADS_PER_TILE
            tile = xb[:, j * LANES:(j + 1) * LANES]
            keep = jnp.where(lane_head == h % HEADS_PER_TILE, tile, jnp.zeros_like(tile))
            blocks.append(jnp.concatenate([keep if t == j else zero_tile for t in range(GW // LANES)],
                                          axis=1))
        return jnp.concatenate(blocks, axis=0)

    def mm(a, b_bd):
        return _vmap(lambda x, y: _dot(x.astype(BF), y), a, b_bd)

    def stack2(a, b):
        return _vmap(lambda x, y: jnp.concatenate([x, y], axis=0), a, b)

    def top(a):
        return [x[:CH] for x in a]

    def bot(a):
        return [x[CH:] for x in a]

    kk = _vmap(lambda k_, p: k_ * p[0:1], k, kp)
    ss = _vmap(lambda x: headsum(x * x), kk)
    kk = _vmap(lambda x, s: x * lax.rsqrt(jnp.maximum(s, 1e-24)), kk, ss)
    kd = _vmap(lambda k_, a, p: k_ * (1.0 + (a - 1.0) * p[1:2]), k, asg, kp)
    bvec = _vmap(lambda x, a: x * a, kk, asg)

    cum = _vmap(lambda x: sum(_dot(tri, p) for p in _split3(x)[:2]), lw)
    tot = _vmap(lambda x: x[0:1] if reverse else x[CH - 1:CH], cum)
    e_pos = _vmap(jnp.exp, cum)
    e_neg = _vmap(lambda x: jnp.exp(-x), cum)
    e_rem = _vmap(lambda t, en: jnp.exp(t) * en, tot, e_neg)
    a_t = _vmap(lambda x, ep, l: -x * (ep * jnp.exp(-l)), kk, e_pos, lw)
    r_t = _vmap(lambda x, ep: x * ep, r, e_pos)
    ar = [x.astype(BF) for x in stack2(a_t, r_t)]
    bt_bd = _vmap(lambda x, en: bdm(x * en), bvec, e_neg)
    kt_bd = _vmap(lambda x, en: bdm(x * en), kd, e_neg)
    keep2 = jnp.concatenate([strict, incl], axis=0) > 0
    ab = [jnp.where(keep2, m, 0.0) for m in _vmap(_dot_nt, ar, bt_bd)]
    ak = [jnp.where(keep2, m, 0.0).astype(BF) for m in _vmap(_dot_nt, ar, kt_bd)]
    a_ab = top(ab)
    a_rb = [m.astype(BF) for m in bot(ab)]

    p = [n * lvl_ref[0] for n in a_ab]
    x = [eye_c + n for n in p]
    p = [n.astype(BF) for n in p]
    pw = [m.astype(BF) for m in mm(p, _vmap(bdm, p))]
    e = 2
    while e < INV_BASE:
        rhs = _vmap(bdm, pw)
        if 2 * e < INV_BASE:
            both = mm(stack2([m.astype(BF) for m in x], pw), rhs)
            x = _vmap(jnp.add, x, top(both))
            pw = [m.astype(BF) for m in bot(both)]
        else:
            x = _vmap(jnp.add, x, mm(x, rhs))
        e *= 2
    for lvl in range(1, lvl_ref.shape[0]):
        off = [(n * lvl_ref[lvl]).astype(BF) for n in a_ab]
        xb = [m.astype(BF) for m in x]
        x = _vmap(jnp.add, x, mm(xb, _vmap(bdm, mm(off, _vmap(bdm, xb)))))
    xb = [m.astype(BF) for m in x]

    vb = [m.astype(BF) for m in v]
    s_v = mm(ak, _vmap(bdm, vb))
    def z_rhs(zq):
        n_t = len(zq)
        zero = jnp.zeros((LANES, LANES), BF)
        return jnp.concatenate(
            [jnp.concatenate([zq[i].astype(BF) if i == t else zero for t in range(n_t)], axis=1)
             for i in range(n_t)], axis=0)

    s_z = _vmap(lambda a, zq: _dot(a, z_rhs(zq)), ar, z)
    u = mm(xb, _vmap(lambda a, b_: bdm(a + b_), top(s_z), top(s_v)))
    ub = [m.astype(BF) for m in u]
    y =_vmap(lambda a, b_, c_: a + b_ + c_, bot(s_z), bot(s_v), mm(a_rb, _vmap(bdm, ub)))
    row_id = lax.broadcasted_iota(jnp.int32, (LANES, 1), 0)
    lane_id = lax.broadcasted_iota(jnp.int32, (1, LANES), 1)
    same_head = (row_id // HD) == (lane_id // HD)
    diag = row_id == lane_id
    lhs = _vmap(lambda b_, k_, er: jnp.concatenate([b_ * er, k_ * er], axis=0).astype(BF), bvec, kd, e_rem)
    rhs = stack2(ub, vb)
    decay = _vmap(jnp.exp, tot)
    z_new = [[] for _ in z]
    for t in range(GW // LANES):
        cols = slice(t * LANES, (t + 1) * LANES)
        for g in range(len(z)):
            upd = jnp.where(same_head, _dot_tn(lhs[g][:, cols], rhs[g][:, cols]), 0.0)
            g_col = jnp.sum(jnp.where(diag, decay[g][:, cols], 0.0), axis=1, keepdims=True)
            z_new[g].append(z[g][t] * g_col + upd)
    return y, z_new


def _rwkv_scan(r, k, v, tw, ta, lora, jd, kp, consts, d, *, batch, seq_len, z0=None, want_state=False,
               finish=None):
    rows, e = r.shape
    ngs = min(SCAN_GROUPS, e // GW)
    sw = ngs * GW
    ng = e // sw
    nc = seq_len // CH
    reverse = d == 1
    zero_init = z0 is None
    rank = lora["w2"].shape[1]

    def cidx(c):
        return (nc - 1 - c) if reverse else c

    def lora_specs(dd, jdd):
        return [pl.BlockSpec((CH, rank), lambda b, g, c: (b * nc + cidx(c), dd)),
                pl.BlockSpec((None, rank, sw), lambda b, g, c: (jdd, 0, g)),
                pl.BlockSpec((None, 1, sw), lambda b, g, c: (jdd, 0, g))]

    tok = pl.BlockSpec((CH, sw), lambda b, g, c: (b * nc + cidx(c), g))
    w_specs, a_specs = lora_specs(d, jd), lora_specs(d, jd)
    in_specs = [tok, tok, tok, w_specs[0], a_specs[0], w_specs[1], w_specs[2], a_specs[1], a_specs[2],
                pl.BlockSpec((2, sw), lambda b, g, c: (0, g)),
                pl.BlockSpec((GW, GW), lambda b, g, c: (0, 0)),
                pl.BlockSpec((None, CH, CH), lambda b, g, c: (d, 0, 0)),
                pl.BlockSpec((None, 2, CH, GW), lambda b, g, c: (d, 0, 0, 0)),
                pl.BlockSpec(consts["lvl"].shape, lambda b, g, c: (0, 0, 0))]
    args = [r, k, v, tw, ta, lora["w2"], lora["w0"], lora["a2"], lora["a0"], kp, consts["bm"],
            consts["tri"], consts["cat"], consts["lvl"]]
    if finish is not None:
        y_other, zg, hp = finish
        in_specs += lora_specs(1 - d, jd + 1 - 2 * d) + [tok, tok, pl.BlockSpec((4, sw), lambda b, g, c: (0, g))]
        args += [ta, lora["a2"], lora["a0"], y_other, zg, hp]
    if want_state:
        in_specs.append(pl.BlockSpec((LANES, HD), lambda b, g, c: (0, 0)))
        args.append(consts["sel"])
    if not zero_init:
        in_specs.append(pl.BlockSpec((None, ngs, GW // LANES, LANES, LANES), lambda b, g, c: (b, g, 0, 0, 0)))
        args.append(z0)
    out_shape = [jax.ShapeDtypeStruct((rows, e), F32 if finish is None else BF)]
    out_specs = [tok]
    if want_state:
        out_shape.append(jax.ShapeDtypeStruct((batch, e, HD), F32))
        out_specs.append(pl.BlockSpec((None, sw, HD), lambda b, g, c: (b, g, 0)))
    res = pl.pallas_call(
        functools.partial(_rwkv_scan_body, reverse=reverse, zero_init=zero_init,
                          want_state=want_state, finish=finish is not None, n_chunks=nc),
        out_shape=out_shape,
        grid=(batch, ng, nc),
        in_specs=in_specs,
        out_specs=out_specs,
        scratch_shapes=[pltpu.VMEM((ngs, GW // LANES, LANES, LANES), F32)],
        compiler_params=_cparams("parallel", "parallel", "arbitrary"),
        name="rwkv_scan",
    )(*args)
    return res if want_state else (res[0], None)


def _silu(x):
    return x * jax.nn.sigmoid(x)


def _act_tanh(x):
    return jnp.tanh(x)


def _act_sigmoid(x):
    return jax.nn.sigmoid(x)


def _act_logdecay(x):
    return -jnp.exp(-0.5) * jax.nn.sigmoid(x)


def _act_vmix(acc, v_first, mix):
    return acc + (v_first - acc) * mix


def _trunk(x, mods, init_lru, init_rwkv, W, *, batch, seq_len, grid_mode, want_state):
    rows, d = x.shape
    e = W["lru_w_out"].shape[1]
    depth = W["norm_pre"].shape[0]
    tm_row = min(NORM_TILE_ROWS, seq_len)
    tm_mm = min(MM_TILE_ROWS, rows)
    mk = dict(tm=tm_mm, m_tiles=rows // tm_mm)
    cond_rows = rows // mods[0].shape[0]
    fin_lru, fin_rwkv = [], []
    v_first = None

    for i in range(depth):
        mod = mods[i]
        j = i // 2
        if i % 2 == 0:
            h = _prenorm(x, W["norm_pre"], i, mod, tm=tm_row, rows_per_cond=cond_rows)
            xz = _mm(h, W["lru_w_in"], j, None, tn=MM_TN, out_dtype=BF, name="lru_in", **mk)
            xz = xz.reshape(batch, seq_len, 2 * e)
            y0, hf0 = _lru_core_rows(xz, W["lru"], j, 0, init_lru[j][0])
            y, hf1 = _lru_core_rows(xz, W["lru"], j, 1, init_lru[j][1], y_prev=y0)
            m = _mm(y.reshape(rows, e), W["lru_w_out"], j, None, tn=MM_TN, name="lru_out", **mk)
            fin_lru.append((hf0, hf1))
        else:
            h, xr, xw, xk, xv, xa = _shiftmix(
                x, W["norm_pre"], i, mod, W["rwkv_mu"][j], seq_len=seq_len, cond_rows=cond_rows, grid_mode=grid_mode,
                tm=seq_len if not grid_mode else min(SHIFT_TILE_ROWS, seq_len))
            r = _mm(xr, W["rwkv_w_r"], j, None, tn=MM_TN, name="rwkv_r", **mk)
            k = _mm(xk, W["rwkv_w_k"], j, None, tn=MM_TN, name="rwkv_k", **mk)
            z = _mm(h, W["rwkv_w_g"], j, None, tn=MM_TN, name="rwkv_g", **mk)
            if j == 0:
                v = _mm(xv, W["rwkv_w_v"], j, None, tn=MM_TN, name="rwkv_v", **mk)
                v_first = v
            else:
                tv = _mm(xv, W["rwkv_v1"], j - 1, None, tn=MM_TN, out_dtype=BF, name="rwkv_v1", **mk)
                mix = _mm(tv, W["rwkv_v2"], j - 1, W["rwkv_v0"][j - 1], tn=MM_TN, act=_act_sigmoid,
                          name="rwkv_v2", **mk)
                v = _mm(xv, W["rwkv_w_v"], j, None, tn=MM_TN, extras=(v_first, mix), act=_act_vmix,
                        name="rwkv_v", **mk)
            tw = _mm(xw, W["rwkv_w1"], j, None, tn=MM_TN, act=_act_tanh, out_dtype=BF, name="rwkv_w1", **mk)
            ta = _mm(xa, W["rwkv_a1"], j, None, tn=MM_TN, out_dtype=BF, name="rwkv_a1", **mk)
            fins = []
            y_d = None
            for dd in range(2):
                z0 = None if init_rwkv is None else init_rwkv[j][dd]
                finish = None if dd == 0 else (y_d, z, W["rwkv_hp"][j])
                y_d, s_d = _rwkv_scan(r, k, v, tw, ta, W["rwkv_lora"], j * 2 + dd, W["rwkv_kp"][j],
                                      W["consts"], dd, batch=batch, seq_len=seq_len, z0=z0,
                                      want_state=want_state, finish=finish)
                fins.append(s_d)
            fin_rwkv.append(fins)
            m = _mm(y_d, W["rwkv_w_o"], j, None, tn=MM_TN, name="rwkv_o", **mk)
        x = _postnorm(x, m, W["norm_post"], i, mod, tm=tm_row, rows_per_cond=cond_rows)
    return x, fin_lru, fin_rwkv


def _scan_consts():
    idx = jnp.arange(GW)
    bm = ((idx[:, None] // HD) == (idx[None, :] // HD)).astype(F32)
    t_r, t_c = jnp.arange(CH)[:, None], idx[None, :] % CH
    cats, tris = [], []
    ic = jnp.arange(CH)
    for d in range(2):
        ge = (t_r >= t_c) if d == 0 else (t_r <= t_c)
        gt = (t_r > t_c) if d == 0 else (t_r < t_c)
        cats.append(jnp.stack([ge.astype(F32), gt.astype(F32)]))
        tris.append(((ic[:, None] >= ic[None, :]) if d == 0 else (ic[:, None] <= ic[None, :])).astype(BF))
    sel = (idx[:, None] % HD == jnp.arange(HD)[None, :]).astype(BF)
    lvl = [t_r // INV_BASE == t_c // INV_BASE]
    size = INV_BASE
    while size < CH:
        lvl.append((t_r // (2 * size) == t_c // (2 * size)) & (t_r // size != t_c // size))
        size *= 2
    return dict(bm=bm, cat=jnp.stack(cats), tri=jnp.stack(tris), sel=sel,
                lvl=jnp.stack(lvl).astype(F32))


def kernel(x_prompt, x_sample, state_lru, state_rwkv, c, c_ctx, ada_w, ada_b, norm_pre, norm_post, lru_w_in, lru_conv_w, lru_conv_b, lru_gate_w, lru_gate_b, lru_lambda, lru_w_out, rwkv_mu, rwkv_w_r, rwkv_w_k, rwkv_w_v, rwkv_w_g, rwkv_w_o, rwkv_w0, rwkv_w1, rwkv_w2, rwkv_a0, rwkv_a1, rwkv_a2, rwkv_k_k, rwkv_k_a, rwkv_r_k, rwkv_ln_w, rwkv_ln_b, rwkv_v0, rwkv_v1, rwkv_v2):
    bp, tp, d = x_prompt.shape
    bs, ts, _ = x_sample.shape
    depth = ada_w.shape[0]
    n_lru, _, e2 = lru_w_in.shape
    e = e2 // 2
    n_rwkv = rwkv_w_r.shape[0]
    nblk = e // LB
    heads = e // HD
    ng = e // GW

    def merge_lead(w, k):
        return w.reshape((-1,) + w.shape[k:])

    lora_v = rwkv_v1.shape[-1]
    pad_v = (-lora_v) % 128
    W = dict(
        norm_pre=norm_pre.reshape(depth, 1, d), norm_post=norm_post.reshape(depth, 1, d),
        lru_w_in=lru_w_in.astype(BF), lru_w_out=lru_w_out.astype(BF),
        lru=dict(conv_w=lru_conv_w, conv_b=lru_conv_b.reshape(n_lru, 1, e),
                 gate_w=merge_lead(lru_gate_w, 2),
                 gate_b=merge_lead(lru_gate_b, 2),
                 lam=lru_lambda.reshape(n_lru * 2, 1, e)),
        rwkv_mu=rwkv_mu,
        rwkv_w_r=rwkv_w_r.astype(BF), rwkv_w_k=rwkv_w_k.astype(BF), rwkv_w_v=rwkv_w_v.astype(BF),
        rwkv_w_g=rwkv_w_g.astype(BF), rwkv_w_o=rwkv_w_o.astype(BF),
        rwkv_w1=jnp.concatenate([rwkv_w1[:, 0], rwkv_w1[:, 1]], axis=-1).astype(BF),
        rwkv_a1=jnp.concatenate([rwkv_a1[:, 0], rwkv_a1[:, 1]], axis=-1).astype(BF),
        rwkv_lora=dict(w2=merge_lead(rwkv_w2, 2).astype(BF), a2=merge_lead(rwkv_a2, 2).astype(BF),
                       w0=rwkv_w0.reshape(n_rwkv * 2, 1, e), a0=rwkv_a0.reshape(n_rwkv * 2, 1, e)),
        rwkv_v0=rwkv_v0.reshape(-1, 1, e),
        rwkv_v1=jnp.pad(rwkv_v1, ((0, 0), (0, 0), (0, pad_v))).astype(BF),
        rwkv_v2=jnp.pad(rwkv_v2, ((0, 0), (0, pad_v), (0, 0))).astype(BF),
        rwkv_kp=jnp.stack([rwkv_k_k, rwkv_k_a], axis=1),
        rwkv_hp=jnp.stack([rwkv_k_a, rwkv_r_k, rwkv_ln_w, rwkv_ln_b], axis=1),
        consts=_scan_consts(),
    )

    n_cond = 1 + bs
    cond = jnp.zeros((SUBLANES * pl.cdiv(n_cond, SUBLANES), d), F32)
    cond = cond.at[0].set(c_ctx).at[1:n_cond].set(c)
    ada_wb = ada_w.astype(BF)
    mods_p, mods_s = [], []
    for i in range(depth):
        mod = _mm(cond, ada_wb, i, ada_b[i][None], tm=cond.shape[0], tn=256, m_tiles=1,
                  a_pre=_silu, name="adaln")
        mod = mod.reshape(cond.shape[0], 3, d)
        mods_p.append(mod[0:1])
        mods_s.append(mod[1:n_cond])

    zero_lru = [(jnp.zeros((bp, e), F32), jnp.zeros((bp, e), F32))] * n_lru
    y_p, fin_lru, fin_rwkv = _trunk(x_prompt.reshape(bp * tp, d), mods_p, zero_lru, None, W,
                                    batch=bp, seq_len=tp, grid_mode=False, want_state=True)
    new_state_lru = jnp.stack([jnp.stack(f, axis=1) for f in fin_lru], axis=1)
    new_state_rwkv = jnp.stack(
        [jnp.stack([s.reshape(bp, heads, HD, HD) for s in f], axis=1) for f in fin_rwkv], axis=1)

    rep = max(SUBLANES // bs, 1)
    init_lru = [tuple(jnp.tile(state_lru[:, jj, dd], (rep, 1)) for dd in range(2)) for jj in range(n_lru)]
    eye_t = jnp.eye(HEADS_PER_TILE, dtype=F32)
    n_tiles = e // LANES
    st = state_rwkv.reshape(bs, n_rwkv, 2, n_tiles, HEADS_PER_TILE, HD, HD)
    z0 = jnp.einsum("bldthvk,hi->bldthkiv", st, eye_t).reshape(
        bs, n_rwkv, 2, ng, GW // LANES, LANES, LANES)
    init_rwkv = [[z0[:, jj, dd] for dd in range(2)] for jj in range(n_rwkv)]
    y_s, _, _ = _trunk(x_sample.reshape(bs * ts, d), mods_s, init_lru, init_rwkv, W,
                       batch=bs, seq_len=ts, grid_mode=True, want_state=False)
    return (y_p.reshape(bp, tp, d), y_s.reshape(bs, ts, d), new_state_lru, new_state_rwkv)
```

```python
import functools

import jax
import jax.numpy as jnp
from jax import lax
from jax.experimental import pallas as pl
from jax.experimental.pallas import tpu as pltpu

F32 = jnp.float32
BF = jnp.bfloat16

NORM_EPS = 1e-6
GN_EPS = 64e-5
LRU_C = 8.0
GRID_W = 64
CONV_W = 4
CONV_LEFT = 2

HD = 64
GH = 4
GW = GH * HD
CH = 64
INV_BASE = 16
SCAN_GROUPS = 16
LB = 256
SUBLANES = 8
LANES = 128
HEADS_PER_TILE = LANES // HD
VMEM_LIMIT = 56 * 1024 * 1024
MM_VMEM_BUDGET = 44 * 1024 * 1024
MM_TILE_ROWS = 1024
MM_TN = 1024
NORM_TILE_ROWS = 512
SHIFT_TILE_ROWS = 512
LRU_TILE_ROWS = 1024
LRU_STEP_LANES = 1024


def _cparams(*sem):
    return pltpu.CompilerParams(dimension_semantics=sem, vmem_limit_bytes=VMEM_LIMIT)


def _dot(a, b):
    return jnp.dot(a, b, preferred_element_type=F32)


def _dot_nt(a, b):
    return lax.dot_general(a, b, (((1,), (1,)), ((), ())), preferred_element_type=F32)


def _dot_tn(a, b):
    return lax.dot_general(a, b, (((0,), (0,)), ((), ())), preferred_element_type=F32)


def _split3(x):
    h1 = x.astype(BF)
    r1 = x - h1.astype(F32)
    h2 = r1.astype(BF)
    h3 = (r1 - h2.astype(F32)).astype(BF)
    return h1, h2, h3


def _modnorm(x, g, mod):
    y = x * lax.rsqrt(jnp.mean(x * x, axis=-1, keepdims=True) + NORM_EPS) * g
    return y * (1.0 + mod[1:2]) + mod[0:1]


def _mm_body(*refs, act, a_pre, n_extra):
    a_ref, w_ref, b_ref = refs[:3]
    extras = refs[3:3 + n_extra]
    o_ref = refs[3 + n_extra]
    a = a_ref[...]
    if a_pre is not None:
        a = a_pre(a.astype(F32))
    acc = _dot(a.astype(BF), w_ref[...]) + b_ref[...]
    if act is not None:
        acc = act(acc, *[e[...] for e in extras])
    o_ref[...] = acc.astype(o_ref.dtype)


def _mm_tn(k, n, tm, tn_max, n_extra, out_itemsize):
    tn = min(tn_max, n)
    while tn > LANES:
        w_bytes = 2 * k * tn * 2
        io_bytes = 2 * tm * (k * 2 + tn * (out_itemsize + 4 * n_extra))
        if n % tn == 0 and w_bytes + io_bytes <= MM_VMEM_BUDGET:
            break
        tn //= 2
    return tn


def _mm(a, w3, li, bias, *, tm, tn, m_tiles, extras=(), act=None, a_pre=None, out_dtype=F32, name="mm"):
    _, K, N = w3.shape
    tn = _mm_tn(K, N, tm, tn, len(extras), jnp.dtype(out_dtype).itemsize)
    assert N % tn == 0 and K % LANES == 0
    if bias is None:
        bias = jnp.zeros((1, N), F32)
    in_specs = [
        pl.BlockSpec((tm, K), lambda n, m: (m, 0)),
        pl.BlockSpec((None, K, tn), lambda n, m: (li, 0, n)),
        pl.BlockSpec((1, tn), lambda n, m: (0, n)),
    ] + [pl.BlockSpec((tm, tn), lambda n, m: (m, n)) for _ in extras]
    return pl.pallas_call(
        functools.partial(_mm_body, act=act, a_pre=a_pre, n_extra=len(extras)),
        out_shape=jax.ShapeDtypeStruct((m_tiles * tm, N), out_dtype),
        grid=(N // tn, m_tiles),
        in_specs=in_specs,
        out_specs=pl.BlockSpec((tm, tn), lambda n, m: (m, n)),
        compiler_params=_cparams("parallel", "parallel"),
        name=name,
    )(a, w3, bias, *extras)


def _prenorm_body(x_ref, g_ref, mod_ref, o_ref):
    o_ref[...] = _modnorm(x_ref[...], g_ref[...], mod_ref[0]).astype(o_ref.dtype)


def _prenorm(x, g3, layer, mod, *, tm, rows_per_cond):
    rows, d = x.shape
    return pl.pallas_call(
        _prenorm_body,
        out_shape=jax.ShapeDtypeStruct((rows, d), BF),
        grid=(rows // tm,),
        in_specs=[
            pl.BlockSpec((tm, d), lambda i: (i, 0)),
            pl.BlockSpec((None, 1, d), lambda i: (layer, 0, 0)),
            pl.BlockSpec((1, 3, d), lambda i: ((i * tm) // rows_per_cond, 0, 0)),
        ],
        out_specs=pl.BlockSpec((tm, d), lambda i: (i, 0)),
        compiler_params=_cparams("parallel"),
        name="prenorm",
    )(x, g3, mod)


def _postnorm_body(x_ref, m_ref, g_ref, mod_ref, o_ref):
    m = m_ref[...]
    y = m * lax.rsqrt(jnp.mean(m * m, axis=-1, keepdims=True) + NORM_EPS) * g_ref[...]
    o_ref[...] = x_ref[...] + mod_ref[0][2:3] * y


def _postnorm(x, m, g3, layer, mod, *, tm, rows_per_cond):
    rows, d = x.shape
    return pl.pallas_call(
        _postnorm_body,
        out_shape=jax.ShapeDtypeStruct((rows, d), F32),
        grid=(rows // tm,),
        in_specs=[
            pl.BlockSpec((tm, d), lambda i: (i, 0)),
            pl.BlockSpec((tm, d), lambda i: (i, 0)),
            pl.BlockSpec((None, 1, d), lambda i: (layer, 0, 0)),
            pl.BlockSpec((1, 3, d), lambda i: ((i * tm) // rows_per_cond, 0, 0)),
        ],
        out_specs=pl.BlockSpec((tm, d), lambda i: (i, 0)),
        compiler_params=_cparams("parallel"),
        name="postnorm",
    )(x, m, g3, mod)


def _write_mixes(h, shifted, mu_ref, out_refs):
    xx = shifted - h
    out_refs[0][...] = h.astype(BF)
    for m in range(5):
        out_refs[m + 1][...] = (h + xx * mu_ref[m:m + 1]).astype(BF)


def _shiftmix_seq_body(x_ref, g_ref, mod_ref, mu_ref, *out_refs):
    h = _modnorm(x_ref[...], g_ref[...], mod_ref[0])
    t_len, d = h.shape
    half = d // 2
    t = lax.broadcasted_iota(jnp.int32, (t_len, 1), 0)
    prev = jnp.where(t == 0, 0.0, pltpu.roll(h[:, :half], 1, 0))
    nxt = jnp.where(t == t_len - 1, 0.0, pltpu.roll(h[:, half:], t_len - 1, 0))
    _write_mixes(h, jnp.concatenate([prev, nxt], axis=1), mu_ref, out_refs)


def _shiftmix_grid_body(x_ref, up_ref, dn_ref, g_ref, mod_ref, mu_ref, *out_refs, tiles_per_batch):
    ti = pl.program_id(0) % tiles_per_batch
    g = g_ref[...]
    mod = mod_ref[0]
    h = _modnorm(x_ref[...], g, mod)
    tr, d = h.shape
    q = d // 4
    h_up = jnp.where(ti > 0, _modnorm(up_ref[...], g, mod)[:, 2 * q:3 * q], 0.0)
    h_dn = jnp.where(ti < tiles_per_batch - 1, _modnorm(dn_ref[...], g, mod)[:, 3 * q:], 0.0)
    col = lax.broadcasted_iota(jnp.int32, (tr, 1), 0) % GRID_W
    left = jnp.where(col == 0, 0.0, pltpu.roll(h[:, :q], 1, 0))
    right = jnp.where(col == GRID_W - 1, 0.0, pltpu.roll(h[:, q:2 * q], tr - 1, 0))
    up = jnp.concatenate([h_up, h[:tr - GRID_W, 2 * q:3 * q]], axis=0)
    down = jnp.concatenate([h[GRID_W:, 3 * q:], h_dn], axis=0)
    _write_mixes(h, jnp.concatenate([left, right, up, down], axis=1), mu_ref, out_refs)


def _shiftmix(x, g3, layer, mod, mu, *, seq_len, cond_rows, grid_mode, tm):
    rows, d = x.shape
    outs = [jax.ShapeDtypeStruct((rows, d), BF)] * 6
    common = [
        pl.BlockSpec((None, 1, d), lambda i: (layer, 0, 0)),
        pl.BlockSpec((1, 3, d), lambda i: ((i * tm) // cond_rows, 0, 0)),
        pl.BlockSpec((5, d), lambda i: (0, 0)),
    ]
    out_specs = [pl.BlockSpec((tm, d), lambda i: (i, 0))] * 6
    if not grid_mode:
        assert tm == seq_len
        return pl.pallas_call(
            _shiftmix_seq_body, out_shape=outs, grid=(rows // tm,),
            in_specs=[pl.BlockSpec((tm, d), lambda i: (i, 0))] + common,
            out_specs=out_specs, compiler_params=_cparams("parallel"), name="shiftmix_seq",
        )(x, g3, mod, mu)
    per = tm // GRID_W
    n_halo = rows // GRID_W
    return pl.pallas_call(
        functools.partial(_shiftmix_grid_body, tiles_per_batch=seq_len // tm),
        out_shape=outs, grid=(rows // tm,),
        in_specs=[
            pl.BlockSpec((tm, d), lambda i: (i, 0)),
            pl.BlockSpec((GRID_W, d), lambda i: (jnp.maximum(i * per - 1, 0), 0)),
            pl.BlockSpec((GRID_W, d), lambda i: (jnp.minimum((i + 1) * per, n_halo - 1), 0)),
        ] + common,
        out_specs=out_specs, compiler_params=_cparams("parallel"), name="shiftmix_grid",
    )(x, x, x, g3, mod, mu)


def _lru_gates_scan(xc, gw_ref, gb_ref, lam_ref, a_s, u_s, h, *, rb, reverse):
    tr = xc.shape[0]
    for q in range(xc.shape[1] // LB):
        ln = slice(q * LB, (q + 1) * LB)
        xq = xc[:, ln]
        xb = xq.astype(BF)

        def gate(n):
            return 0.5 * jnp.tanh(0.5 * (_dot(xb, gw_ref[n, q].astype(BF)) + gb_ref[n:n + 1, ln])) + 0.5

        r = gate(0)
        i = gate(1)
        log_a = r * (-LRU_C * jax.nn.softplus(-lam_ref[:, ln]))
        a_s[:, ln] = jnp.exp(log_a)
        th = jnp.tanh(log_a)
        u_s[:, ln] = jnp.sqrt(-2.0 * th / (1.0 - th)) * (i * xq)

    if rb % SUBLANES == 0:
        n_steps = tr // rb

        def body(s, h):
            row = pl.multiple_of(((n_steps - 1 - s) if reverse else s) * rb, rb)
            h = a_s[pl.ds(row, rb), :] * h + u_s[pl.ds(row, rb), :]
            u_s[pl.ds(row, rb), :] = h
            return h

        h = lax.fori_loop(0, n_steps, body, h, unroll=4)
    else:
        assert 2 * rb == SUBLANES
        n_steps = tr // SUBLANES
        first = lax.broadcasted_iota(jnp.int32, (SUBLANES, 1), 0) < rb

        def body(s, h):
            row = pl.multiple_of(((n_steps - 1 - s) if reverse else s) * SUBLANES, SUBLANES)
            a = a_s[pl.ds(row, SUBLANES), :]
            u = u_s[pl.ds(row, SUBLANES), :]
            h1 = a * h + u
            h2 = a * pltpu.roll(h1, rb, 0) + u
            u_s[pl.ds(row, SUBLANES), :] = jnp.where(first != reverse, h1, h2)
            return pltpu.roll(h2, rb, 0)

        h = lax.fori_loop(0, n_steps, body, h, unroll=4)
    return h


def _lru_rows_body(*refs, nb, reverse, combine, n_tiles):
    if combine:
        (x_ref, xp_ref, xn_ref, cw_ref, cb_ref, gw_ref, gb_ref, lam_ref, h0_ref, yp_ref, z_ref,
         y_ref, hf_ref, xs, a_s, u_s, hc, zs) = refs
    else:
        (x_ref, xp_ref, xn_ref, cw_ref, cb_ref, gw_ref, gb_ref, lam_ref, h0_ref,
         y_ref, hf_ref, xs, a_s, u_s, hc) = refs
    tt = x_ref.shape[1]
    ht = xp_ref.shape[1]
    tr, hb = tt * nb, ht * nb
    tiles = [slice(t * LANES, (t + 1) * LANES) for t in range(x_ref.shape[2] // LANES)]
    step = pl.program_id(2)
    t_idx = (n_tiles - 1 - step) if reverse else step

    @pl.when(step == 0)
    def _():
        hc[...] = h0_ref[...]

    for b in range(nb):
        for t, ln in enumerate(tiles):
            xs[t, pl.ds(hb + b, tt, stride=nb), :] = x_ref[b, :, ln].astype(F32)
            xs[t, pl.ds(b, ht, stride=nb), :] = jnp.where(t_idx > 0, xp_ref[b, :, ln].astype(F32), 0.0)
            xs[t, pl.ds(hb + tr + b, ht, stride=nb), :] = jnp.where(
                t_idx < n_tiles - 1, xn_ref[b, :, ln].astype(F32), 0.0)
    parts = []
    for t, ln in enumerate(tiles):
        acc = cb_ref[:, ln]
        for tap in range(CONV_W):
            off = hb + (tap - CONV_LEFT) * nb
            acc = acc + xs[t, off:off + tr, :] * cw_ref[tap:tap + 1, ln]
        parts.append(acc)
    xc = jnp.concatenate(parts, axis=1)
    h = _lru_gates_scan(xc, gw_ref, gb_ref, lam_ref, a_s, u_s, hc[...], rb=nb, reverse=reverse)
    hc[...] = h
    hf_ref[...] = h
    if not combine:
        y_ref[...] = u_s[...]
        return
    for t, ln in enumerate(tiles):
        zs[t] = yp_ref[:, ln] + u_s[:, ln]
    for b in range(nb):
        for t, ln in enumerate(tiles):
            z = z_ref[b, :, ln].astype(F32)
            y_b = zs[t, pl.ds(b, tt, stride=nb), :]
            y_ref[b, :, ln] = (y_b * (z * jax.nn.sigmoid(z))).astype(y_ref.dtype)


def _lru_core_rows(xz3, lru, j, d, h0, *, y_prev=None):
    batch, seq_len, e2 = xz3.shape
    e = e2 // 2
    nb = min(batch, SUBLANES)
    n_groups = batch // nb
    tt = min(LRU_TILE_ROWS // nb, seq_len)
    n_tiles = seq_len // tt
    ht = 2 * SUBLANES
    per = tt // ht
    n_halo = seq_len // ht
    tr, hb = tt * nb, ht * nb
    hr = h0.shape[0] // n_groups
    reverse = d == 1
    combine = y_prev is not None

    def tidx(s):
        return (n_tiles - 1 - s) if reverse else s

    sw = min(LRU_STEP_LANES, e)
    nsub = sw // LB
    in_specs = [
        pl.BlockSpec((nb, tt, sw), lambda c, g, s: (g, tidx(s), c)),
        pl.BlockSpec((nb, ht, sw), lambda c, g, s: (g, jnp.maximum(tidx(s) * per - 1, 0), c)),
        pl.BlockSpec((nb, ht, sw), lambda c, g, s: (g, jnp.minimum((tidx(s) + 1) * per, n_halo - 1), c)),
        pl.BlockSpec((None, CONV_W, sw), lambda c, g, s: (j, 0, c)),
        pl.BlockSpec((None, 1, sw), lambda c, g, s: (j, 0, c)),
        pl.BlockSpec((None, 2, nsub, LB, LB), lambda c, g, s: (j * 2 + d, 0, c, 0, 0)),
        pl.BlockSpec((None, 2, sw), lambda c, g, s: (j * 2 + d, 0, c)),
        pl.BlockSpec((None, 1, sw), lambda c, g, s: (j * 2 + d, 0, c)),
        pl.BlockSpec((hr, sw), lambda c, g, s: (g, c)),
    ]
    args = [xz3, xz3, xz3, lru["conv_w"], lru["conv_b"], lru["gate_w"], lru["gate_b"], lru["lam"], h0]
    lt = sw // LANES
    scratch = [pltpu.VMEM((lt, tr + 2 * hb, LANES), F32), pltpu.VMEM((tr, sw), F32),
               pltpu.VMEM((tr, sw), F32), pltpu.VMEM((hr, sw), F32)]
    if combine:
        in_specs += [
            pl.BlockSpec((None, tr, sw), lambda c, g, s: (g, tidx(s), c)),
            pl.BlockSpec((nb, tt, sw), lambda c, g, s: (g, tidx(s), e // sw + c)),
        ]
        args += [y_prev, xz3]
        scratch.append(pltpu.VMEM((lt, tr, LANES), F32))
        y_shape = jax.ShapeDtypeStruct((batch, seq_len, e), BF)
        y_spec = pl.BlockSpec((nb, tt, sw), lambda c, g, s: (g, tidx(s), c))
    else:
        y_shape = jax.ShapeDtypeStruct((n_groups, seq_len * nb, e), F32)
        y_spec = pl.BlockSpec((None, tr, sw), lambda c, g, s: (g, tidx(s), c))
    return pl.pallas_call(
        functools.partial(_lru_rows_body, nb=nb, reverse=reverse, combine=combine, n_tiles=n_tiles),
        out_shape=[y_shape, jax.ShapeDtypeStruct((hr * n_groups, e), F32)],
        grid=(e // sw, n_groups, n_tiles),
        in_specs=in_specs,
        out_specs=[y_spec, pl.BlockSpec((hr, sw), lambda c, g, s: (g, c))],
        scratch_shapes=scratch,
        compiler_params=_cparams("parallel", "parallel", "arbitrary"),
        name="lru_core_rows",
    )(*args)


def _rwkv_scan_body(*refs, reverse, zero_init, want_state, finish, n_chunks):
    it = iter(refs)
    (r_ref, k_ref, v_ref, tw_ref, ta_ref, w2_ref, w0_ref, a2_ref, a0_ref, kp_ref, bm_ref,
     tri_ref, cat_ref, lvl_ref) = (next(it) for _ in range(14))
    if finish:
        tao_ref, a2o_ref, a0o_ref, yo_ref, zg_ref, hp_ref = (next(it) for _ in range(6))
    sel_ref = next(it) if want_state else None
    z0_ref = None if zero_init else next(it)
    y_ref = next(it)
    sf_ref = next(it) if want_state else None
    z_s = next(it)
    c = pl.program_id(2)

    @pl.when(c == 0)
    def _():
        z_s[...] = jnp.zeros_like(z_s) if zero_init else z0_ref[...]

    lw_all = _act_logdecay(_dot(tw_ref[...], w2_ref[...]) + w0_ref[...])
    as_all = jax.nn.sigmoid(_dot(ta_ref[...], a2_ref[...]) + a0_ref[...])

    groups = range(z_s.shape[0])
    lanes = [slice(g * GW, (g + 1) * GW) for g in groups]
    bm = bm_ref[...]
    rs = [r_ref[:, l] for l in lanes]
    ks = [k_ref[:, l] for l in lanes]
    vs = [v_ref[:, l] for l in lanes]
    ys, z_new = _rwkv_chunk(
        rs, ks, vs, [lw_all[:, l] for l in lanes], [as_all[:, l] for l in lanes],
        [kp_ref[:, l] for l in lanes], bm, tri_ref[...], cat_ref, lvl_ref,
        [[z_s[g, t] for t in range(z_s.shape[1])] for g in groups], reverse=reverse)
    for g in groups:
        for t in range(z_s.shape[1]):
            z_s[g, t] = z_new[g][t]
    if not finish:
        for g in groups:
            y_ref[:, lanes[g]] = ys[g]
    else:
        bm_b = bm.astype(BF)
        aso_all = jax.nn.sigmoid(_dot(tao_ref[...], a2o_ref[...]) + a0o_ref[...])

        def headsum(x):
            return sum(_dot(p, bm_b) for p in _split3(x)[:2])

        ysum = [ys[g] + yo_ref[:, lanes[g]] for g in groups]
        arg = [rs[g] * ks[g] * hp_ref[1:2, lanes[g]]
               * (2.0 + hp_ref[0:1, lanes[g]] * (as_all[:, lanes[g]] + aso_all[:, lanes[g]] - 2.0))
               for g in groups]
        sums = [headsum(jnp.concatenate([ysum[g], arg[g]], axis=0)) for g in groups]
        yc = [ysum[g] - sums[g][:CH] * (1.0 / HD) for g in groups]
        var = [headsum(yc[g] * yc[g]) * (1.0 / HD) for g in groups]
        for g in groups:
            out = (yc[g] * lax.rsqrt(var[g] + GN_EPS) * hp_ref[2:3, lanes[g]] + hp_ref[3:4, lanes[g]]
                   + sums[g][CH:] * vs[g])
            zg = zg_ref[:, lanes[g]]
            y_ref[:, lanes[g]] = (out * (zg * jax.nn.sigmoid(zg))).astype(y_ref.dtype)
    if want_state:
        @pl.when(c == n_chunks - 1)
        def _():
            sel = sel_ref[...]
            for g in groups:
                for t in range(z_s.shape[1]):
                    sf_ref[g * GW + t * LANES:g * GW + (t + 1) * LANES, :] = sum(
                        _dot_tn(p, sel) for p in _split3(z_new[g][t])[:2])


def _vmap(fn, *lists):
    return [fn(*args) for args in zip(*lists)]


def _rwkv_chunk(r, k, v, lw, asg, kp, bm, tri, cat_ref, lvl_ref, z, *, reverse):
    incl = cat_ref[0]
    strict = cat_ref[1]
    eye_c = incl - strict
    bm_b = bm.astype(BF)

    def headsum(x):
        return sum(_dot(p, bm_b) for p in _split3(x)[:2])

    lane_head = lax.broadcasted_iota(jnp.int32, (1, LANES), 1) // HD
    zero_tile = jnp.zeros((CH, LANES), BF)

    def bdm(x):
        xb = x.astype(BF)
        blocks = []
        for h in range(GH):
            j = h // HEADS_PER_TILE
            tile = xb[:, j * LANES:(j + 1) * LANES]
            keep = jnp.where(lane_head == h % HEADS_PER_TILE, tile, jnp.zeros_like(tile))
            blocks.append(jnp.concatenate([keep if t == j else zero_tile for t in range(GW // LANES)],
                                          axis=1))
        return jnp.concatenate(blocks, axis=0)

    def mm(a, b_bd):
        return _vmap(lambda x, y: _dot(x.astype(BF), y), a, b_bd)

    def stack2(a, b):
        return _vmap(lambda x, y: jnp.concatenate([x, y], axis=0), a, b)

    def top(a):
        return [x[:CH] for x in a]

    def bot(a):
        return [x[CH:] for x in a]

    kk = _vmap(lambda k_, p: k_ * p[0:1], k, kp)
    ss = _vmap(lambda x: headsum(x * x), kk)
    kk = _vmap(lambda x, s: x * lax.rsqrt(jnp.maximum(s, 1e-24)), kk, ss)
    kd = _vmap(lambda k_, a, p: k_ * (1.0 + (a - 1.0) * p[1:2]), k, asg, kp)
    bvec = _vmap(lambda x, a: x * a, kk, asg)

    cum = _vmap(lambda x: sum(_dot(tri, p) for p in _split3(x)[:2]), lw)
    tot = _vmap(lambda x: x[0:1] if reverse else x[CH - 1:CH], cum)
    e_pos = _vmap(jnp.exp, cum)
    e_neg = _vmap(lambda x: jnp.exp(-x), cum)
    e_rem = _vmap(lambda t, en: jnp.exp(t) * en, tot, e_neg)
    a_t = _vmap(lambda x, ep, l: -x * (ep * jnp.exp(-l)), kk, e_pos, lw)
    r_t = _vmap(lambda x, ep: x * ep, r, e_pos)
    ar = [x.astype(BF) for x in stack2(a_t, r_t)]
    bt_bd = _vmap(lambda x, en: bdm(x * en), bvec, e_neg)
    kt_bd = _vmap(lambda x, en: bdm(x * en), kd, e_neg)
    keep2 = jnp.concatenate([strict, incl], axis=0) > 0
    ab = [jnp.where(keep2, m, 0.0) for m in _vmap(_dot_nt, ar, bt_bd)]
    ak = [jnp.where(keep2, m, 0.0).astype(BF) for m in _vmap(_dot_nt, ar, kt_bd)]
    a_ab = top(ab)
    a_rb = [m.astype(BF) for m in bot(ab)]

    p = [n * lvl_ref[0] for n in a_ab]
    x = [eye_c + n for n in p]
    p = [n.astype(BF) for n in p]
    pw = [m.astype(BF) for m in mm(p, _vmap(bdm, p))]
    e = 2
    while e < INV_BASE:
        rhs = _vmap(bdm, pw)
        if 2 * e < INV_BASE:
            both = mm(stack2([m.astype(BF) for m in x], pw), rhs)
            x = _vmap(jnp.add, x, top(both))
            pw = [m.astype(BF) for m in bot(both)]
        else:
            x = _vmap(jnp.add, x, mm(x, rhs))
        e *= 2
    for lvl in range(1, lvl_ref.shape[0]):
        off = [(n * lvl_ref[lvl]).astype(BF) for n in a_ab]
        xb = [m.astype(BF) for m in x]
        x = _vmap(jnp.add, x, mm(xb, _vmap(bdm, mm(off, _vmap(bdm, xb)))))
    xb = [m.astype(BF) for m in x]

    vb = [m.astype(BF) for m in v]
    s_v = mm(ak, _vmap(bdm, vb))
    def z_rhs(zq):
        n_t = len(zq)
        zero = jnp.zeros((LANES, LANES), BF)
        return jnp.concatenate(
            [jnp.concatenate([zq[i].astype(BF) if i == t else zero for t in range(n_t)], axis=1)
             for i in range(n_t)], axis=0)

    s_z = _vmap(lambda a, zq: _dot(a, z_rhs(zq)), ar, z)
    u = mm(xb, _vmap(lambda a, b_: bdm(a + b_), top(s_z), top(s_v)))
    ub = [m.astype(BF) for m in u]
    y =_vmap(lambda a, b_, c_: a + b_ + c_, bot(s_z), bot(s_v), mm(a_rb, _vmap(bdm, ub)))
    row_id = lax.broadcasted_iota(jnp.int32, (LANES, 1), 0)
    lane_id = lax.broadcasted_iota(jnp.int32, (1, LANES), 1)
    same_head = (row_id // HD) == (lane_id // HD)
    diag = row_id == lane_id
    lhs = _vmap(lambda b_, k_, er: jnp.concatenate([b_ * er, k_ * er], axis=0).astype(BF), bvec, kd, e_rem)
    rhs = stack2(ub, vb)
    decay = _vmap(jnp.exp, tot)
    z_new = [[] for _ in z]
    for t in range(GW // LANES):
        cols = slice(t * LANES, (t + 1) * LANES)
        for g in range(len(z)):
            upd = jnp.where(same_head, _dot_tn(lhs[g][:, cols], rhs[g][:, cols]), 0.0)
            g_col = jnp.sum(jnp.where(diag, decay[g][:, cols], 0.0), axis=1, keepdims=True)
            z_new[g].append(z[g][t] * g_col + upd)
    return y, z_new


def _rwkv_scan(r, k, v, tw, ta, lora, jd, kp, consts, d, *, batch, seq_len, z0=None, want_state=False,
               finish=None):
    rows, e = r.shape
    ngs = min(SCAN_GROUPS, e // GW)
    sw = ngs * GW
    ng = e // sw
    nc = seq_len // CH
    reverse = d == 1
    zero_init = z0 is None
    rank = lora["w2"].shape[1]

    def cidx(c):
        return (nc - 1 - c) if reverse else c

    def lora_specs(dd, jdd):
        return [pl.BlockSpec((CH, rank), lambda b, g, c: (b * nc + cidx(c), dd)),
                pl.BlockSpec((None, rank, sw), lambda b, g, c: (jdd, 0, g)),
                pl.BlockSpec((None, 1, sw), lambda b, g, c: (jdd, 0, g))]

    tok = pl.BlockSpec((CH, sw), lambda b, g, c: (b * nc + cidx(c), g))
    w_specs, a_specs = lora_specs(d, jd), lora_specs(d, jd)
    in_specs = [tok, tok, tok, w_specs[0], a_specs[0], w_specs[1], w_specs[2], a_specs[1], a_specs[2],
                pl.BlockSpec((2, sw), lambda b, g, c: (0, g)),
                pl.BlockSpec((GW, GW), lambda b, g, c: (0, 0)),
                pl.BlockSpec((None, CH, CH), lambda b, g, c: (d, 0, 0)),
                pl.BlockSpec((None, 2, CH, GW), lambda b, g, c: (d, 0, 0, 0)),
                pl.BlockSpec(consts["lvl"].shape, lambda b, g, c: (0, 0, 0))]
    args = [r, k, v, tw, ta, lora["w2"], lora["w0"], lora["a2"], lora["a0"], kp, consts["bm"],
            consts["tri"], consts["cat"], consts["lvl"]]
    if finish is not None:
        y_other, zg, hp = finish
        in_specs += lora_specs(1 - d, jd + 1 - 2 * d) + [tok, tok, pl.BlockSpec((4, sw), lambda b, g, c: (0, g))]
        args += [ta, lora["a2"], lora["a0"], y_other, zg, hp]
    if want_state:
        in_specs.append(pl.BlockSpec((LANES, HD), lambda b, g, c: (0, 0)))
        args.append(consts["sel"])
    if not zero_init:
        in_specs.append(pl.BlockSpec((None, ngs, GW // LANES, LANES, LANES), lambda b, g, c: (b, g, 0, 0, 0)))
        args.append(z0)
    out_shape = [jax.ShapeDtypeStruct((rows, e), F32 if finish is None else BF)]
    out_specs = [tok]
    if want_state:
        out_shape.append(jax.ShapeDtypeStruct((batch, e, HD), F32))
        out_specs.append(pl.BlockSpec((None, sw, HD), lambda b, g, c: (b, g, 0)))
    res = pl.pallas_call(
        functools.partial(_rwkv_scan_body, reverse=reverse, zero_init=zero_init,
                          want_state=want_state, finish=finish is not None, n_chunks=nc),
        out_shape=out_shape,
        grid=(batch, ng, nc),
        in_specs=in_specs,
        out_specs=out_specs,
        scratch_shapes=[pltpu.VMEM((ngs, GW // LANES, LANES, LANES), F32)],
        compiler_params=_cparams("parallel", "parallel", "arbitrary"),
        name="rwkv_scan",
    )(*args)
    return res if want_state else (res[0], None)


def _silu(x):
    return x * jax.nn.sigmoid(x)


def _act_tanh(x):
    return jnp.tanh(x)


def _act_sigmoid(x):
    return jax.nn.sigmoid(x)


def _act_logdecay(x):
    return -jnp.exp(-0.5) * jax.nn.sigmoid(x)


def _act_vmix(acc, v_first, mix):
    return acc + (v_first - acc) * mix


def _trunk(x, mods, init_lru, init_rwkv, W, *, batch, seq_len, grid_mode, want_state):
    rows, d = x.shape
    e = W["lru_w_out"].shape[1]
    depth = W["norm_pre"].shape[0]
    cond_rows = rows // mods[0].shape[0]
    tm_row = min(NORM_TILE_ROWS, cond_rows)
    tm_mm = min(MM_TILE_ROWS, rows)
    mk = dict(tm=tm_mm, m_tiles=rows // tm_mm)
    fin_lru, fin_rwkv = [], []
    v_first = None

    for i in range(depth):
        mod = mods[i]
        j = i // 2
        if i % 2 == 0:
            h = _prenorm(x, W["norm_pre"], i, mod, tm=tm_row, rows_per_cond=cond_rows)
            xz = _mm(h, W["lru_w_in"], j, None, tn=MM_TN, out_dtype=BF, name="lru_in", **mk)
            xz = xz.reshape(batch, seq_len, 2 * e)
            y0, hf0 = _lru_core_rows(xz, W["lru"], j, 0, init_lru[j][0])
            y, hf1 = _lru_core_rows(xz, W["lru"], j, 1, init_lru[j][1], y_prev=y0)
            m = _mm(y.reshape(rows, e), W["lru_w_out"], j, None, tn=MM_TN, name="lru_out", **mk)
            fin_lru.append((hf0, hf1))
        else:
            h, xr, xw, xk, xv, xa = _shiftmix(
                x, W["norm_pre"], i, mod, W["rwkv_mu"][j], seq_len=seq_len, cond_rows=cond_rows, grid_mode=grid_mode,
                tm=seq_len if not grid_mode else min(SHIFT_TILE_ROWS, seq_len))
            r = _mm(xr, W["rwkv_w_r"], j, None, tn=MM_TN, name="rwkv_r", **mk)
            k = _mm(xk, W["rwkv_w_k"], j, None, tn=MM_TN, name="rwkv_k", **mk)
            z = _mm(h, W["rwkv_w_g"], j, None, tn=MM_TN, name="rwkv_g", **mk)
            if j == 0:
                v = _mm(xv, W["rwkv_w_v"], j, None, tn=MM_TN, name="rwkv_v", **mk)
                v_first = v
            else:
                tv = _mm(xv, W["rwkv_v1"], j - 1, None, tn=MM_TN, out_dtype=BF, name="rwkv_v1", **mk)
                mix = _mm(tv, W["rwkv_v2"], j - 1, W["rwkv_v0"][j - 1], tn=MM_TN, act=_act_sigmoid,
                          name="rwkv_v2", **mk)
                v = _mm(xv, W["rwkv_w_v"], j, None, tn=MM_TN, extras=(v_first, mix), act=_act_vmix,
                        name="rwkv_v", **mk)
            tw = _mm(xw, W["rwkv_w1"], j, None, tn=MM_TN, act=_act_tanh, out_dtype=BF, name="rwkv_w1", **mk)
            ta = _mm(xa, W["rwkv_a1"], j, None, tn=MM_TN, out_dtype=BF, name="rwkv_a1", **mk)
            fins = []
            y_d = None
            for dd in range(2):
                z0 = None if init_rwkv is None else init_rwkv[j][dd]
                finish = None if dd == 0 else (y_d, z, W["rwkv_hp"][j])
                y_d, s_d = _rwkv_scan(r, k, v, tw, ta, W["rwkv_lora"], j * 2 + dd, W["rwkv_kp"][j],
                                      W["consts"], dd, batch=batch, seq_len=seq_len, z0=z0,
                                      want_state=want_state, finish=finish)
                fins.append(s_d)
            fin_rwkv.append(fins)
            m = _mm(y_d, W["rwkv_w_o"], j, None, tn=MM_TN, name="rwkv_o", **mk)
        x = _postnorm(x, m, W["norm_post"], i, mod, tm=tm_row, rows_per_cond=cond_rows)
    return x, fin_lru, fin_rwkv


def _scan_consts():
    idx = jnp.arange(GW)
    bm = ((idx[:, None] // HD) == (idx[None, :] // HD)).astype(F32)
    t_r, t_c = jnp.arange(CH)[:, None], idx[None, :] % CH
    cats, tris = [], []
    ic = jnp.arange(CH)
    for d in range(2):
        ge = (t_r >= t_c) if d == 0 else (t_r <= t_c)
        gt = (t_r > t_c) if d == 0 else (t_r < t_c)
        cats.append(jnp.stack([ge.astype(F32), gt.astype(F32)]))
        tris.append(((ic[:, None] >= ic[None, :]) if d == 0 else (ic[:, None] <= ic[None, :])).astype(BF))
    sel = (idx[:, None] % HD == jnp.arange(HD)[None, :]).astype(BF)
    lvl = [t_r // INV_BASE == t_c // INV_BASE]
    size = INV_BASE
    while size < CH:
        lvl.append((t_r // (2 * size) == t_c // (2 * size)) & (t_r // size != t_c // size))
        size *= 2
    return dict(bm=bm, cat=jnp.stack(cats), tri=jnp.stack(tris), sel=sel,
                lvl=jnp.stack(lvl).astype(F32))


def kernel(x_prompt, x_sample, state_lru, state_rwkv, c, c_ctx, ada_w, ada_b, norm_pre, norm_post, lru_w_in, lru_conv_w, lru_conv_b, lru_gate_w, lru_gate_b, lru_lambda, lru_w_out, rwkv_mu, rwkv_w_r, rwkv_w_k, rwkv_w_v, rwkv_w_g, rwkv_w_o, rwkv_w0, rwkv_w1, rwkv_w2, rwkv_a0, rwkv_a1, rwkv_a2, rwkv_k_k, rwkv_k_a, rwkv_r_k, rwkv_ln_w, rwkv_ln_b, rwkv_v0, rwkv_v1, rwkv_v2):
    bp, tp, d = x_prompt.shape
    bs, ts, _ = x_sample.shape
    depth = ada_w.shape[0]
    n_lru, _, e2 = lru_w_in.shape
    e = e2 // 2
    n_rwkv = rwkv_w_r.shape[0]
    heads = e // HD
    ng = e // GW

    def merge_lead(w, k):
        return w.reshape((-1,) + w.shape[k:])

    lora_v = rwkv_v1.shape[-1]
    pad_v = (-lora_v) % 128
    W = dict(
        norm_pre=norm_pre.reshape(depth, 1, d), norm_post=norm_post.reshape(depth, 1, d),
        lru_w_in=lru_w_in.astype(BF), lru_w_out=lru_w_out.astype(BF),
        lru=dict(conv_w=lru_conv_w, conv_b=lru_conv_b.reshape(n_lru, 1, e),
                 gate_w=merge_lead(lru_gate_w, 2),
                 gate_b=merge_lead(lru_gate_b, 2),
                 lam=lru_lambda.reshape(n_lru * 2, 1, e)),
        rwkv_mu=rwkv_mu,
        rwkv_w_r=rwkv_w_r.astype(BF), rwkv_w_k=rwkv_w_k.astype(BF), rwkv_w_v=rwkv_w_v.astype(BF),
        rwkv_w_g=rwkv_w_g.astype(BF), rwkv_w_o=rwkv_w_o.astype(BF),
        rwkv_w1=jnp.concatenate([rwkv_w1[:, 0], rwkv_w1[:, 1]], axis=-1).astype(BF),
        rwkv_a1=jnp.concatenate([rwkv_a1[:, 0], rwkv_a1[:, 1]], axis=-1).astype(BF),
        rwkv_lora=dict(w2=merge_lead(rwkv_w2, 2).astype(BF), a2=merge_lead(rwkv_a2, 2).astype(BF),
                       w0=rwkv_w0.reshape(n_rwkv * 2, 1, e), a0=rwkv_a0.reshape(n_rwkv * 2, 1, e)),
        rwkv_v0=rwkv_v0.reshape(-1, 1, e),
        rwkv_v1=jnp.pad(rwkv_v1, ((0, 0), (0, 0), (0, pad_v))).astype(BF),
        rwkv_v2=jnp.pad(rwkv_v2, ((0, 0), (0, pad_v), (0, 0))).astype(BF),
        rwkv_kp=jnp.stack([rwkv_k_k, rwkv_k_a], axis=1),
        rwkv_hp=jnp.stack([rwkv_k_a, rwkv_r_k, rwkv_ln_w, rwkv_ln_b], axis=1),
        consts=_scan_consts(),
    )

    n_cond = 1 + bs
    cond = jnp.zeros((SUBLANES * pl.cdiv(n_cond, SUBLANES), d), F32)
    cond = cond.at[0].set(c_ctx).at[1:n_cond].set(c)
    ada_wb = ada_w.astype(BF)
    mods_p, mods_s = [], []
    for i in range(depth):
        mod = _mm(cond, ada_wb, i, ada_b[i][None], tm=cond.shape[0], tn=256, m_tiles=1,
                  a_pre=_silu, name="adaln")
        mod = mod.reshape(cond.shape[0], 3, d)
        mods_p.append(mod[0:1])
        mods_s.append(mod[1:n_cond])

    zero_lru = [(jnp.zeros((bp, e), F32), jnp.zeros((bp, e), F32))] * n_lru
    y_p, fin_lru, fin_rwkv = _trunk(x_prompt.reshape(bp * tp, d), mods_p, zero_lru, None, W,
                                    batch=bp, seq_len=tp, grid_mode=False, want_state=True)
    new_state_lru = jnp.stack([jnp.stack(f, axis=1) for f in fin_lru], axis=1)
    new_state_rwkv = jnp.stack(
        [jnp.stack([s.reshape(bp, heads, HD, HD) for s in f], axis=1) for f in fin_rwkv], axis=1)

    rep = max(SUBLANES // bs, 1)
    init_lru = [tuple(jnp.tile(state_lru[:, jj, dd], (rep, 1)) for dd in range(2)) for jj in range(n_lru)]
    eye_t = jnp.eye(HEADS_PER_TILE, dtype=F32)
    n_tiles = e // LANES
    st = state_rwkv.reshape(bs, n_rwkv, 2, n_tiles, HEADS_PER_TILE, HD, HD)
    z0 = jnp.einsum("bldthvk,hi->bldthkiv", st, eye_t).reshape(
        bs, n_rwkv, 2, ng, GW // LANES, LANES, LANES)
    init_rwkv = [[z0[:, jj, dd] for dd in range(2)] for jj in range(n_rwkv)]
    y_s, _, _ = _trunk(x_sample.reshape(bs * ts, d), mods_s, init_lru, init_rwkv, W,
                       batch=bs, seq_len=ts, grid_mode=True, want_state=False)
    return (y_p.reshape(bp, tp, d), y_s.reshape(bs, ts, d), new_state_lru, new_state_rwkv)
```

```python
import functools

import jax
import jax.numpy as jnp
from jax import lax
from jax.experimental import pallas as pl
from jax.experimental.pallas import tpu as pltpu

F32 = jnp.float32
BF = jnp.bfloat16

NORM_EPS = 1e-6
GN_EPS = 64e-5
LRU_C = 8.0
GRID_W = 64
CONV_W = 4
CONV_LEFT = 2

HD = 64
GH = 4
GW = GH * HD
CH = 64
INV_BASE = 16
SCAN_GROUPS = 16
LB = 256
SUBLANES = 8
LANES = 128
HEADS_PER_TILE = LANES // HD
VMEM_LIMIT = 56 * 1024 * 1024
MM_VMEM_BUDGET = 44 * 1024 * 1024
MM_TILE_ROWS = 1024
MM_TN = 1024
NORM_TILE_ROWS = 512
SHIFT_TILE_ROWS = 512
LRU_TILE_ROWS = 1024
LRU_STEP_LANES = 1024


def _cparams(*sem):
    return pltpu.CompilerParams(dimension_semantics=sem, vmem_limit_bytes=VMEM_LIMIT)


def _dot(a, b):
    return jnp.dot(a, b, preferred_element_type=F32)


def _dot_nt(a, b):
    return lax.dot_general(a, b, (((1,), (1,)), ((), ())), preferred_element_type=F32)


def _dot_tn(a, b):
    return lax.dot_general(a, b, (((0,), (0,)), ((), ())), preferred_element_type=F32)


def _split3(x):
    h1 = x.astype(BF)
    r1 = x - h1.astype(F32)
    h2 = r1.astype(BF)
    h3 = (r1 - h2.astype(F32)).astype(BF)
    return h1, h2, h3


def _modnorm(x, g, mod):
    y = x * lax.rsqrt(jnp.mean(x * x, axis=-1, keepdims=True) + NORM_EPS) * g
    return y * (1.0 + mod[1:2]) + mod[0:1]


def _mm_body(*refs, act, a_pre, n_extra):
    a_ref, w_ref, b_ref = refs[:3]
    extras = refs[3:3 + n_extra]
    o_ref = refs[3 + n_extra]
    a = a_ref[...]
    if a_pre is not None:
        a = a_pre(a.astype(F32))
    acc = _dot(a.astype(BF), w_ref[...]) + b_ref[...]
    if act is not None:
        acc = act(acc, *[e[...] for e in extras])
    o_ref[...] = acc.astype(o_ref.dtype)


def _mm_tn(k, n, tm, tn_max, n_extra, out_itemsize):
    tn = min(tn_max, n)
    while tn > LANES:
        w_bytes = 2 * k * tn * 2
        io_bytes = 2 * tm * (k * 2 + tn * (out_itemsize + 4 * n_extra))
        if n % tn == 0 and w_bytes + io_bytes <= MM_VMEM_BUDGET:
            break
        tn //= 2
    return tn


def _mm(a, w3, li, bias, *, tm, tn, m_tiles, extras=(), act=None, a_pre=None, out_dtype=F32, name="mm"):
    _, K, N = w3.shape
    out_bytes = jnp.dtype(out_dtype).itemsize
    tn = _mm_tn(K, N, tm, tn, len(extras), out_bytes)
    if m_tiles % 2 == 0 and _mm_tn(K, N, 2 * tm, tn, len(extras), out_bytes) == tn:
        tm, m_tiles = 2 * tm, m_tiles // 2
    assert N % tn == 0 and K % LANES == 0
    if bias is None:
        bias = jnp.zeros((1, N), F32)
    in_specs = [
        pl.BlockSpec((tm, K), lambda n, m: (m, 0)),
        pl.BlockSpec((None, K, tn), lambda n, m: (li, 0, n)),
        pl.BlockSpec((1, tn), lambda n, m: (0, n)),
    ] + [pl.BlockSpec((tm, tn), lambda n, m: (m, n)) for _ in extras]
    return pl.pallas_call(
        functools.partial(_mm_body, act=act, a_pre=a_pre, n_extra=len(extras)),
        out_shape=jax.ShapeDtypeStruct((m_tiles * tm, N), out_dtype),
        grid=(N // tn, m_tiles),
        in_specs=in_specs,
        out_specs=pl.BlockSpec((tm, tn), lambda n, m: (m, n)),
        compiler_params=_cparams("parallel", "parallel"),
        name=name,
    )(a, w3, bias, *extras)


def _prenorm_body(x_ref, g_ref, mod_ref, o_ref):
    o_ref[...] = _modnorm(x_ref[...], g_ref[...], mod_ref[0]).astype(o_ref.dtype)


def _prenorm(x, g3, layer, mod, *, tm, rows_per_cond):
    rows, d = x.shape
    return pl.pallas_call(
        _prenorm_body,
        out_shape=jax.ShapeDtypeStruct((rows, d), BF),
        grid=(rows // tm,),
        in_specs=[
            pl.BlockSpec((tm, d), lambda i: (i, 0)),
            pl.BlockSpec((None, 1, d), lambda i: (layer, 0, 0)),
            pl.BlockSpec((1, 3, d), lambda i: ((i * tm) // rows_per_cond, 0, 0)),
        ],
        out_specs=pl.BlockSpec((tm, d), lambda i: (i, 0)),
        compiler_params=_cparams("parallel"),
        name="prenorm",
    )(x, g3, mod)


def _postnorm_body(x_ref, m_ref, g_ref, mod_ref, o_ref):
    m = m_ref[...]
    y = m * lax.rsqrt(jnp.mean(m * m, axis=-1, keepdims=True) + NORM_EPS) * g_ref[...]
    o_ref[...] = x_ref[...] + mod_ref[0][2:3] * y


def _postnorm(x, m, g3, layer, mod, *, tm, rows_per_cond):
    rows, d = x.shape
    return pl.pallas_call(
        _postnorm_body,
        out_shape=jax.ShapeDtypeStruct((rows, d), F32),
        grid=(rows // tm,),
        in_specs=[
            pl.BlockSpec((tm, d), lambda i: (i, 0)),
            pl.BlockSpec((tm, d), lambda i: (i, 0)),
            pl.BlockSpec((None, 1, d), lambda i: (layer, 0, 0)),
            pl.BlockSpec((1, 3, d), lambda i: ((i * tm) // rows_per_cond, 0, 0)),
        ],
        out_specs=pl.BlockSpec((tm, d), lambda i: (i, 0)),
        compiler_params=_cparams("parallel"),
        name="postnorm",
    )(x, m, g3, mod)


def _write_mixes(h, shifted, mu_ref, out_refs):
    xx = shifted - h
    out_refs[0][...] = h.astype(BF)
    for m in range(5):
        out_refs[m + 1][...] = (h + xx * mu_ref[m:m + 1]).astype(BF)


def _shiftmix_seq_body(x_ref, g_ref, mod_ref, mu_ref, *out_refs):
    h = _modnorm(x_ref[...], g_ref[...], mod_ref[0])
    t_len, d = h.shape
    half = d // 2
    t = lax.broadcasted_iota(jnp.int32, (t_len, 1), 0)
    prev = jnp.where(t == 0, 0.0, pltpu.roll(h[:, :half], 1, 0))
    nxt = jnp.where(t == t_len - 1, 0.0, pltpu.roll(h[:, half:], t_len - 1, 0))
    _write_mixes(h, jnp.concatenate([prev, nxt], axis=1), mu_ref, out_refs)


def _shiftmix_grid_body(x_ref, up_ref, dn_ref, g_ref, mod_ref, mu_ref, *out_refs, tiles_per_batch):
    ti = pl.program_id(0) % tiles_per_batch
    g = g_ref[...]
    mod = mod_ref[0]
    h = _modnorm(x_ref[...], g, mod)
    tr, d = h.shape
    q = d // 4
    h_up = jnp.where(ti > 0, _modnorm(up_ref[...], g, mod)[:, 2 * q:3 * q], 0.0)
    h_dn = jnp.where(ti < tiles_per_batch - 1, _modnorm(dn_ref[...], g, mod)[:, 3 * q:], 0.0)
    col = lax.broadcasted_iota(jnp.int32, (tr, 1), 0) % GRID_W
    left = jnp.where(col == 0, 0.0, pltpu.roll(h[:, :q], 1, 0))
    right = jnp.where(col == GRID_W - 1, 0.0, pltpu.roll(h[:, q:2 * q], tr - 1, 0))
    up = jnp.concatenate([h_up, h[:tr - GRID_W, 2 * q:3 * q]], axis=0)
    down = jnp.concatenate([h[GRID_W:, 3 * q:], h_dn], axis=0)
    _write_mixes(h, jnp.concatenate([left, right, up, down], axis=1), mu_ref, out_refs)


def _shiftmix(x, g3, layer, mod, mu, *, seq_len, cond_rows, grid_mode, tm):
    rows, d = x.shape
    outs = [jax.ShapeDtypeStruct((rows, d), BF)] * 6
    common = [
        pl.BlockSpec((None, 1, d), lambda i: (layer, 0, 0)),
        pl.BlockSpec((1, 3, d), lambda i: ((i * tm) // cond_rows, 0, 0)),
        pl.BlockSpec((5, d), lambda i: (0, 0)),
    ]
    out_specs = [pl.BlockSpec((tm, d), lambda i: (i, 0))] * 6
    if not grid_mode:
        assert tm == seq_len
        return pl.pallas_call(
            _shiftmix_seq_body, out_shape=outs, grid=(rows // tm,),
            in_specs=[pl.BlockSpec((tm, d), lambda i: (i, 0))] + common,
            out_specs=out_specs, compiler_params=_cparams("parallel"), name="shiftmix_seq",
        )(x, g3, mod, mu)
    per = tm // GRID_W
    n_halo = rows // GRID_W
    return pl.pallas_call(
        functools.partial(_shiftmix_grid_body, tiles_per_batch=seq_len // tm),
        out_shape=outs, grid=(rows // tm,),
        in_specs=[
            pl.BlockSpec((tm, d), lambda i: (i, 0)),
            pl.BlockSpec((GRID_W, d), lambda i: (jnp.maximum(i * per - 1, 0), 0)),
            pl.BlockSpec((GRID_W, d), lambda i: (jnp.minimum((i + 1) * per, n_halo - 1), 0)),
        ] + common,
        out_specs=out_specs, compiler_params=_cparams("parallel"), name="shiftmix_grid",
    )(x, x, x, g3, mod, mu)


def _lru_gates_scan(xc, gw_ref, gb_ref, lam_ref, a_s, u_s, h, *, rb, reverse):
    tr = xc.shape[0]
    for q in range(xc.shape[1] // LB):
        ln = slice(q * LB, (q + 1) * LB)
        xq = xc[:, ln]
        xb = xq.astype(BF)

        def gate(n):
            return 0.5 * jnp.tanh(0.5 * (_dot(xb, gw_ref[n, q].astype(BF)) + gb_ref[n:n + 1, ln])) + 0.5

        r = gate(0)
        i = gate(1)
        log_a = r * (-LRU_C * jax.nn.softplus(-lam_ref[:, ln]))
        a_s[:, ln] = jnp.exp(log_a)
        th = jnp.tanh(log_a)
        u_s[:, ln] = jnp.sqrt(-2.0 * th / (1.0 - th)) * (i * xq)

    if rb % SUBLANES == 0:
        n_steps = tr // rb

        def body(s, h):
            row = pl.multiple_of(((n_steps - 1 - s) if reverse else s) * rb, rb)
            h = a_s[pl.ds(row, rb), :] * h + u_s[pl.ds(row, rb), :]
            u_s[pl.ds(row, rb), :] = h
            return h

        h = lax.fori_loop(0, n_steps, body, h, unroll=4)
    else:
        assert 2 * rb == SUBLANES
        n_steps = tr // SUBLANES
        first = lax.broadcasted_iota(jnp.int32, (SUBLANES, 1), 0) < rb

        def body(s, h):
            row = pl.multiple_of(((n_steps - 1 - s) if reverse else s) * SUBLANES, SUBLANES)
            a = a_s[pl.ds(row, SUBLANES), :]
            u = u_s[pl.ds(row, SUBLANES), :]
            h1 = a * h + u
            h2 = a * pltpu.roll(h1, rb, 0) + u
            u_s[pl.ds(row, SUBLANES), :] = jnp.where(first != reverse, h1, h2)
            return pltpu.roll(h2, rb, 0)

        h = lax.fori_loop(0, n_steps, body, h, unroll=4)
    return h


def _lru_rows_body(*refs, nb, reverse, combine, n_tiles):
    if combine:
        (x_ref, xp_ref, xn_ref, cw_ref, cb_ref, gw_ref, gb_ref, lam_ref, h0_ref, yp_ref, z_ref,
         y_ref, hf_ref, xs, a_s, u_s, hc, zs) = refs
    else:
        (x_ref, xp_ref, xn_ref, cw_ref, cb_ref, gw_ref, gb_ref, lam_ref, h0_ref,
         y_ref, hf_ref, xs, a_s, u_s, hc) = refs
    tt = x_ref.shape[1]
    ht = xp_ref.shape[1]
    tr, hb = tt * nb, ht * nb
    tiles = [slice(t * LANES, (t + 1) * LANES) for t in range(x_ref.shape[2] // LANES)]
    step = pl.program_id(2)
    t_idx = (n_tiles - 1 - step) if reverse else step

    @pl.when(step == 0)
    def _():
        hc[...] = h0_ref[...]

    for b in range(nb):
        for t, ln in enumerate(tiles):
            xs[t, pl.ds(hb + b, tt, stride=nb), :] = x_ref[b, :, ln].astype(F32)
            xs[t, pl.ds(b, ht, stride=nb), :] = jnp.where(t_idx > 0, xp_ref[b, :, ln].astype(F32), 0.0)
            xs[t, pl.ds(hb + tr + b, ht, stride=nb), :] = jnp.where(
                t_idx < n_tiles - 1, xn_ref[b, :, ln].astype(F32), 0.0)
    parts = []
    for t, ln in enumerate(tiles):
        acc = cb_ref[:, ln]
        for tap in range(CONV_W):
            off = hb + (tap - CONV_LEFT) * nb
            acc = acc + xs[t, off:off + tr, :] * cw_ref[tap:tap + 1, ln]
        parts.append(acc)
    xc = jnp.concatenate(parts, axis=1)
    h = _lru_gates_scan(xc, gw_ref, gb_ref, lam_ref, a_s, u_s, hc[...], rb=nb, reverse=reverse)
    hc[...] = h
    hf_ref[...] = h
    if not combine:
        y_ref[...] = u_s[...]
        return
    for t, ln in enumerate(tiles):
        zs[t] = yp_ref[:, ln] + u_s[:, ln]
    for b in range(nb):
        for t, ln in enumerate(tiles):
            z = z_ref[b, :, ln].astype(F32)
            y_b = zs[t, pl.ds(b, tt, stride=nb), :]
            y_ref[b, :, ln] = (y_b * (z * jax.nn.sigmoid(z))).astype(y_ref.dtype)


def _lru_core_rows(xz3, lru, j, d, h0, *, y_prev=None):
    batch, seq_len, e2 = xz3.shape
    e = e2 // 2
    nb = min(batch, SUBLANES)
    n_groups = batch // nb
    tt = min(LRU_TILE_ROWS // nb, seq_len)
    n_tiles = seq_len // tt
    ht = 2 * SUBLANES
    per = tt // ht
    n_halo = seq_len // ht
    tr, hb = tt * nb, ht * nb
    hr = h0.shape[0] // n_groups
    reverse = d == 1
    combine = y_prev is not None

    def tidx(s):
        return (n_tiles - 1 - s) if reverse else s

    sw = min(LRU_STEP_LANES, e)
    nsub = sw // LB
    in_specs = [
        pl.BlockSpec((nb, tt, sw), lambda c, g, s: (g, tidx(s), c)),
        pl.BlockSpec((nb, ht, sw), lambda c, g, s: (g, jnp.maximum(tidx(s) * per - 1, 0), c)),
        pl.BlockSpec((nb, ht, sw), lambda c, g, s: (g, jnp.minimum((tidx(s) + 1) * per, n_halo - 1), c)),
        pl.BlockSpec((None, CONV_W, sw), lambda c, g, s: (j, 0, c)),
        pl.BlockSpec((None, 1, sw), lambda c, g, s: (j, 0, c)),
        pl.BlockSpec((None, 2, nsub, LB, LB), lambda c, g, s: (j * 2 + d, 0, c, 0, 0)),
        pl.BlockSpec((None, 2, sw), lambda c, g, s: (j * 2 + d, 0, c)),
        pl.BlockSpec((None, 1, sw), lambda c, g, s: (j * 2 + d, 0, c)),
        pl.BlockSpec((hr, sw), lambda c, g, s: (g, c)),
    ]
    args = [xz3, xz3, xz3, lru["conv_w"], lru["conv_b"], lru["gate_w"], lru["gate_b"], lru["lam"], h0]
    lt = sw // LANES
    scratch = [pltpu.VMEM((lt, tr + 2 * hb, LANES), F32), pltpu.VMEM((tr, sw), F32),
               pltpu.VMEM((tr, sw), F32), pltpu.VMEM((hr, sw), F32)]
    if combine:
        in_specs += [
            pl.BlockSpec((None, tr, sw), lambda c, g, s: (g, tidx(s), c)),
            pl.BlockSpec((nb, tt, sw), lambda c, g, s: (g, tidx(s), e // sw + c)),
        ]
        args += [y_prev, xz3]
        scratch.append(pltpu.VMEM((lt, tr, LANES), F32))
        y_shape = jax.ShapeDtypeStruct((batch, seq_len, e), BF)
        y_spec = pl.BlockSpec((nb, tt, sw), lambda c, g, s: (g, tidx(s), c))
    else:
        y_shape = jax.ShapeDtypeStruct((n_groups, seq_len * nb, e), F32)
        y_spec = pl.BlockSpec((None, tr, sw), lambda c, g, s: (g, tidx(s), c))
    return pl.pallas_call(
        functools.partial(_lru_rows_body, nb=nb, reverse=reverse, combine=combine, n_tiles=n_tiles),
        out_shape=[y_shape, jax.ShapeDtypeStruct((hr * n_groups, e), F32)],
        grid=(e // sw, n_groups, n_tiles),
        in_specs=in_specs,
        out_specs=[y_spec, pl.BlockSpec((hr, sw), lambda c, g, s: (g, c))],
        scratch_shapes=scratch,
        compiler_params=_cparams("parallel", "parallel", "arbitrary"),
        name="lru_core_rows",
    )(*args)


def _rwkv_scan_body(*refs, reverse, zero_init, want_state, finish, n_chunks):
    it = iter(refs)
    (r_ref, k_ref, v_ref, tw_ref, ta_ref, w2_ref, w0_ref, a2_ref, a0_ref, kp_ref, bm_ref,
     tri_ref, cat_ref, lvl_ref) = (next(it) for _ in range(14))
    if finish:
        tao_ref, a2o_ref, a0o_ref, yo_ref, zg_ref, hp_ref = (next(it) for _ in range(6))
    sel_ref = next(it) if want_state else None
    z0_ref = None if zero_init else next(it)
    y_ref = next(it)
    sf_ref = next(it) if want_state else None
    z_s = next(it)
    c = pl.program_id(2)

    @pl.when(c == 0)
    def _():
        z_s[...] = jnp.zeros_like(z_s) if zero_init else z0_ref[...]

    lw_all = _act_logdecay(_dot(tw_ref[...], w2_ref[...]) + w0_ref[...])
    as_all = jax.nn.sigmoid(_dot(ta_ref[...], a2_ref[...]) + a0_ref[...])

    groups = range(z_s.shape[0])
    lanes = [slice(g * GW, (g + 1) * GW) for g in groups]
    bm = bm_ref[...]
    rs = [r_ref[:, l] for l in lanes]
    ks = [k_ref[:, l] for l in lanes]
    vs = [v_ref[:, l] for l in lanes]
    ys, z_new = _rwkv_chunk(
        rs, ks, vs, [lw_all[:, l] for l in lanes], [as_all[:, l] for l in lanes],
        [kp_ref[:, l] for l in lanes], bm, tri_ref[...], cat_ref, lvl_ref,
        [[z_s[g, t] for t in range(z_s.shape[1])] for g in groups], reverse=reverse)
    for g in groups:
        for t in range(z_s.shape[1]):
            z_s[g, t] = z_new[g][t]
    if not finish:
        for g in groups:
            y_ref[:, lanes[g]] = ys[g]
    else:
        bm_b = bm.astype(BF)
        aso_all = jax.nn.sigmoid(_dot(tao_ref[...], a2o_ref[...]) + a0o_ref[...])

        def headsum(x):
            return sum(_dot(p, bm_b) for p in _split3(x)[:2])

        ysum = [ys[g] + yo_ref[:, lanes[g]] for g in groups]
        arg = [rs[g] * ks[g] * hp_ref[1:2, lanes[g]]
               * (2.0 + hp_ref[0:1, lanes[g]] * (as_all[:, lanes[g]] + aso_all[:, lanes[g]] - 2.0))
               for g in groups]
        sums = [headsum(jnp.concatenate([ysum[g], arg[g]], axis=0)) for g in groups]
        yc = [ysum[g] - sums[g][:CH] * (1.0 / HD) for g in groups]
        var = [headsum(yc[g] * yc[g]) * (1.0 / HD) for g in groups]
        for g in groups:
            out = (yc[g] * lax.rsqrt(var[g] + GN_EPS) * hp_ref[2:3, lanes[g]] + hp_ref[3:4, lanes[g]]
                   + sums[g][CH:] * vs[g])
            zg = zg_ref[:, lanes[g]]
            y_ref[:, lanes[g]] = (out * (zg * jax.nn.sigmoid(zg))).astype(y_ref.dtype)
    if want_state:
        @pl.when(c == n_chunks - 1)
        def _():
            sel = sel_ref[...]
            for g in groups:
                for t in range(z_s.shape[1]):
                    sf_ref[g * GW + t * LANES:g * GW + (t + 1) * LANES, :] = sum(
                        _dot_tn(p, sel) for p in _split3(z_new[g][t])[:2])


def _vmap(fn, *lists):
    return [fn(*args) for args in zip(*lists)]


def _rwkv_chunk(r, k, v, lw, asg, kp, bm, tri, cat_ref, lvl_ref, z, *, reverse):
    incl = cat_ref[0]
    strict = cat_ref[1]
    eye_c = incl - strict
    bm_b = bm.astype(BF)

    def headsum(x):
        return sum(_dot(p, bm_b) for p in _split3(x)[:2])

    lane_head = lax.broadcasted_iota(jnp.int32, (1, LANES), 1) // HD
    zero_tile = jnp.zeros((CH, LANES), BF)

    def bdm(x):
        xb = x.astype(BF)
        blocks = []
        for h in range(GH):
            j = h // HEADS_PER_TILE
            tile = xb[:, j * LANES:(j + 1) * LANES]
            keep = jnp.where(lane_head == h % HEADS_PER_TILE, tile, jnp.zeros_like(tile))
            blocks.append(jnp.concatenate([keep if t == j else zero_tile for t in range(GW // LANES)],
                                          axis=1))
        return jnp.concatenate(blocks, axis=0)

    def mm(a, b_bd):
        return _vmap(lambda x, y: _dot(x.astype(BF), y), a, b_bd)

    def stack2(a, b):
        return _vmap(lambda x, y: jnp.concatenate([x, y], axis=0), a, b)

    def top(a):
        return [x[:CH] for x in a]

    def bot(a):
        return [x[CH:] for x in a]

    kk = _vmap(lambda k_, p: k_ * p[0:1], k, kp)
    ss = _vmap(lambda x: headsum(x * x), kk)
    kk = _vmap(lambda x, s: x * lax.rsqrt(jnp.maximum(s, 1e-24)), kk, ss)
    kd = _vmap(lambda k_, a, p: k_ * (1.0 + (a - 1.0) * p[1:2]), k, asg, kp)
    bvec = _vmap(lambda x, a: x * a, kk, asg)

    cum = _vmap(lambda x: sum(_dot(tri, p) for p in _split3(x)[:2]), lw)
    tot = _vmap(lambda x: x[0:1] if reverse else x[CH - 1:CH], cum)
    e_pos = _vmap(jnp.exp, cum)
    e_neg = _vmap(lambda x: jnp.exp(-x), cum)
    e_rem = _vmap(lambda t, en: jnp.exp(t) * en, tot, e_neg)
    a_t = _vmap(lambda x, ep, l: -x * (ep * jnp.exp(-l)), kk, e_pos, lw)
    r_t = _vmap(lambda x, ep: x * ep, r, e_pos)
    ar = [x.astype(BF) for x in stack2(a_t, r_t)]
    bt_bd = _vmap(lambda x, en: bdm(x * en), bvec, e_neg)
    kt_bd = _vmap(lambda x, en: bdm(x * en), kd, e_neg)
    keep2 = jnp.concatenate([strict, incl], axis=0) > 0
    ab = [jnp.where(keep2, m, 0.0) for m in _vmap(_dot_nt, ar, bt_bd)]
    ak = [jnp.where(keep2, m, 0.0).astype(BF) for m in _vmap(_dot_nt, ar, kt_bd)]
    a_ab = top(ab)
    a_rb = [m.astype(BF) for m in bot(ab)]

    p = [n * lvl_ref[0] for n in a_ab]
    x = [eye_c + n for n in p]
    p = [n.astype(BF) for n in p]
    pw = [m.astype(BF) for m in mm(p, _vmap(bdm, p))]
    e = 2
    while e < INV_BASE:
        rhs = _vmap(bdm, pw)
        if 2 * e < INV_BASE:
            both = mm(stack2([m.astype(BF) for m in x], pw), rhs)
            x = _vmap(jnp.add, x, top(both))
            pw = [m.astype(BF) for m in bot(both)]
        else:
            x = _vmap(jnp.add, x, mm(x, rhs))
        e *= 2
    for lvl in range(1, lvl_ref.shape[0]):
        off = [(n * lvl_ref[lvl]).astype(BF) for n in a_ab]
        xb = [m.astype(BF) for m in x]
        x = _vmap(jnp.add, x, mm(xb, _vmap(bdm, mm(off, _vmap(bdm, xb)))))
    xb = [m.astype(BF) for m in x]

    vb = [m.astype(BF) for m in v]
    s_v = mm(ak, _vmap(bdm, vb))
    def z_rhs(zq):
        n_t = len(zq)
        zero = jnp.zeros((LANES, LANES), BF)
        return jnp.concatenate(
            [jnp.concatenate([zq[i].astype(BF) if i == t else zero for t in range(n_t)], axis=1)
             for i in range(n_t)], axis=0)

    s_z = _vmap(lambda a, zq: _dot(a, z_rhs(zq)), ar, z)
    u = mm(xb, _vmap(lambda a, b_: bdm(a + b_), top(s_z), top(s_v)))
    ub = [m.astype(BF) for m in u]
    y =_vmap(lambda a, b_, c_: a + b_ + c_, bot(s_z), bot(s_v), mm(a_rb, _vmap(bdm, ub)))
    row_id = lax.broadcasted_iota(jnp.int32, (LANES, 1), 0)
    lane_id = lax.broadcasted_iota(jnp.int32, (1, LANES), 1)
    same_head = (row_id // HD) == (lane_id // HD)
    diag = row_id == lane_id
    lhs = _vmap(lambda b_, k_, er: jnp.concatenate([b_ * er, k_ * er], axis=0).astype(BF), bvec, kd, e_rem)
    rhs = stack2(ub, vb)
    decay = _vmap(jnp.exp, tot)
    z_new = [[] for _ in z]
    for t in range(GW // LANES):
        cols = slice(t * LANES, (t + 1) * LANES)
        for g in range(len(z)):
            upd = jnp.where(same_head, _dot_tn(lhs[g][:, cols], rhs[g][:, cols]), 0.0)
            g_col = jnp.sum(jnp.where(diag, decay[g][:, cols], 0.0), axis=1, keepdims=True)
            z_new[g].append(z[g][t] * g_col + upd)
    return y, z_new


def _rwkv_scan(r, k, v, tw, ta, lora, jd, kp, consts, d, *, batch, seq_len, z0=None, want_state=False,
               finish=None):
    rows, e = r.shape
    ngs = min(SCAN_GROUPS, e // GW)
    sw = ngs * GW
    ng = e // sw
    nc = seq_len // CH
    reverse = d == 1
    zero_init = z0 is None
    rank = lora["w2"].shape[1]

    def cidx(c):
        return (nc - 1 - c) if reverse else c

    def lora_specs(dd, jdd):
        return [pl.BlockSpec((CH, rank), lambda b, g, c: (b * nc + cidx(c), dd)),
                pl.BlockSpec((None, rank, sw), lambda b, g, c: (jdd, 0, g)),
                pl.BlockSpec((None, 1, sw), lambda b, g, c: (jdd, 0, g))]

    tok = pl.BlockSpec((CH, sw), lambda b, g, c: (b * nc + cidx(c), g))
    w_specs, a_specs = lora_specs(d, jd), lora_specs(d, jd)
    in_specs = [tok, tok, tok, w_specs[0], a_specs[0], w_specs[1], w_specs[2], a_specs[1], a_specs[2],
                pl.BlockSpec((2, sw), lambda b, g, c: (0, g)),
                pl.BlockSpec((GW, GW), lambda b, g, c: (0, 0)),
                pl.BlockSpec((None, CH, CH), lambda b, g, c: (d, 0, 0)),
                pl.BlockSpec((None, 2, CH, GW), lambda b, g, c: (d, 0, 0, 0)),
                pl.BlockSpec(consts["lvl"].shape, lambda b, g, c: (0, 0, 0))]
    args = [r, k, v, tw, ta, lora["w2"], lora["w0"], lora["a2"], lora["a0"], kp, consts["bm"],
            consts["tri"], consts["cat"], consts["lvl"]]
    if finish is not None:
        y_other, zg, hp = finish
        in_specs += lora_specs(1 - d, jd + 1 - 2 * d) + [tok, tok, pl.BlockSpec((4, sw), lambda b, g, c: (0, g))]
        args += [ta, lora["a2"], lora["a0"], y_other, zg, hp]
    if want_state:
        in_specs.append(pl.BlockSpec((LANES, HD), lambda b, g, c: (0, 0)))
        args.append(consts["sel"])
    if not zero_init:
        in_specs.append(pl.BlockSpec((None, ngs, GW // LANES, LANES, LANES), lambda b, g, c: (b, g, 0, 0, 0)))
        args.append(z0)
    out_shape = [jax.ShapeDtypeStruct((rows, e), F32 if finish is None else BF)]
    out_specs = [tok]
    if want_state:
        out_shape.append(jax.ShapeDtypeStruct((batch, e, HD), F32))
        out_specs.append(pl.BlockSpec((None, sw, HD), lambda b, g, c: (b, g, 0)))
    res = pl.pallas_call(
        functools.partial(_rwkv_scan_body, reverse=reverse, zero_init=zero_init,
                          want_state=want_state, finish=finish is not None, n_chunks=nc),
        out_shape=out_shape,
        grid=(batch, ng, nc),
        in_specs=in_specs,
        out_specs=out_specs,
        scratch_shapes=[pltpu.VMEM((ngs, GW // LANES, LANES, LANES), F32)],
        compiler_params=_cparams("parallel", "parallel", "arbitrary"),
        name="rwkv_scan",
    )(*args)
    return res if want_state else (res[0], None)


def _silu(x):
    return x * jax.nn.sigmoid(x)


def _act_tanh(x):
    return jnp.tanh(x)


def _act_sigmoid(x):
    return jax.nn.sigmoid(x)


def _act_logdecay(x):
    return -jnp.exp(-0.5) * jax.nn.sigmoid(x)


def _act_vmix(acc, v_first, mix):
    return acc + (v_first - acc) * mix


def _trunk(x, mods, init_lru, init_rwkv, W, *, batch, seq_len, grid_mode, want_state):
    rows, d = x.shape
    e = W["lru_w_out"].shape[1]
    depth = W["norm_pre"].shape[0]
    cond_rows = rows // mods[0].shape[0]
    tm_row = min(NORM_TILE_ROWS, cond_rows)
    tm_mm = min(MM_TILE_ROWS, rows)
    mk = dict(tm=tm_mm, m_tiles=rows // tm_mm)
    fin_lru, fin_rwkv = [], []
    v_first = None

    for i in range(depth):
        mod = mods[i]
        j = i // 2
        if i % 2 == 0:
            h = _prenorm(x, W["norm_pre"], i, mod, tm=tm_row, rows_per_cond=cond_rows)
            xz = _mm(h, W["lru_w_in"], j, None, tn=MM_TN, out_dtype=BF, name="lru_in", **mk)
            xz = xz.reshape(batch, seq_len, 2 * e)
            y0, hf0 = _lru_core_rows(xz, W["lru"], j, 0, init_lru[j][0])
            y, hf1 = _lru_core_rows(xz, W["lru"], j, 1, init_lru[j][1], y_prev=y0)
            m = _mm(y.reshape(rows, e), W["lru_w_out"], j, None, tn=MM_TN, name="lru_out", **mk)
            fin_lru.append((hf0, hf1))
        else:
            h, xr, xw, xk, xv, xa = _shiftmix(
                x, W["norm_pre"], i, mod, W["rwkv_mu"][j], seq_len=seq_len, cond_rows=cond_rows, grid_mode=grid_mode,
                tm=seq_len if not grid_mode else min(SHIFT_TILE_ROWS, seq_len))
            r = _mm(xr, W["rwkv_w_r"], j, None, tn=MM_TN, name="rwkv_r", **mk)
            k = _mm(xk, W["rwkv_w_k"], j, None, tn=MM_TN, name="rwkv_k", **mk)
            z = _mm(h, W["rwkv_w_g"], j, None, tn=MM_TN, name="rwkv_g", **mk)
            if j == 0:
                v = _mm(xv, W["rwkv_w_v"], j, None, tn=MM_TN, name="rwkv_v", **mk)
                v_first = v
            else:
                tv = _mm(xv, W["rwkv_v1"], j - 1, None, tn=MM_TN, out_dtype=BF, name="rwkv_v1", **mk)
                mix = _mm(tv, W["rwkv_v2"], j - 1, W["rwkv_v0"][j - 1], tn=MM_TN, act=_act_sigmoid,
                          name="rwkv_v2", **mk)
                v = _mm(xv, W["rwkv_w_v"], j, None, tn=MM_TN, extras=(v_first, mix), act=_act_vmix,
                        name="rwkv_v", **mk)
            tw = _mm(xw, W["rwkv_w1"], j, None, tn=MM_TN, act=_act_tanh, out_dtype=BF, name="rwkv_w1", **mk)
            ta = _mm(xa, W["rwkv_a1"], j, None, tn=MM_TN, out_dtype=BF, name="rwkv_a1", **mk)
            fins = []
            y_d = None
            for dd in range(2):
                z0 = None if init_rwkv is None else init_rwkv[j][dd]
                finish = None if dd == 0 else (y_d, z, W["rwkv_hp"][j])
                y_d, s_d = _rwkv_scan(r, k, v, tw, ta, W["rwkv_lora"], j * 2 + dd, W["rwkv_kp"][j],
                                      W["consts"], dd, batch=batch, seq_len=seq_len, z0=z0,
                                      want_state=want_state, finish=finish)
                fins.append(s_d)
            fin_rwkv.append(fins)
            m = _mm(y_d, W["rwkv_w_o"], j, None, tn=MM_TN, name="rwkv_o", **mk)
        x = _postnorm(x, m, W["norm_post"], i, mod, tm=tm_row, rows_per_cond=cond_rows)
    return x, fin_lru, fin_rwkv


def _scan_consts():
    idx = jnp.arange(GW)
    bm = ((idx[:, None] // HD) == (idx[None, :] // HD)).astype(F32)
    t_r, t_c = jnp.arange(CH)[:, None], idx[None, :] % CH
    cats, tris = [], []
    ic = jnp.arange(CH)
    for d in range(2):
        ge = (t_r >= t_c) if d == 0 else (t_r <= t_c)
        gt = (t_r > t_c) if d == 0 else (t_r < t_c)
        cats.append(jnp.stack([ge.astype(F32), gt.astype(F32)]))
        tris.append(((ic[:, None] >= ic[None, :]) if d == 0 else (ic[:, None] <= ic[None, :])).astype(BF))
    sel = (idx[:, None] % HD == jnp.arange(HD)[None, :]).astype(BF)
    lvl = [t_r // INV_BASE == t_c // INV_BASE]
    size = INV_BASE
    while size < CH:
        lvl.append((t_r // (2 * size) == t_c // (2 * size)) & (t_r // size != t_c // size))
        size *= 2
    return dict(bm=bm, cat=jnp.stack(cats), tri=jnp.stack(tris), sel=sel,
                lvl=jnp.stack(lvl).astype(F32))


def kernel(x_prompt, x_sample, state_lru, state_rwkv, c, c_ctx, ada_w, ada_b, norm_pre, norm_post, lru_w_in, lru_conv_w, lru_conv_b, lru_gate_w, lru_gate_b, lru_lambda, lru_w_out, rwkv_mu, rwkv_w_r, rwkv_w_k, rwkv_w_v, rwkv_w_g, rwkv_w_o, rwkv_w0, rwkv_w1, rwkv_w2, rwkv_a0, rwkv_a1, rwkv_a2, rwkv_k_k, rwkv_k_a, rwkv_r_k, rwkv_ln_w, rwkv_ln_b, rwkv_v0, rwkv_v1, rwkv_v2):
    bp, tp, d = x_prompt.shape
    bs, ts, _ = x_sample.shape
    depth = ada_w.shape[0]
    n_lru, _, e2 = lru_w_in.shape
    e = e2 // 2
    n_rwkv = rwkv_w_r.shape[0]
    heads = e // HD
    ng = e // GW

    def merge_lead(w, k):
        return w.reshape((-1,) + w.shape[k:])

    lora_v = rwkv_v1.shape[-1]
    pad_v = (-lora_v) % 128
    W = dict(
        norm_pre=norm_pre.reshape(depth, 1, d), norm_post=norm_post.reshape(depth, 1, d),
        lru_w_in=lru_w_in.astype(BF), lru_w_out=lru_w_out.astype(BF),
        lru=dict(conv_w=lru_conv_w, conv_b=lru_conv_b.reshape(n_lru, 1, e),
                 gate_w=merge_lead(lru_gate_w, 2),
                 gate_b=merge_lead(lru_gate_b, 2),
                 lam=lru_lambda.reshape(n_lru * 2, 1, e)),
        rwkv_mu=rwkv_mu,
        rwkv_w_r=rwkv_w_r.astype(BF), rwkv_w_k=rwkv_w_k.astype(BF), rwkv_w_v=rwkv_w_v.astype(BF),
        rwkv_w_g=rwkv_w_g.astype(BF), rwkv_w_o=rwkv_w_o.astype(BF),
        rwkv_w1=jnp.concatenate([rwkv_w1[:, 0], rwkv_w1[:, 1]], axis=-1).astype(BF),
        rwkv_a1=jnp.concatenate([rwkv_a1[:, 0], rwkv_a1[:, 1]], axis=-1).astype(BF),
        rwkv_lora=dict(w2=merge_lead(rwkv_w2, 2).astype(BF), a2=merge_lead(rwkv_a2, 2).astype(BF),
                       w0=rwkv_w0.reshape(n_rwkv * 2, 1, e), a0=rwkv_a0.reshape(n_rwkv * 2, 1, e)),
        rwkv_v0=rwkv_v0.reshape(-1, 1, e),
        rwkv_v1=jnp.pad(rwkv_v1, ((0, 0), (0, 0), (0, pad_v))).astype(BF),
        rwkv_v2=jnp.pad(rwkv_v2, ((0, 0), (0, pad_v), (0, 0))).astype(BF),
        rwkv_kp=jnp.stack([rwkv_k_k, rwkv_k_a], axis=1),
        rwkv_hp=jnp.stack([rwkv_k_a, rwkv_r_k, rwkv_ln_w, rwkv_ln_b], axis=1),
        consts=_scan_consts(),
    )

    n_cond = 1 + bs
    cond = jnp.zeros((SUBLANES * pl.cdiv(n_cond, SUBLANES), d), F32)
    cond = cond.at[0].set(c_ctx).at[1:n_cond].set(c)
    ada_wb = ada_w.astype(BF)
    mods_p, mods_s = [], []
    for i in range(depth):
        mod = _mm(cond, ada_wb, i, ada_b[i][None], tm=cond.shape[0], tn=256, m_tiles=1,
                  a_pre=_silu, name="adaln")
        mod = mod.reshape(cond.shape[0], 3, d)
        mods_p.append(mod[0:1])
        mods_s.append(mod[1:n_cond])

    zero_lru = [(jnp.zeros((bp, e), F32), jnp.zeros((bp, e), F32))] * n_lru
    y_p, fin_lru, fin_rwkv = _trunk(x_prompt.reshape(bp * tp, d), mods_p, zero_lru, None, W,
                                    batch=bp, seq_len=tp, grid_mode=False, want_state=True)
    new_state_lru = jnp.stack([jnp.stack(f, axis=1) for f in fin_lru], axis=1)
    new_state_rwkv = jnp.stack(
        [jnp.stack([s.reshape(bp, heads, HD, HD) for s in f], axis=1) for f in fin_rwkv], axis=1)

    rep = max(SUBLANES // bs, 1)
    init_lru = [tuple(jnp.tile(state_lru[:, jj, dd], (rep, 1)) for dd in range(2)) for jj in range(n_lru)]
    eye_t = jnp.eye(HEADS_PER_TILE, dtype=F32)
    n_tiles = e // LANES
    st = state_rwkv.reshape(bs, n_rwkv, 2, n_tiles, HEADS_PER_TILE, HD, HD)
    z0 = jnp.einsum("bldthvk,hi->bldthkiv", st, eye_t).reshape(
        bs, n_rwkv, 2, ng, GW // LANES, LANES, LANES)
    init_rwkv = [[z0[:, jj, dd] for dd in range(2)] for jj in range(n_rwkv)]
    y_s, _, _ = _trunk(x_sample.reshape(bs * ts, d), mods_s, init_lru, init_rwkv, W,
                       batch=bs, seq_len=ts, grid_mode=True, want_state=False)
    return (y_p.reshape(bp, tp, d), y_s.reshape(bs, ts, d), new_state_lru, new_state_rwkv)
```
